```python
import jax
import jax.numpy as jnp
from jax import lax
import numpy as np

D_MODEL = 1024
BATCH = 4
SEQ = 4096
DEPTH = 2

GRID_W = 64
CTX_LEN = 256
N_GROUPS = 4
GROUP_W = D_MODEL // N_GROUPS
EPS = 1e-6

FNET_GROUPS = 4
FNET_CH = GROUP_W // FNET_GROUPS

LRU_HEADS = 4
LRU_HEAD_DIM = GROUP_W // LRU_HEADS
LRU_CONV = 4
LRU_C = 8.0

MLA_HEADS = 4
MLA_NOPE = 64
MLA_ROPE = 32
MLA_V = GROUP_W // MLA_HEADS
MLA_Q_LORA = 256
MLA_KV_LORA = 128
ROPE_BASE = 10000.0
Q_BLOCK = 128
ATTN_SCALE = (MLA_NOPE + MLA_ROPE) ** -0.5

SC_CONV = 3

N_EXPERTS = 32
TOP_K = 4
D_EXPERT = D_MODEL
SWIGLU_ALPHA = 1.702
SWIGLU_LIMIT = 7.0
MOE_BLOCK = 128

PROJ_SIZES = (GROUP_W, GROUP_W, GROUP_W, MLA_Q_LORA, MLA_KV_LORA, MLA_ROPE, GROUP_W, GROUP_W, GROUP_W)
P_TOTAL = 6 * GROUP_W + MLA_Q_LORA + MLA_KV_LORA + MLA_ROPE

kernel_name = 'hybrid_parallel_group_diffusion_block'


def rms_norm(x, g):
    x32 = x.astype(jnp.float32)
    y = x32 * lax.rsqrt(jnp.mean(x32 * x32, axis=-1, keepdims=True) + EPS)
    return (y * g.astype(jnp.float32)).astype(x.dtype)


def modulate(h, shift, scale):
    return h * (1 + scale) + shift


def split_projection(p):
    idx = np.cumsum(PROJ_SIZES)[:-1].tolist()
    return jnp.split(p, idx, axis=-1)


def depthwise_conv(u, w, b, left):
    k_w = w.shape[0]
    n = u.shape[1]
    up = jnp.pad(u, ((0, 0), (left, k_w - 1 - left), (0, 0)))
    return sum(w[k] * up[:, k:k + n] for k in range(k_w)) + b


def fourier_mix(u):
    bsz, n, _ = u.shape
    ug = u.astype(jnp.float32).reshape(bsz, n, FNET_GROUPS, FNET_CH)
    y = jnp.fft.fft2(ug, axes=(1, 3), norm='ortho').real
    return y.reshape(bsz, n, GROUP_W).astype(u.dtype)


def _linear_combine(e1, e2):
    a1, b1 = e1
    a2, b2 = e2
    return a1 * a2, a2 * b1 + b2


def rglru_direction(u, h0, conv_w, conv_b, w_a, b_a, w_i, b_i, lam, reverse):
    left = 1 if reverse else LRU_CONV // 2
    xc = depthwise_conv(u, conv_w, conv_b, left).astype(jnp.float32)
    bsz, n, wdt = xc.shape
    xh = xc.reshape(bsz, n, LRU_HEADS, LRU_HEAD_DIM)
    r = jax.nn.sigmoid(jnp.einsum('bnhd,hde->bnhe', xh, w_a.astype(jnp.float32)).reshape(bsz, n, wdt) + b_a.astype(jnp.float32))
    i = jax.nn.sigmoid(jnp.einsum('bnhd,hde->bnhe', xh, w_i.astype(jnp.float32)).reshape(bsz, n, wdt) + b_i.astype(jnp.float32))
    a = jnp.exp(-LRU_C * r * jax.nn.softplus(-lam.astype(jnp.float32)))
    bterm = jnp.sqrt(jnp.maximum(1.0 - a * a, 0.0)) * (i * xc)
    a_cum, b_cum = lax.associative_scan(_linear_combine, (a, bterm), axis=1, reverse=reverse)
    h = a_cum * h0[:, None, :] + b_cum
    h_final = h[:, 0] if reverse else h[:, -1]
    return h, h_final


def axial_rope(v, row, col):
    half = MLA_ROPE // 2
    inv = ROPE_BASE ** (-jnp.arange(0, half, 2, dtype=jnp.float32) / half)

    def rot(z, pos):
        ang = pos[:, None] * inv[None, :]
        cos = jnp.cos(ang)[None, :, None, :]
        sin = jnp.sin(ang)[None, :, None, :]
        z1, z2 = jnp.split(z, 2, axis=-1)
        return jnp.concatenate([z1 * cos - z2 * sin, z1 * sin + z2 * cos], axis=-1)

    vf = v.astype(jnp.float32)
    out = jnp.concatenate([rot(vf[..., :half], row), rot(vf[..., half:], col)], axis=-1)
    return out.astype(v.dtype)


def mla_queries(cq, q_norm_g, w_uq, row, col):
    bsz, m, _ = cq.shape
    q = (rms_norm(cq, q_norm_g) @ w_uq).reshape(bsz, m, MLA_HEADS, MLA_NOPE + MLA_ROPE)
    if row is None:
        return q
    return jnp.concatenate([q[..., :MLA_NOPE], axial_rope(q[..., MLA_NOPE:], row, col)], axis=-1)


def mla_keys_values(ckv, krope, kv_norm_g, w_ukv, row, col):
    bsz, m, _ = ckv.shape
    kv = (rms_norm(ckv, kv_norm_g) @ w_ukv).reshape(bsz, m, MLA_HEADS, MLA_NOPE + MLA_V)
    k_nope, v = kv[..., :MLA_NOPE], kv[..., MLA_NOPE:]
    kr = krope[:, :, None, :]
    if row is not None:
        kr = axial_rope(kr, row, col)
    k = jnp.concatenate([k_nope, jnp.broadcast_to(kr, (bsz, m, MLA_HEADS, MLA_ROPE))], axis=-1)
    return k, v


def attend(q, k, v):
    s = jnp.einsum('bqhd,bkhd->bhqk', q, k, preferred_element_type=jnp.float32) * ATTN_SCALE
    p = jax.nn.softmax(s, axis=-1)
    return jnp.einsum('bhqk,bkhd->bqhd', p.astype(v.dtype), v)


def attend_query_blocks(q, k, v):
    bsz, n, h, dq = q.shape
    nb = n // Q_BLOCK
    qb = q.reshape(bsz, nb, Q_BLOCK, h, dq).transpose(1, 0, 2, 3, 4)
    o = lax.map(lambda qi: attend(qi, k, v), qb)
    return o.transpose(1, 0, 2, 3, 4).reshape(bsz, n, h * v.shape[-1])


def short_conv_mix(xin, gb, gc, w, b):
    return gb * depthwise_conv(gc * xin, w, b, SC_CONV // 2)


def mix_layer(p_ctx, p_lat, row, col, with_ctx_out, lru_conv_w, lru_conv_b, lru_w_a, lru_b_a, lru_w_i, lru_b_i,
              lru_lambda, mla_q_norm_g, mla_w_uq, mla_kv_norm_g, mla_w_ukv, sc_conv_w, sc_conv_b):
    fc, lxc, lgc, cqc, ckvc, krc, sxc, sbc, scc = split_projection(p_ctx)
    fl, lxl, lgl, cql, ckvl, krl, sxl, sbl, scl = split_projection(p_lat)
    bsz = p_lat.shape[0]

    a_lat = fourier_mix(fl)

    rec_ctx_dirs = []
    rec_lat_dirs = []
    for d, rev in enumerate((False, True)):
        prm = (lru_conv_w[d], lru_conv_b[d], lru_w_a[d], lru_b_a[d], lru_w_i[d], lru_b_i[d], lru_lambda[d])
        h0 = jnp.zeros((bsz, GROUP_W), jnp.float32)
        h_c, s_c = rglru_direction(lxc, h0, *prm, reverse=rev)
        h_l, _ = rglru_direction(lxl, s_c, *prm, reverse=rev)
        rec_ctx_dirs.append(h_c)
        rec_lat_dirs.append(h_l)
    b_lat = jax.nn.gelu(lgl) * (rec_lat_dirs[0] + rec_lat_dirs[1]).astype(lgl.dtype)

    k_ctx, v_ctx = mla_keys_values(ckvc, krc, mla_kv_norm_g, mla_w_ukv, None, None)
    k_lat, v_lat = mla_keys_values(ckvl, krl, mla_kv_norm_g, mla_w_ukv, row, col)
    q_lat = mla_queries(cql, mla_q_norm_g, mla_w_uq, row, col)
    k_all = jnp.concatenate([k_ctx, k_lat], axis=1)
    v_all = jnp.concatenate([v_ctx, v_lat], axis=1)
    c_lat = attend_query_blocks(q_lat, k_all, v_all)

    d_lat = short_conv_mix(sxl, sbl, scl, sc_conv_w, sc_conv_b)

    mix_lat = jnp.concatenate([a_lat, b_lat, c_lat, d_lat], axis=-1)
    if not with_ctx_out:
        return None, mix_lat

    a_ctx = fourier_mix(fc)
    b_ctx = jax.nn.gelu(lgc) * (rec_ctx_dirs[0] + rec_ctx_dirs[1]).astype(lgc.dtype)
    q_ctx = mla_queries(cqc, mla_q_norm_g, mla_w_uq, None, None)
    c_ctx_out = attend(q_ctx, k_ctx, v_ctx).reshape(bsz, q_ctx.shape[1], MLA_HEADS * MLA_V)
    d_ctx = short_conv_mix(sxc, sbc, scc, sc_conv_w, sc_conv_b)
    mix_ctx = jnp.concatenate([a_ctx, b_ctx, c_ctx_out, d_ctx], axis=-1)
    return mix_ctx, mix_lat


def moe_ffn(h, router_w, router_b, w_gu, b_gu, w_down, b_down):
    shp = h.shape
    t = h.reshape(-1, D_MODEL)
    n_tok = t.shape[0]
    n_asg = n_tok * TOP_K
    logits = jnp.dot(t, router_w, preferred_element_type=jnp.float32) + router_b.astype(jnp.float32)
    top_val, top_idx = lax.top_k(logits, TOP_K)
    gates = jax.nn.softmax(top_val, axis=-1).astype(h.dtype)
    flat_e = top_idx.reshape(-1)
    flat_tok = jnp.arange(n_asg, dtype=jnp.int32) // TOP_K
    order = jnp.argsort(flat_e)
    e_sorted = flat_e[order]
    counts = jnp.bincount(flat_e, length=N_EXPERTS)
    padded = (counts + MOE_BLOCK - 1) // MOE_BLOCK * MOE_BLOCK
    pad_end = jnp.cumsum(padded)
    pad_start = pad_end - padded
    grp_start = jnp.cumsum(counts) - counts
    slot = pad_start[e_sorted] + jnp.arange(n_asg, dtype=jnp.int32) - grp_start[e_sorted]
    n_blocks = (n_asg + N_EXPERTS * (MOE_BLOCK - 1) + MOE_BLOCK - 1) // MOE_BLOCK
    n_slots = n_blocks * MOE_BLOCK
    slot_tok = jnp.full((n_slots,), n_tok, jnp.int32).at[slot].set(flat_tok[order])
    slot_gate = jnp.zeros((n_slots,), h.dtype).at[slot].set(gates.reshape(-1)[order])
    block_exp = jnp.minimum(jnp.searchsorted(pad_end, jnp.arange(n_blocks, dtype=jnp.int32) * MOE_BLOCK, side='right'), N_EXPERTS - 1)
    t_pad = jnp.concatenate([t, jnp.zeros((1, D_MODEL), t.dtype)], axis=0)

    def expert_block(args):
        tok, e = args
        xb = t_pad[tok]
        gu = xb @ w_gu[e] + b_gu[e]
        g = jnp.minimum(gu[:, :D_EXPERT], SWIGLU_LIMIT)
        u = jnp.clip(gu[:, D_EXPERT:], -SWIGLU_LIMIT, SWIGLU_LIMIT)
        act = (u + 1) * (g * jax.nn.sigmoid(SWIGLU_ALPHA * g))
        return act @ w_down[e] + b_down[e]

    y = lax.map(expert_block, (slot_tok.reshape(n_blocks, MOE_BLOCK), block_exp))
    y = y.reshape(n_slots, D_MODEL) * slot_gate[:, None]
    out = jnp.zeros((n_tok + 1, D_MODEL), h.dtype).at[slot_tok].add(y)[:n_tok]
    return out.reshape(shp)


def setup_inputs(seed: int = 0) -> dict:
    key = jax.random.key(seed)
    ks = iter(jax.random.split(key, 40))
    f32 = jnp.float32
    L = DEPTH

    def nrm(shape, scale):
        return jax.random.normal(next(ks), shape, f32) * scale

    def gain(shape):
        return 1.0 + nrm(shape, 0.05)

    x = nrm((BATCH, SEQ, D_MODEL), 1.0)
    c = nrm((BATCH, D_MODEL), 1.0)
    ctx = nrm((BATCH, CTX_LEN, D_MODEL), 1.0)
    c_ctx = nrm((D_MODEL,), 1.0)
    mod_w = nrm((L, D_MODEL, 6 * D_MODEL), D_MODEL ** -0.5)
    mod_b = nrm((L, 6 * D_MODEL), 0.02)
    norm1_g = gain((L, D_MODEL))
    norm2_g = gain((L, D_MODEL))
    w_in = nrm((L, D_MODEL, P_TOTAL), D_MODEL ** -0.5)
    lru_conv_w = nrm((L, 2, LRU_CONV, GROUP_W), LRU_CONV ** -0.5)
    lru_conv_b = nrm((L, 2, GROUP_W), 0.02)
    lru_w_a = nrm((L, 2, LRU_HEADS, LRU_HEAD_DIM, LRU_HEAD_DIM), LRU_HEAD_DIM ** -0.5)
    lru_b_a = nrm((L, 2, GROUP_W), 0.1)
    lru_w_i = nrm((L, 2, LRU_HEADS, LRU_HEAD_DIM, LRU_HEAD_DIM), LRU_HEAD_DIM ** -0.5)
    lru_b_i = nrm((L, 2, GROUP_W), 0.1)
    a_c = jax.random.uniform(next(ks), (L, 2, GROUP_W), f32, 0.9, 0.999)
    a0 = a_c ** (1.0 / LRU_C)
    lru_lambda = jnp.log(a0) - jnp.log1p(-a0)
    mla_q_norm_g = gain((L, MLA_Q_LORA))
    mla_w_uq = nrm((L, MLA_Q_LORA, MLA_HEADS * (MLA_NOPE + MLA_ROPE)), MLA_Q_LORA ** -0.5)
    mla_kv_norm_g = gain((L, MLA_KV_LORA))
    mla_w_ukv = nrm((L, MLA_KV_LORA, MLA_HEADS * (MLA_NOPE + MLA_V)), MLA_KV_LORA ** -0.5)
    sc_conv_w = nrm((L, SC_CONV, GROUP_W), SC_CONV ** -0.5)
    sc_conv_b = nrm((L, GROUP_W), 0.02)
    w_out = nrm((L, D_MODEL, D_MODEL), D_MODEL ** -0.5)
    router_w = nrm((L, D_MODEL, N_EXPERTS), D_MODEL ** -0.5)
    router_b = nrm((L, N_EXPERTS), 0.01)
    exp_w_gu = nrm((L, N_EXPERTS, D_MODEL, 2 * D_EXPERT), D_MODEL ** -0.5)
    exp_b_gu = nrm((L, N_EXPERTS, 2 * D_EXPERT), 0.01)
    exp_w_down = nrm((L, N_EXPERTS, D_EXPERT, D_MODEL), D_EXPERT ** -0.5)
    exp_b_down = nrm((L, N_EXPERTS, D_MODEL), 0.01)
    final_g = gain((D_MODEL,))
    return {'x': x, 'c': c, 'ctx': ctx, 'c_ctx': c_ctx, 'mod_w': mod_w, 'mod_b': mod_b,
            'norm1_g': norm1_g, 'norm2_g': norm2_g, 'w_in': w_in,
            'lru_conv_w': lru_conv_w, 'lru_conv_b': lru_conv_b, 'lru_w_a': lru_w_a, 'lru_b_a': lru_b_a,
            'lru_w_i': lru_w_i, 'lru_b_i': lru_b_i, 'lru_lambda': lru_lambda,
            'mla_q_norm_g': mla_q_norm_g, 'mla_w_uq': mla_w_uq, 'mla_kv_norm_g': mla_kv_norm_g,
            'mla_w_ukv': mla_w_ukv, 'sc_conv_w': sc_conv_w, 'sc_conv_b': sc_conv_b, 'w_out': w_out,
            'router_w': router_w, 'router_b': router_b, 'exp_w_gu': exp_w_gu, 'exp_b_gu': exp_b_gu,
            'exp_w_down': exp_w_down, 'exp_b_down': exp_b_down, 'final_g': final_g}


def reference(x, c, ctx, c_ctx, mod_w, mod_b, norm1_g, norm2_g, w_in, lru_conv_w, lru_conv_b, lru_w_a, lru_b_a,
              lru_w_i, lru_b_i, lru_lambda, mla_q_norm_g, mla_w_uq, mla_kv_norm_g, mla_w_ukv, sc_conv_w, sc_conv_b,
              w_out, router_w, router_b, exp_w_gu, exp_b_gu, exp_w_down, exp_b_down, final_g):
    n = x.shape[1]
    grid_rows = n // GRID_W
    row = jnp.broadcast_to(jnp.arange(grid_rows, dtype=jnp.float32)[:, None], (grid_rows, GRID_W)).reshape(-1)
    col = jnp.broadcast_to(jnp.arange(GRID_W, dtype=jnp.float32)[None, :], (grid_rows, GRID_W)).reshape(-1)
    n_ctx = ctx.shape[1]

    h_lat = x
    h_ctx = ctx
    for l in range(DEPTH):
        with_ctx_out = l < DEPTH - 1
        mod_lat = (jax.nn.silu(c) @ mod_w[l] + mod_b[l])[:, None, :]
        mod_ctx = (jax.nn.silu(c_ctx) @ mod_w[l] + mod_b[l])[None, None, :]
        sh1_l, sc1_l, g1_l, sh2_l, sc2_l, g2_l = jnp.split(mod_lat, 6, axis=-1)
        sh1_c, sc1_c, g1_c, sh2_c, sc2_c, g2_c = jnp.split(mod_ctx, 6, axis=-1)

        a_lat = modulate(rms_norm(h_lat, norm1_g[l]), sh1_l, sc1_l)
        a_ctx = modulate(rms_norm(h_ctx, norm1_g[l]), sh1_c, sc1_c)
        p_lat = a_lat @ w_in[l]
        p_ctx = a_ctx @ w_in[l]
        mix_ctx, mix_lat = mix_layer(p_ctx, p_lat, row, col, with_ctx_out,
                                     lru_conv_w[l], lru_conv_b[l], lru_w_a[l], lru_b_a[l], lru_w_i[l], lru_b_i[l],
                                     lru_lambda[l], mla_q_norm_g[l], mla_w_uq[l], mla_kv_norm_g[l], mla_w_ukv[l],
                                     sc_conv_w[l], sc_conv_b[l])
        h_lat = h_lat + g1_l * (mix_lat @ w_out[l])

        f_lat = modulate(rms_norm(h_lat, norm2_g[l]), sh2_l, sc2_l)
        moe_args = (router_w[l], router_b[l], exp_w_gu[l], exp_b_gu[l], exp_w_down[l], exp_b_down[l])
        if with_ctx_out:
            h_ctx = h_ctx + g1_c * (mix_ctx @ w_out[l])
            f_ctx = modulate(rms_norm(h_ctx, norm2_g[l]), sh2_c, sc2_c)
            y_all = moe_ffn(jnp.concatenate([f_ctx, f_lat], axis=1), *moe_args)
            h_ctx = h_ctx + g2_c * y_all[:, :n_ctx]
            h_lat = h_lat + g2_l * y_all[:, n_ctx:]
        else:
            h_lat = h_lat + g2_l * moe_ffn(f_lat, *moe_args)

    return rms_norm(h_lat, final_g)
```

```python
import functools
import math

import numpy as np
import jax
import jax.numpy as jnp
from jax import lax
from jax.experimental import pallas as pl
from jax.experimental.pallas import tpu as pltpu

F32 = jnp.float32
BF16 = jnp.bfloat16

D_MODEL = 1024
GRID_W = 64
GROUP_W = 256
EPS = 1e-6
FNET_CH = 64
LRU_HEADS = 4
LRU_C = 8.0
MLA_HEADS = 4
MLA_NOPE = 64
MLA_ROPE = 32
MLA_V = 64
MLA_Q_LORA = 256
MLA_KV_LORA = 128
ROPE_BASE = 10000.0
ATTN_SCALE = (MLA_NOPE + MLA_ROPE) ** -0.5
N_EXPERTS = 32
TOP_K = 4
SWIGLU_ALPHA = 1.702
SWIGLU_LIMIT = 7.0

HEAD_PAD = 128
P_PAD = 2048
MOE_BM = 256
VMEM_LIMIT = 56 * 1024 * 1024


def _cparams(n_axes, vmem=None):
    return pltpu.CompilerParams(dimension_semantics=("arbitrary",) * n_axes,
                                vmem_limit_bytes=vmem or VMEM_LIMIT)


def _bdot(a, b):
    return jnp.dot(a.astype(BF16), b.astype(BF16), preferred_element_type=F32)


def _mod_kernel(c_ref, w_ref, b_ref, o_ref):
    c = c_ref[...]
    s = c * jax.nn.sigmoid(c)
    o_ref[0] = _bdot(s, w_ref[0]) + b_ref[0]


def _modulation(c_all, mod_w, mod_b):
    L, D, N = mod_w.shape
    R = c_all.shape[0]
    tn = 1024
    return pl.pallas_call(
        _mod_kernel,
        out_shape=jax.ShapeDtypeStruct((L, R, N), F32),
        grid=(L, N // tn),
        in_specs=[pl.BlockSpec((R, D), lambda l, j: (0, 0)),
                  pl.BlockSpec((1, D, tn), lambda l, j: (l, 0, j)),
                  pl.BlockSpec((1, 1, tn), lambda l, j: (l, 0, j))],
        out_specs=pl.BlockSpec((1, R, tn), lambda l, j: (l, 0, j)),
        compiler_params=_cparams(2),
        name="adaln_mod",
    )(c_all, mod_w, mod_b.reshape(L, 1, N))


def _inproj_kernel(h_ref, g_ref, sh_ref, sc_ref, w_ref, cs_ref, zf_ref, lru_ref, mla_ref, u_ref, sb_ref):
    x = h_ref[0]
    y = x * lax.rsqrt(jnp.mean(x * x, axis=-1, keepdims=True) + EPS) * g_ref[...]
    a = y * (1.0 + sc_ref[0]) + sh_ref[0]
    p = _bdot(a, w_ref[...])
    zf_ref[0] = _bdot(p[:, 0:256], cs_ref[...]).astype(BF16)
    lru_ref[0] = p[:, 256:768]
    mla_ref[0] = p[:, 768:1280]
    u_ref[0] = p[:, 1280:1536] * p[:, 1792:2048]
    sb_ref[0] = p[:, 1536:1792]


def _in_projection(h, g, sh, sc, w_in_p, cs):
    B, n, D = h.shape
    tm = min(256, n)
    tile = lambda w: pl.BlockSpec((1, tm, w), lambda b, i: (b, i, 0))
    vec = pl.BlockSpec((1, 1, D), lambda b, i: (b, 0, 0))
    return pl.pallas_call(
        _inproj_kernel,
        out_shape=(jax.ShapeDtypeStruct((B, n, 512), BF16),
                   jax.ShapeDtypeStruct((B, n, 512), F32),
                   jax.ShapeDtypeStruct((B, n, 512), F32),
                   jax.ShapeDtypeStruct((B, n, 256), F32),
                   jax.ShapeDtypeStruct((B, n, 256), F32)),
        grid=(B, n // tm),
        in_specs=[tile(D),
                  pl.BlockSpec((1, D), lambda b, i: (0, 0)),
                  vec, vec,
                  pl.BlockSpec((D, P_PAD), lambda b, i: (0, 0)),
                  pl.BlockSpec((256, 512), lambda b, i: (0, 0))],
        out_specs=(tile(512), tile(512), tile(512), tile(256), tile(256)),
        compiler_params=_cparams(2),
        name="norm_inproj",
    )(h, g.reshape(1, D), sh, sc, w_in_p, cs)


def _fnet_kernel(c_ref, s_ref, z_ref, o_ref):
    z = z_ref[0]
    acc = jnp.dot(c_ref[...], z[:, 0:256], preferred_element_type=F32)
    acc = acc - jnp.dot(s_ref[...], z[:, 256:512], preferred_element_type=F32)
    o_ref[0] = acc.astype(BF16)


def _fourier_positions(zf, cn, sn):
    B, n, _ = zf.shape
    tm = min(512, n)
    return pl.pallas_call(
        _fnet_kernel,
        out_shape=jax.ShapeDtypeStruct((B, n, 256), BF16),
        grid=(n // tm, B),
        in_specs=[pl.BlockSpec((tm, n), lambda i, b: (i, 0)),
                  pl.BlockSpec((tm, n), lambda i, b: (i, 0)),
                  pl.BlockSpec((1, n, 512), lambda i, b: (b, 0, 0))],
        out_specs=pl.BlockSpec((1, tm, 256), lambda i, b: (b, i, 0)),
        compiler_params=_cparams(2),
        name="fourier_positions",
    )(cn, sn, zf)


def _shift_rows(xcat, off, tc):
    if off == 0:
        return xcat[8:8 + tc]
    rolled = pltpu.roll(xcat, (tc + 16) - (8 + off), axis=0)
    return rolled[0:tc]


def _load_halo(ref, col0, width, start, tc, n):
    main = ref[0, pl.ds(start, tc), col0:col0 + width]
    ps = jnp.maximum(start - 8, 0)
    prev = ref[0, pl.ds(pl.multiple_of(ps, 8), 8), col0:col0 + width]
    prev = jnp.where(start > 0, prev, 0.0)
    ns = jnp.minimum(start + tc, n - 8)
    nxt = ref[0, pl.ds(pl.multiple_of(ns, 8), 8), col0:col0 + width]
    nxt = jnp.where(start + tc < n, nxt, 0.0)
    return jnp.concatenate([prev, main, nxt], axis=0)


def _lru_kernel(x_ref, h0_ref, cw_ref, cb_ref, wai_ref, bai_ref, lam_ref, o_ref, hf_ref, hs_ref, *, tc, n):
    nc = n // tc
    row = lax.broadcasted_iota(jnp.int32, (tc, GROUP_W), 0)

    def direction(d, reverse):
        left = 1 if reverse else 2
        cw = cw_ref[d]
        cb = cb_ref[d]
        wai = wai_ref[d]
        bai = bai_ref[d]
        sp = jax.nn.softplus(-lam_ref[d])

        def chunk(ci, carry):
            c = (nc - 1 - ci) if reverse else ci
            start = pl.multiple_of(c * tc, tc)
            xcat = _load_halo(x_ref, 0, GROUP_W, start, tc, n)
            xc = cb
            for k in range(4):
                xc = xc + cw[k:k + 1, :] * _shift_rows(xcat, k - left, tc)
            gates = jax.nn.sigmoid(_bdot(xc, wai) + bai)
            r = gates[:, 0:GROUP_W]
            gi = gates[:, GROUP_W:2 * GROUP_W]
            a = jnp.exp(-LRU_C * r * sp)
            b = jnp.sqrt(jnp.maximum(1.0 - a * a, 0.0)) * (gi * xc)
            s = 1
            while s < tc:
                if reverse:
                    a_sh = pltpu.roll(a, tc - s, axis=0)
                    b_sh = pltpu.roll(b, tc - s, axis=0)
                    keep = row < (tc - s)
                else:
                    a_sh = pltpu.roll(a, s, axis=0)
                    b_sh = pltpu.roll(b, s, axis=0)
                    keep = row >= s
                b = jnp.where(keep, a * b_sh, 0.0) + b
                a = jnp.where(keep, a * a_sh, a)
                s *= 2
            h = a * carry + b
            if reverse:
                gate = x_ref[0, pl.ds(start, tc), GROUP_W:2 * GROUP_W]
                tot = hs_ref[pl.ds(start, tc), :] + h
                o_ref[0, pl.ds(start, tc), :] = (jax.nn.gelu(gate) * tot).astype(o_ref.dtype)
                return h[0:1, :]
            hs_ref[pl.ds(start, tc), :] = h
            return h[tc - 1:tc, :]

        return lax.fori_loop(0, nc, chunk, h0_ref[0, d])

    hf_ref[0, 0] = direction(0, False)
    hf_ref[0, 1] = direction(1, True)


def _rglru(lru, h0, cw, cb, wai, bai, lam):
    B, n, _ = lru.shape
    tc = min(256, n)
    full = lambda *s: pl.BlockSpec(s, lambda b: (0,) * len(s))
    kern = functools.partial(_lru_kernel, tc=tc, n=n)
    return pl.pallas_call(
        kern,
        out_shape=(jax.ShapeDtypeStruct((B, n, GROUP_W), BF16),
                   jax.ShapeDtypeStruct((B, 2, 1, GROUP_W), F32)),
        grid=(B,),
        in_specs=[pl.BlockSpec((1, n, 512), lambda b: (b, 0, 0)),
                  pl.BlockSpec((1, 2, 1, GROUP_W), lambda b: (b, 0, 0, 0)),
                  full(2, 4, GROUP_W), full(2, 1, GROUP_W), full(2, GROUP_W, 2 * GROUP_W),
                  full(2, 1, 2 * GROUP_W), full(2, 1, GROUP_W)],
        out_specs=(pl.BlockSpec((1, n, GROUP_W), lambda b: (b, 0, 0)),
                   pl.BlockSpec((1, 2, 1, GROUP_W), lambda b: (b, 0, 0, 0))),
        scratch_shapes=[pltpu.VMEM((n, GROUP_W), F32)],
        compiler_params=_cparams(1),
        name="rglru",
    )(lru, h0, cw, cb, wai, bai, lam)


def _sconv_kernel(u_ref, sb_ref, w_ref, b_ref, o_ref, *, tc, n):
    w = w_ref[...]
    bias = b_ref[...]

    def chunk(c, _):
        start = pl.multiple_of(c * tc, tc)
        ucat = _load_halo(u_ref, 0, GROUP_W, start, tc, n)
        y = bias
        for k in range(3):
            y = y + w[k:k + 1, :] * _shift_rows(ucat, k - 1, tc)
        o_ref[0, pl.ds(start, tc), :] = (sb_ref[0, pl.ds(start, tc), :] * y).astype(o_ref.dtype)
        return 0

    lax.fori_loop(0, n // tc, chunk, 0)


def _short_conv(u, sb, w, b):
    B, n, W = u.shape
    tc = min(256, n)
    seq = pl.BlockSpec((1, n, W), lambda i: (i, 0, 0))
    return pl.pallas_call(
        functools.partial(_sconv_kernel, tc=tc, n=n),
        out_shape=jax.ShapeDtypeStruct((B, n, W), BF16),
        grid=(B,),
        in_specs=[seq, seq, pl.BlockSpec((3, W), lambda i: (0, 0)), pl.BlockSpec((1, W), lambda i: (0, 0))],
        out_specs=seq,
        compiler_params=_cparams(1),
        name="short_conv",
    )(u, sb, w, b.reshape(1, W))


def _mla_prep_kernel(x_ref, qg_ref, kg_ref, wq_ref, wk_ref, wv_ref, e2_ref, cos_ref, sin_ref,
                     q_ref, k_ref, v_ref):
    x = x_ref[0]
    cq = x[:, 0:256]
    ckv = x[:, 256:384]
    kr = x[:, 384:448]
    cqn = cq * lax.rsqrt(jnp.mean(cq * cq, axis=-1, keepdims=True) + EPS) * qg_ref[...]
    ckvn = ckv * lax.rsqrt(jnp.mean(ckv * ckv, axis=-1, keepdims=True) + EPS) * kg_ref[...]
    cos = cos_ref[...]
    sin = sin_ref[...]
    q2 = _bdot(cqn, wq_ref[...])
    q = (q2[:, 0:512] * cos + q2[:, 512:1024] * sin) * ATTN_SCALE
    kr2 = _bdot(kr, e2_ref[...])
    k = _bdot(ckvn, wk_ref[...]) + kr2[:, 0:512] * cos + kr2[:, 512:1024] * sin
    v = _bdot(ckvn, wv_ref[...])
    for h in range(MLA_HEADS):
        q_ref[0, h] = q[:, h * HEAD_PAD:(h + 1) * HEAD_PAD].astype(BF16)
        k_ref[0, h] = k[:, h * HEAD_PAD:(h + 1) * HEAD_PAD].astype(BF16)
    v_ref[0] = v.astype(BF16)


def _mla_prep(mla, qg, kg, wq2, wk, wv, e2, cos, sin):
    B, n, _ = mla.shape
    tm = min(256, n)
    HW = MLA_HEADS * HEAD_PAD
    full = lambda *s: pl.BlockSpec(s, lambda b, i: (0,) * len(s))
    return pl.pallas_call(
        _mla_prep_kernel,
        out_shape=(jax.ShapeDtypeStruct((B, MLA_HEADS, n, HEAD_PAD), BF16),
                   jax.ShapeDtypeStruct((B, MLA_HEADS, n, HEAD_PAD), BF16),
                   jax.ShapeDtypeStruct((B, n, MLA_HEADS * MLA_V), BF16)),
        grid=(B, n // tm),
        in_specs=[pl.BlockSpec((1, tm, 512), lambda b, i: (b, i, 0)),
                  full(1, MLA_Q_LORA), full(1, MLA_KV_LORA),
                  full(MLA_Q_LORA, 2 * HW), full(MLA_KV_LORA, HW), full(MLA_KV_LORA, MLA_HEADS * MLA_V),
                  full(2 * MLA_ROPE, 2 * HW),
                  pl.BlockSpec((tm, HW), lambda b, i: (i, 0)),
                  pl.BlockSpec((tm, HW), lambda b, i: (i, 0))],
        out_specs=(pl.BlockSpec((1, MLA_HEADS, tm, HEAD_PAD), lambda b, i: (b, 0, i, 0)),
                   pl.BlockSpec((1, MLA_HEADS, tm, HEAD_PAD), lambda b, i: (b, 0, i, 0)),
                   pl.BlockSpec((1, tm, MLA_HEADS * MLA_V), lambda b, i: (b, i, 0))),
        compiler_params=_cparams(2),
        name="mla_prep",
    )(mla, qg.reshape(1, -1), kg.reshape(1, -1), wq2, wk, wv, e2, cos, sin)


def _attn_kernel(q_ref, k_ref, v_ref, o_ref, *, tk, n_kt):
    tq = q_ref.shape[2]
    outs = []
    for h in range(MLA_HEADS):
        q = q_ref[0, h]

        def body(j, carry, h=h, q=q):
            m, l, acc = carry
            ks = pl.multiple_of(j * tk, tk)
            kt = k_ref[0, h, pl.ds(ks, tk), :]
            vt = v_ref[0, pl.ds(ks, tk), h * MLA_V:(h + 1) * MLA_V]
            s = lax.dot_general(q, kt, (((1,), (1,)), ((), ())), preferred_element_type=F32)
            m_new = jnp.maximum(m, jnp.max(s, axis=-1, keepdims=True))
            alpha = jnp.exp(m - m_new)
            p = jnp.exp(s - m_new)
            l_new = alpha * l + jnp.sum(p, axis=-1, keepdims=True)
            acc_new = alpha * acc + jnp.dot(p.astype(BF16), vt, preferred_element_type=F32)
            return m_new, l_new, acc_new

        init = (jnp.full((tq, 1), -jnp.inf, F32), jnp.zeros((tq, 1), F32), jnp.zeros((tq, MLA_V), F32))
        m, l, acc = lax.fori_loop(0, n_kt, body, init)
        outs.append(acc / l)
    o_ref[0] = jnp.concatenate(outs, axis=-1).astype(o_ref.dtype)


def _attention(q, k, v):
    B, H, n, _ = q.shape
    M = k.shape[2]
    tq = min(256, n)
    tk = min(256, M)
    return pl.pallas_call(
        functools.partial(_attn_kernel, tk=tk, n_kt=M // tk),
        out_shape=jax.ShapeDtypeStruct((B, n, H * MLA_V), BF16),
        grid=(B, n // tq),
        in_specs=[pl.BlockSpec((1, H, tq, HEAD_PAD), lambda b, i: (b, 0, i, 0)),
                  pl.BlockSpec((1, H, M, HEAD_PAD), lambda b, i: (b, 0, 0, 0)),
                  pl.BlockSpec((1, M, H * MLA_V), lambda b, i: (b, 0, 0))],
        out_specs=pl.BlockSpec((1, tq, H * MLA_V), lambda b, i: (b, i, 0)),
        compiler_params=_cparams(2),
        name="mla_attention",
    )(q, k, v)


def _outproj_kernel(a_ref, b_ref, c_ref, d_ref, h_ref, w_ref, g1_ref, ng_ref, sh_ref, sc_ref, rw_ref, rb_ref,
                    hn_ref, f_ref, ti_ref, tg_ref):
    mix = jnp.dot(a_ref[0], w_ref[0:256, :], preferred_element_type=F32)
    mix = mix + jnp.dot(b_ref[0], w_ref[256:512, :], preferred_element_type=F32)
    mix = mix + jnp.dot(c_ref[0], w_ref[512:768, :], preferred_element_type=F32)
    mix = mix + jnp.dot(d_ref[0], w_ref[768:1024, :], preferred_element_type=F32)
    hn = h_ref[0] + g1_ref[0] * mix
    hn_ref[0] = hn
    y = hn * lax.rsqrt(jnp.mean(hn * hn, axis=-1, keepdims=True) + EPS) * ng_ref[...]
    f = y * (1.0 + sc_ref[0]) + sh_ref[0]
    f_ref[0] = f
    logits = jnp.dot(f, rw_ref[...], preferred_element_type=F32, precision=lax.Precision.HIGHEST) + rb_ref[...]
    tm = logits.shape[0]
    lane = lax.broadcasted_iota(jnp.int32, (tm, N_EXPERTS), 1)
    lane_o = lax.broadcasted_iota(jnp.int32, (tm, 128), 1)
    top_v = jnp.full((tm, 128), -jnp.inf, F32)
    top_i = jnp.zeros((tm, 128), jnp.int32)
    cur = logits
    m0 = None
    for kk in range(TOP_K):
        m = jnp.max(cur, axis=-1, keepdims=True)
        sel = jnp.min(jnp.where(cur == m, lane, N_EXPERTS), axis=-1, keepdims=True)
        if kk == 0:
            m0 = m
        top_v = jnp.where(lane_o == kk, m, top_v)
        top_i = jnp.where(lane_o == kk, sel, top_i)
        cur = jnp.where(lane == sel, -jnp.inf, cur)
    e = jnp.where(lane_o < TOP_K, jnp.exp(top_v - m0), 0.0)
    ti_ref[0] = top_i
    tg_ref[0] = e / jnp.sum(e, axis=-1, keepdims=True)


def _out_projection(a, b, c, d, h, w_out, g1, ng, sh, sc, rw, rb):
    B, n, D = h.shape
    tm = min(256, n)
    tile = lambda w: pl.BlockSpec((1, tm, w), lambda bb, i: (bb, i, 0))
    vec = pl.BlockSpec((1, 1, D), lambda bb, i: (bb, 0, 0))
    full = lambda *s: pl.BlockSpec(s, lambda bb, i: (0,) * len(s))
    return pl.pallas_call(
        _outproj_kernel,
        out_shape=(jax.ShapeDtypeStruct((B, n, D), F32),
                   jax.ShapeDtypeStruct((B, n, D), F32),
                   jax.ShapeDtypeStruct((B, n, 128), jnp.int32),
                   jax.ShapeDtypeStruct((B, n, 128), F32)),
        grid=(B, n // tm),
        in_specs=[tile(256), tile(256), tile(256), tile(256), tile(D),
                  full(D, D), vec, full(1, D), vec, vec, full(D, N_EXPERTS), full(1, N_EXPERTS)],
        out_specs=(tile(D), tile(D), tile(128), tile(128)),
        compiler_params=_cparams(2),
        name="outproj_router",
    )(a, b, c, d, h, w_out, g1, ng.reshape(1, D), sh, sc, rw, rb.reshape(1, N_EXPERTS))


def _gather_kernel(idx_ref, src_ref, o_ref, buf_ref, sem_ref, *, bm):
    def issue(r, _):
        tok = idx_ref[0, 0, r]
        pltpu.make_async_copy(src_ref.at[pl.ds(tok, 1)], buf_ref.at[pl.ds(r, 1)], sem_ref).start()
        return 0

    lax.fori_loop(0, bm, issue, 0)
    pltpu.make_async_copy(src_ref.at[pl.ds(0, bm)], buf_ref, sem_ref).wait()
    o_ref[...] = buf_ref[...].astype(o_ref.dtype)


def _gather_rows(src, idx, bm):
    n_rows = idx.shape[0]
    D = src.shape[1]
    nb = n_rows // bm
    return pl.pallas_call(
        functools.partial(_gather_kernel, bm=bm),
        out_shape=jax.ShapeDtypeStruct((n_rows, D), BF16),
        grid=(nb,),
        in_specs=[pl.BlockSpec((1, 1, bm), lambda i: (i, 0, 0), memory_space=pltpu.SMEM),
                  pl.BlockSpec(memory_space=pl.ANY)],
        out_specs=pl.BlockSpec((bm, D), lambda i: (i, 0)),
        scratch_shapes=[pltpu.VMEM((bm, D), src.dtype), pltpu.SemaphoreType.DMA],
        compiler_params=_cparams(1),
        name="moe_gather",
    )(idx.reshape(nb, 1, bm), src)


def _ffn_kernel(be_ref, nu_ref, x_ref, wgu_ref, bgu_ref, wd_ref, bd_ref, y_ref):
    @pl.when(pl.program_id(0) < nu_ref[0])
    def _():
        De = wd_ref.shape[1]
        gu = jnp.dot(x_ref[...], wgu_ref[0], preferred_element_type=F32) + bgu_ref[0]
        g = jnp.minimum(gu[:, 0:De], SWIGLU_LIMIT)
        u = jnp.clip(gu[:, De:2 * De], -SWIGLU_LIMIT, SWIGLU_LIMIT)
        act = (u + 1.0) * (g * jax.nn.sigmoid(SWIGLU_ALPHA * g))
        y_ref[...] = jnp.dot(act.astype(BF16), wd_ref[0], preferred_element_type=F32) + bd_ref[0]

    @pl.when(pl.program_id(0) >= nu_ref[0])
    def _():
        y_ref[...] = jnp.zeros_like(y_ref)


def _expert_ffn(xs, block_exp, n_used, wgu, bgu, wd, bd, bm):
    n_slots, D = xs.shape
    E, _, G = wgu.shape
    nb = n_slots // bm
    grid_spec = pltpu.PrefetchScalarGridSpec(
        num_scalar_prefetch=2,
        grid=(nb,),
        in_specs=[pl.BlockSpec((bm, D), lambda i, be, nu: (i, 0)),
                  pl.BlockSpec((1, D, G), lambda i, be, nu: (be[i], 0, 0)),
                  pl.BlockSpec((1, 1, G), lambda i, be, nu: (be[i], 0, 0)),
                  pl.BlockSpec((1, G // 2, D), lambda i, be, nu: (be[i], 0, 0)),
                  pl.BlockSpec((1, 1, D), lambda i, be, nu: (be[i], 0, 0))],
        out_specs=pl.BlockSpec((bm, D), lambda i, be, nu: (i, 0)),
    )
    return pl.pallas_call(
        _ffn_kernel,
        out_shape=jax.ShapeDtypeStruct((n_slots, D), F32),
        grid_spec=grid_spec,
        compiler_params=_cparams(1),
        name="moe_expert_ffn",
    )(block_exp, n_used, xs, wgu, bgu.reshape(E, 1, G), wd, bd.reshape(E, 1, D))


def _combine_kernel(pos_ref, y_ref, h_ref, g2_ref, gate_ref, o_ref, buf_ref, sem_ref, *, tm):
    def issue(r, _):
        for kk in range(TOP_K):
            slot = pos_ref[0, 0, r * TOP_K + kk]
            pltpu.make_async_copy(y_ref.at[pl.ds(slot, 1)], buf_ref.at[kk, pl.ds(r, 1)], sem_ref).start()
        return 0

    lax.fori_loop(0, tm, issue, 0)
    pltpu.make_async_copy(buf_ref, buf_ref, sem_ref).wait()
    gate = gate_ref[...]
    acc = gate[:, 0:1] * buf_ref[0]
    for kk in range(1, TOP_K):
        acc = acc + gate[:, kk:kk + 1] * buf_ref[kk]
    o_ref[...] = h_ref[...] + g2_ref[0] * acc


def _combine(y, pos, h, g2_tiles, gates, tm):
    T, D = h.shape
    nt = T // tm
    return pl.pallas_call(
        functools.partial(_combine_kernel, tm=tm),
        out_shape=jax.ShapeDtypeStruct((T, D), F32),
        grid=(nt,),
        in_specs=[pl.BlockSpec((1, 1, tm * TOP_K), lambda i: (i, 0, 0), memory_space=pltpu.SMEM),
                  pl.BlockSpec(memory_space=pl.ANY),
                  pl.BlockSpec((tm, D), lambda i: (i, 0)),
                  pl.BlockSpec((1, 1, D), lambda i: (i, 0, 0)),
                  pl.BlockSpec((tm, 128), lambda i: (i, 0))],
        out_specs=pl.BlockSpec((tm, D), lambda i: (i, 0)),
        scratch_shapes=[pltpu.VMEM((TOP_K, tm, D), F32), pltpu.SemaphoreType.DMA],
        compiler_params=_cparams(1),
        name="moe_combine",
    )(pos.reshape(nt, 1, tm * TOP_K), y, h, g2_tiles, gates)


def _moe(f, h, top_i, top_g, g2_tiles, wgu, bgu, wd, bd, tm):
    T, D = f.shape
    bm = MOE_BM
    n_asg = T * TOP_K
    flat_e = top_i[:, 0:TOP_K].reshape(-1)
    onehot = (flat_e[:, None] == jnp.arange(N_EXPERTS, dtype=jnp.int32)[None, :]).astype(jnp.int32)
    csum = jnp.cumsum(onehot, axis=0)
    counts = csum[-1]
    rank = jnp.sum((csum - onehot) * onehot, axis=1)
    padded = (counts + bm - 1) // bm * bm
    pad_end = jnp.cumsum(padded)
    pad_start = pad_end - padded
    pos = (pad_start[flat_e] + rank).astype(jnp.int32)
    nb = (n_asg + N_EXPERTS * (bm - 1) + bm - 1) // bm
    n_slots = nb * bm
    slot_tok = jnp.zeros((n_slots,), jnp.int32).at[pos].set(jnp.arange(n_asg, dtype=jnp.int32) // TOP_K)
    block_exp = jnp.minimum(
        jnp.searchsorted(pad_end, jnp.arange(nb, dtype=jnp.int32) * bm, side='right'), N_EXPERTS - 1).astype(jnp.int32)
    n_used = (pad_end[-1:] // bm).astype(jnp.int32)

    xs = _gather_rows(f, slot_tok, bm)
    y = _expert_ffn(xs, block_exp, n_used, wgu, bgu, wd, bd, bm)
    return _combine(y, pos, h, g2_tiles, top_g, tm)


def _final_norm_kernel(h_ref, g_ref, o_ref):
    x = h_ref[...]
    o_ref[...] = x * lax.rsqrt(jnp.mean(x * x, axis=-1, keepdims=True) + EPS) * g_ref[...]


def _final_norm(h, g):
    T, D = h.shape
    tm = min(512, T)
    return pl.pallas_call(
        _final_norm_kernel,
        out_shape=jax.ShapeDtypeStruct((T, D), F32),
        grid=(T // tm,),
        in_specs=[pl.BlockSpec((tm, D), lambda i: (i, 0)), pl.BlockSpec((1, D), lambda i: (0, 0))],
        out_specs=pl.BlockSpec((tm, D), lambda i: (i, 0)),
        compiler_params=_cparams(1),
        name="final_norm",
    )(h, g.reshape(1, D))


_ROPE_SWAP = np.concatenate([np.arange(8, 16), np.arange(0, 8), np.arange(24, 32), np.arange(16, 24)])


def _dft_tables(n):
    hi = n // 64
    j = jnp.arange(n, dtype=jnp.int32)[:, None]
    k = jnp.arange(64, dtype=jnp.int32)[None, :]
    k2 = jnp.arange(hi, dtype=jnp.int32)[None, :]
    alpha = ((j * k2) % hi).astype(F32) * (2.0 * math.pi / hi)
    beta = ((j * k) % n).astype(F32) * (2.0 * math.pi / n)
    ca, sa = jnp.cos(alpha)[:, :, None], jnp.sin(alpha)[:, :, None]
    cb, sb = jnp.cos(beta)[:, None, :], jnp.sin(beta)[:, None, :]
    scale = 1.0 / math.sqrt(n)
    cn = ((ca * cb - sa * sb) * scale).reshape(n, n).astype(BF16)
    sn = ((sa * cb + ca * sb) * scale).reshape(n, n).astype(BF16)
    return cn, sn


def _channel_dft():
    j = np.arange(FNET_CH)
    ang = 2.0 * np.pi * ((j[:, None] * j[None, :]) % FNET_CH) / FNET_CH
    c = np.kron(np.eye(GROUP_W // FNET_CH), np.cos(ang)) / math.sqrt(FNET_CH)
    s = np.kron(np.eye(GROUP_W // FNET_CH), np.sin(ang)) / math.sqrt(FNET_CH)
    return jnp.asarray(np.concatenate([c, s], axis=1), BF16)


def _rope_tables(n, rotary):
    HW = MLA_HEADS * HEAD_PAD
    cos_blk = np.zeros((HEAD_PAD,), np.float32)
    cos_blk[0:MLA_NOPE + MLA_ROPE] = 1.0
    if not rotary:
        return (jnp.broadcast_to(jnp.asarray(np.tile(cos_blk, MLA_HEADS)), (n, HW)),
                jnp.zeros((n, HW), F32))
    half = MLA_ROPE // 2
    inv = ROPE_BASE ** (-jnp.arange(0, half, 2, dtype=F32) / half)
    t = jnp.arange(n, dtype=jnp.int32)
    row = (t // GRID_W).astype(F32)
    col = (t % GRID_W).astype(F32)
    ar = row[:, None] * inv[None, :]
    ac = col[:, None] * inv[None, :]
    cos32 = jnp.concatenate([jnp.cos(ar), jnp.cos(ar), jnp.cos(ac), jnp.cos(ac)], axis=1)
    sin32 = jnp.concatenate([-jnp.sin(ar), jnp.sin(ar), -jnp.sin(ac), jnp.sin(ac)], axis=1)
    ones = jnp.ones((n, MLA_NOPE), F32)
    zeros_n = jnp.zeros((n, MLA_NOPE), F32)
    zeros_p = jnp.zeros((n, HEAD_PAD - MLA_NOPE - MLA_ROPE), F32)
    cos_h = jnp.concatenate([ones, cos32, zeros_p], axis=1)
    sin_h = jnp.concatenate([zeros_n, sin32, zeros_p], axis=1)
    return jnp.tile(cos_h, (1, MLA_HEADS)), jnp.tile(sin_h, (1, MLA_HEADS))


def _relayout_w_in(w_in):
    D = w_in.shape[0]
    krope = w_in[:, 1152:1184]
    return jnp.concatenate([w_in[:, 0:1184], krope[:, _ROPE_SWAP], jnp.zeros((D, 64), w_in.dtype),
                            w_in[:, 1184:1952]], axis=1).astype(BF16)


def _relayout_mla(w_uq, w_ukv):
    dq = MLA_NOPE + MLA_ROPE
    q_main, q_swap, k_cols, v_cols = [], [], [], []
    zq = jnp.zeros((MLA_Q_LORA, HEAD_PAD - dq), w_uq.dtype)
    zn = jnp.zeros((MLA_Q_LORA, MLA_NOPE), w_uq.dtype)
    zk = jnp.zeros((MLA_KV_LORA, HEAD_PAD - MLA_NOPE), w_ukv.dtype)
    for h in range(MLA_HEADS):
        wq_h = w_uq[:, h * dq:(h + 1) * dq]
        rope = wq_h[:, MLA_NOPE:]
        q_main += [wq_h, zq]
        q_swap += [zn, rope[:, _ROPE_SWAP], zq]
        kv_h = w_ukv[:, h * (MLA_NOPE + MLA_V):(h + 1) * (MLA_NOPE + MLA_V)]
        k_cols += [kv_h[:, 0:MLA_NOPE], zk]
        v_cols += [kv_h[:, MLA_NOPE:]]
    wq2 = jnp.concatenate(q_main + q_swap, axis=1).astype(BF16)
    wk = jnp.concatenate(k_cols, axis=1).astype(BF16)
    wv = jnp.concatenate(v_cols, axis=1).astype(BF16)
    HW = MLA_HEADS * HEAD_PAD
    e2 = np.zeros((2 * MLA_ROPE, 2 * HW), np.float32)
    for h in range(MLA_HEADS):
        for jj in range(MLA_ROPE):
            e2[jj, h * HEAD_PAD + MLA_NOPE + jj] = 1.0
            e2[MLA_ROPE + jj, HW + h * HEAD_PAD + MLA_NOPE + jj] = 1.0
    return wq2, wk, wv, jnp.asarray(e2, BF16)


def _block_diag_heads(w):
    H, d, _ = w.shape
    eye = jnp.eye(H, dtype=w.dtype)
    return (eye[:, None, :, None] * w[:, :, None, :]).reshape(H * d, H * d)


def kernel(x, c, ctx, c_ctx, mod_w, mod_b, norm1_g, norm2_g, w_in, lru_conv_w, lru_conv_b, lru_w_a, lru_b_a, lru_w_i, lru_b_i, lru_lambda, mla_q_norm_g, mla_w_uq, mla_kv_norm_g, mla_w_ukv, sc_conv_w, sc_conv_b, w_out, router_w, router_b, exp_w_gu, exp_b_gu, exp_w_down, exp_b_down, final_g):
    B, n, D = x.shape
    n_ctx = ctx.shape[1]
    depth = mod_w.shape[0]
    tm = min(256, n, n_ctx)

    rows = 8 * ((B + 1 + 7) // 8)
    c_all = jnp.concatenate([c, c_ctx[None, :], jnp.zeros((rows - B - 1, D), F32)], axis=0)
    mod = _modulation(c_all, mod_w, mod_b)

    cs = _channel_dft()
    cn_lat, sn_lat = _dft_tables(n)
    cn_ctx, sn_ctx = _dft_tables(n_ctx)
    cos_lat, sin_lat = _rope_tables(n, True)
    cos_ctx, sin_ctx = _rope_tables(n_ctx, False)

    h_lat, h_ctx = x, ctx
    for l in range(depth):
        with_ctx_out = l < depth - 1
        ml = mod[l, 0:B].reshape(B, 1, 6, D)
        mc = jnp.broadcast_to(mod[l, B:B + 1].reshape(1, 1, 6, D), (B, 1, 6, D))
        part = lambda m, i: m[:, :, i, :]
        w_in_p = _relayout_w_in(w_in[l])
        wq2, wk, wv, e2 = _relayout_mla(mla_w_uq[l], mla_w_ukv[l])
        wai = jnp.stack([jnp.concatenate([_block_diag_heads(lru_w_a[l, d]), _block_diag_heads(lru_w_i[l, d])], axis=1)
                         for d in range(2)]).astype(BF16)
        bai = jnp.concatenate([lru_b_a[l], lru_b_i[l]], axis=1)[:, None, :]
        lru_args = (lru_conv_w[l], lru_conv_b[l][:, None, :], wai, bai, lru_lambda[l][:, None, :])
        w_out_b = w_out[l].astype(BF16)

        zf_c, lru_c, mla_c, u_c, sb_c = _in_projection(h_ctx, norm1_g[l], part(mc, 0), part(mc, 1), w_in_p, cs)
        zf_l, lru_l, mla_l, u_l, sb_l = _in_projection(h_lat, norm1_g[l], part(ml, 0), part(ml, 1), w_in_p, cs)

        h0 = jnp.zeros((B, 2, 1, GROUP_W), F32)
        b_ctx, s_ctx = _rglru(lru_c, h0, *lru_args)
        b_lat, _ = _rglru(lru_l, s_ctx, *lru_args)

        q_c, k_c, v_c = _mla_prep(mla_c, mla_q_norm_g[l], mla_kv_norm_g[l], wq2, wk, wv, e2, cos_ctx, sin_ctx)
        q_l, k_l, v_l = _mla_prep(mla_l, mla_q_norm_g[l], mla_kv_norm_g[l], wq2, wk, wv, e2, cos_lat, sin_lat)
        k_all = jnp.concatenate([k_c, k_l], axis=2)
        v_all = jnp.concatenate([v_c, v_l], axis=1)
        c_lat = _attention(q_l, k_all, v_all)

        a_lat = _fourier_positions(zf_l, cn_lat, sn_lat)
        d_lat = _short_conv(u_l, sb_l, sc_conv_w[l], sc_conv_b[l])

        hn_lat, f_lat, ti_lat, tg_lat = _out_projection(
            a_lat, b_lat, c_lat, d_lat, h_lat, w_out_b, part(ml, 2), norm2_g[l], part(ml, 3), part(ml, 4),
            router_w[l], router_b[l])

        wgu = exp_w_gu[l].astype(BF16)
        wd = exp_w_down[l].astype(BF16)
        g2_lat = part(ml, 5)
        if with_ctx_out:
            a_ctx = _fourier_positions(zf_c, cn_ctx, sn_ctx)
            c_ctx_out = _attention(q_c, k_c, v_c)
            d_ctx = _short_conv(u_c, sb_c, sc_conv_w[l], sc_conv_b[l])
            hn_ctx, f_ctx, ti_ctx, tg_ctx = _out_projection(
                a_ctx, b_ctx, c_ctx_out, d_ctx, h_ctx, w_out_b, part(mc, 2), norm2_g[l], part(mc, 3), part(mc, 4),
                router_w[l], router_b[l])
            S = n_ctx + n
            cat = lambda u, v: jnp.concatenate([u, v], axis=1).reshape(B * S, -1)
            g2_rows = jnp.concatenate([jnp.broadcast_to(part(mc, 5), (B, n_ctx // tm, D)),
                                       jnp.broadcast_to(g2_lat, (B, n // tm, D))], axis=1)
            h_all = _moe(cat(f_ctx, f_lat), cat(hn_ctx, hn_lat), cat(ti_ctx, ti_lat), cat(tg_ctx, tg_lat),
                         g2_rows.reshape(B * S // tm, 1, D), wgu, exp_b_gu[l], wd, exp_b_down[l], tm)
            h_all = h_all.reshape(B, S, D)
            h_ctx, h_lat = h_all[:, 0:n_ctx], h_all[:, n_ctx:]
        else:
            g2_rows = jnp.broadcast_to(g2_lat, (B, n // tm, D)).reshape(B * n // tm, 1, D)
            flat = lambda u: u.reshape(B * n, -1)
            h_lat = _moe(flat(f_lat), flat(hn_lat), flat(ti_lat), flat(tg_lat), g2_rows,
                         wgu, exp_b_gu[l], wd, exp_b_down[l], tm).reshape(B, n, D)

    return _final_norm(h_lat.reshape(B * n, D), final_g).reshape(B, n, D)
```

```python
import functools
import math

import numpy as np
import jax
import jax.numpy as jnp
from jax import lax
from jax.experimental import pallas as pl
from jax.experimental.pallas import tpu as pltpu

F32 = jnp.float32
BF16 = jnp.bfloat16

D_MODEL = 1024
GRID_W = 64
GROUP_W = 256
EPS = 1e-6
FNET_CH = 64
LRU_HEADS = 4
LRU_C = 8.0
MLA_HEADS = 4
MLA_NOPE = 64
MLA_ROPE = 32
MLA_V = 64
MLA_Q_LORA = 256
MLA_KV_LORA = 128
ROPE_BASE = 10000.0
ATTN_SCALE = (MLA_NOPE + MLA_ROPE) ** -0.5
LOG2_E = math.log2(math.e)
N_EXPERTS = 32
TOP_K = 4
SWIGLU_ALPHA = 1.702
SWIGLU_LIMIT = 7.0

HEAD_PAD = 128
P_PAD = 2048
MOE_BM = 512
VMEM_LIMIT = 56 * 1024 * 1024


def _cparams(n_axes, vmem=None):
    return pltpu.CompilerParams(dimension_semantics=("arbitrary",) * n_axes,
                                vmem_limit_bytes=vmem or VMEM_LIMIT)


def _bdot(a, b):
    return jnp.dot(a.astype(BF16), b.astype(BF16), preferred_element_type=F32)


def _mod_kernel(c_ref, w_ref, b_ref, o_ref):
    c = c_ref[...]
    s = c * jax.nn.sigmoid(c)
    o_ref[0] = _bdot(s, w_ref[0]) + b_ref[0]


def _modulation(c_all, mod_w, mod_b):
    L, D, N = mod_w.shape
    R = c_all.shape[0]
    tn = 1024
    return pl.pallas_call(
        _mod_kernel,
        out_shape=jax.ShapeDtypeStruct((L, R, N), F32),
        grid=(L, N // tn),
        in_specs=[pl.BlockSpec((R, D), lambda l, j: (0, 0)),
                  pl.BlockSpec((1, D, tn), lambda l, j: (l, 0, j)),
                  pl.BlockSpec((1, 1, tn), lambda l, j: (l, 0, j))],
        out_specs=pl.BlockSpec((1, R, tn), lambda l, j: (l, 0, j)),
        compiler_params=_cparams(2),
        name="adaln_mod",
    )(c_all, mod_w, mod_b.reshape(L, 1, N))


def _inproj_kernel(h_ref, g_ref, sh_ref, sc_ref, w_ref, cs_ref, zf_ref, lru_ref, mla_ref, u_ref, sb_ref):
    x = h_ref[0]
    y = x * lax.rsqrt(jnp.mean(x * x, axis=-1, keepdims=True) + EPS) * g_ref[...]
    a = y * (1.0 + sc_ref[0]) + sh_ref[0]
    p = _bdot(a, w_ref[...])
    zf_ref[0] = _bdot(p[:, 0:256], cs_ref[...]).astype(BF16)
    lru_ref[0] = p[:, 256:768]
    mla_ref[0] = p[:, 768:1280]
    u_ref[0] = p[:, 1280:1536] * p[:, 1792:2048]
    sb_ref[0] = p[:, 1536:1792]


def _in_projection(h, g, sh, sc, w_in_p, cs):
    B, n, D = h.shape
    tm = min(256, n)
    tile = lambda w: pl.BlockSpec((1, tm, w), lambda b, i: (b, i, 0))
    vec = pl.BlockSpec((1, 1, D), lambda b, i: (b, 0, 0))
    return pl.pallas_call(
        _inproj_kernel,
        out_shape=(jax.ShapeDtypeStruct((B, n, 512), BF16),
                   jax.ShapeDtypeStruct((B, n, 512), F32),
                   jax.ShapeDtypeStruct((B, n, 512), F32),
                   jax.ShapeDtypeStruct((B, n, 256), F32),
                   jax.ShapeDtypeStruct((B, n, 256), F32)),
        grid=(B, n // tm),
        in_specs=[tile(D),
                  pl.BlockSpec((1, D), lambda b, i: (0, 0)),
                  vec, vec,
                  pl.BlockSpec((D, P_PAD), lambda b, i: (0, 0)),
                  pl.BlockSpec((256, 512), lambda b, i: (0, 0))],
        out_specs=(tile(512), tile(512), tile(512), tile(256), tile(256)),
        compiler_params=_cparams(2),
        name="norm_inproj",
    )(h, g.reshape(1, D), sh, sc, w_in_p, cs)


def _fnet_kernel(c_ref, s_ref, z_ref, o_ref):
    z = z_ref[0]
    acc = jnp.dot(c_ref[...], z[:, 0:256], preferred_element_type=F32)
    acc = acc - jnp.dot(s_ref[...], z[:, 256:512], preferred_element_type=F32)
    o_ref[0] = acc.astype(BF16)


def _fourier_positions(zf, cn, sn):
    B, n, _ = zf.shape
    tm = min(512, n)
    return pl.pallas_call(
        _fnet_kernel,
        out_shape=jax.ShapeDtypeStruct((B, n, 256), BF16),
        grid=(n // tm, B),
        in_specs=[pl.BlockSpec((tm, n), lambda i, b: (i, 0)),
                  pl.BlockSpec((tm, n), lambda i, b: (i, 0)),
                  pl.BlockSpec((1, n, 512), lambda i, b: (b, 0, 0))],
        out_specs=pl.BlockSpec((1, tm, 256), lambda i, b: (b, i, 0)),
        compiler_params=_cparams(2),
        name="fourier_positions",
    )(cn, sn, zf)


def _shift_rows(xcat, off, tc):
    if off == 0:
        return xcat[8:8 + tc]
    rolled = pltpu.roll(xcat, (tc + 16) - (8 + off), axis=0)
    return rolled[0:tc]


def _load_halo(ref, col0, width, start, tc, n):
    main = ref[0, pl.ds(start, tc), col0:col0 + width]
    ps = jnp.maximum(start - 8, 0)
    prev = ref[0, pl.ds(pl.multiple_of(ps, 8), 8), col0:col0 + width]
    prev = jnp.where(start > 0, prev, 0.0)
    ns = jnp.minimum(start + tc, n - 8)
    nxt = ref[0, pl.ds(pl.multiple_of(ns, 8), 8), col0:col0 + width]
    nxt = jnp.where(start + tc < n, nxt, 0.0)
    return jnp.concatenate([prev, main, nxt], axis=0)


def _lru_kernel(x_ref, h0_ref, cw_ref, cb_ref, wai_ref, bai_ref, lam_ref, o_ref, hf_ref, hs_ref, *, tc, n):
    nc = n // tc
    row = lax.broadcasted_iota(jnp.int32, (tc, GROUP_W), 0)

    def direction(d, reverse):
        left = 1 if reverse else 2
        cw = cw_ref[d]
        cb = cb_ref[d]
        wai = wai_ref[d]
        bai = bai_ref[d]
        sp = jax.nn.softplus(-lam_ref[d])

        def chunk(ci, carry):
            c = (nc - 1 - ci) if reverse else ci
            start = pl.multiple_of(c * tc, tc)
            xcat = _load_halo(x_ref, 0, GROUP_W, start, tc, n)
            xc = cb
            for k in range(4):
                xc = xc + cw[k:k + 1, :] * _shift_rows(xcat, k - left, tc)
            gates = jax.nn.sigmoid(_bdot(xc, wai) + bai)
            r = gates[:, 0:GROUP_W]
            gi = gates[:, GROUP_W:2 * GROUP_W]
            a = jnp.exp(-LRU_C * r * sp)
            b = jnp.sqrt(jnp.maximum(1.0 - a * a, 0.0)) * (gi * xc)
            s = 1
            while s < tc:
                if reverse:
                    a_sh = pltpu.roll(a, tc - s, axis=0)
                    b_sh = pltpu.roll(b, tc - s, axis=0)
                    keep = row < (tc - s)
                else:
                    a_sh = pltpu.roll(a, s, axis=0)
                    b_sh = pltpu.roll(b, s, axis=0)
                    keep = row >= s
                b = jnp.where(keep, a * b_sh, 0.0) + b
                a = jnp.where(keep, a * a_sh, a)
                s *= 2
            h = a * carry + b
            if reverse:
                gate = x_ref[0, pl.ds(start, tc), GROUP_W:2 * GROUP_W]
                tot = hs_ref[pl.ds(start, tc), :] + h
                o_ref[0, pl.ds(start, tc), :] = (jax.nn.gelu(gate) * tot).astype(o_ref.dtype)
                return h[0:1, :]
            hs_ref[pl.ds(start, tc), :] = h
            return h[tc - 1:tc, :]

        return lax.fori_loop(0, nc, chunk, h0_ref[0, d])

    hf_ref[0, 0] = direction(0, False)
    hf_ref[0, 1] = direction(1, True)


def _rglru(lru, h0, cw, cb, wai, bai, lam):
    B, n, _ = lru.shape
    tc = min(256, n)
    full = lambda *s: pl.BlockSpec(s, lambda b: (0,) * len(s))
    kern = functools.partial(_lru_kernel, tc=tc, n=n)
    return pl.pallas_call(
        kern,
        out_shape=(jax.ShapeDtypeStruct((B, n, GROUP_W), BF16),
                   jax.ShapeDtypeStruct((B, 2, 1, GROUP_W), F32)),
        grid=(B,),
        in_specs=[pl.BlockSpec((1, n, 512), lambda b: (b, 0, 0)),
                  pl.BlockSpec((1, 2, 1, GROUP_W), lambda b: (b, 0, 0, 0)),
                  full(2, 4, GROUP_W), full(2, 1, GROUP_W), full(2, GROUP_W, 2 * GROUP_W),
                  full(2, 1, 2 * GROUP_W), full(2, 1, GROUP_W)],
        out_specs=(pl.BlockSpec((1, n, GROUP_W), lambda b: (b, 0, 0)),
                   pl.BlockSpec((1, 2, 1, GROUP_W), lambda b: (b, 0, 0, 0))),
        scratch_shapes=[pltpu.VMEM((n, GROUP_W), F32)],
        compiler_params=_cparams(1),
        name="rglru",
    )(lru, h0, cw, cb, wai, bai, lam)


def _sconv_kernel(u_ref, sb_ref, w_ref, b_ref, o_ref, *, tc, n):
    w = w_ref[...]
    bias = b_ref[...]

    def chunk(c, _):
        start = pl.multiple_of(c * tc, tc)
        ucat = _load_halo(u_ref, 0, GROUP_W, start, tc, n)
        y = bias
        for k in range(3):
            y = y + w[k:k + 1, :] * _shift_rows(ucat, k - 1, tc)
        o_ref[0, pl.ds(start, tc), :] = (sb_ref[0, pl.ds(start, tc), :] * y).astype(o_ref.dtype)
        return 0

    lax.fori_loop(0, n // tc, chunk, 0)


def _short_conv(u, sb, w, b):
    B, n, W = u.shape
    tc = min(256, n)
    seq = pl.BlockSpec((1, n, W), lambda i: (i, 0, 0))
    return pl.pallas_call(
        functools.partial(_sconv_kernel, tc=tc, n=n),
        out_shape=jax.ShapeDtypeStruct((B, n, W), BF16),
        grid=(B,),
        in_specs=[seq, seq, pl.BlockSpec((3, W), lambda i: (0, 0)), pl.BlockSpec((1, W), lambda i: (0, 0))],
        out_specs=seq,
        compiler_params=_cparams(1),
        name="short_conv",
    )(u, sb, w, b.reshape(1, W))


def _mla_prep_kernel(x_ref, qg_ref, kg_ref, wq_ref, wk_ref, wv_ref, e2_ref, cos_ref, sin_ref,
                     q_ref, k_ref, v_ref):
    x = x_ref[0]
    cq = x[:, 0:256]
    ckv = x[:, 256:384]
    kr = x[:, 384:448]
    cqn = cq * lax.rsqrt(jnp.mean(cq * cq, axis=-1, keepdims=True) + EPS) * qg_ref[...]
    ckvn = ckv * lax.rsqrt(jnp.mean(ckv * ckv, axis=-1, keepdims=True) + EPS) * kg_ref[...]
    cos = cos_ref[...]
    sin = sin_ref[...]
    q2 = _bdot(cqn, wq_ref[...])
    q = (q2[:, 0:512] * cos + q2[:, 512:1024] * sin) * (ATTN_SCALE * LOG2_E)
    kr2 = _bdot(kr, e2_ref[...])
    ckvb = ckvn.astype(BF16)
    k = _bdot(ckvb, wk_ref[...]) + kr2[:, 0:512] * cos + kr2[:, 512:1024] * sin
    vt = lax.dot_general(wv_ref[...], ckvb, (((1,), (1,)), ((), ())), preferred_element_type=F32)
    for h in range(MLA_HEADS):
        q_ref[0, h] = q[:, h * HEAD_PAD:(h + 1) * HEAD_PAD].astype(BF16)
        k_ref[0, h] = k[:, h * HEAD_PAD:(h + 1) * HEAD_PAD].astype(BF16)
    v_ref[0, 0] = vt.astype(BF16)


def _mla_prep(mla, qg, kg, wq2, wk, wv, e2, cos, sin):
    B, n, _ = mla.shape
    tm = min(256, n)
    HW = MLA_HEADS * HEAD_PAD
    full = lambda *s: pl.BlockSpec(s, lambda b, i: (0,) * len(s))
    return pl.pallas_call(
        _mla_prep_kernel,
        out_shape=(jax.ShapeDtypeStruct((B, MLA_HEADS, n, HEAD_PAD), BF16),
                   jax.ShapeDtypeStruct((B, MLA_HEADS, n, HEAD_PAD), BF16),
                   jax.ShapeDtypeStruct((B, n // tm, MLA_HEADS * MLA_V, tm), BF16)),
        grid=(B, n // tm),
        in_specs=[pl.BlockSpec((1, tm, 512), lambda b, i: (b, i, 0)),
                  full(1, MLA_Q_LORA), full(1, MLA_KV_LORA),
                  full(MLA_Q_LORA, 2 * HW), full(MLA_KV_LORA, HW), full(MLA_HEADS * MLA_V, MLA_KV_LORA),
                  full(2 * MLA_ROPE, 2 * HW),
                  pl.BlockSpec((tm, HW), lambda b, i: (i, 0)),
                  pl.BlockSpec((tm, HW), lambda b, i: (i, 0))],
        out_specs=(pl.BlockSpec((1, MLA_HEADS, tm, HEAD_PAD), lambda b, i: (b, 0, i, 0)),
                   pl.BlockSpec((1, MLA_HEADS, tm, HEAD_PAD), lambda b, i: (b, 0, i, 0)),
                   pl.BlockSpec((1, 1, MLA_HEADS * MLA_V, tm), lambda b, i: (b, i, 0, 0))),
        compiler_params=_cparams(2),
        name="mla_prep",
    )(mla, qg.reshape(1, -1), kg.reshape(1, -1), wq2, wk, wv, e2, cos, sin)


def _attn_kernel(q_ref, k_ref, vt_ref, o_ref, st_a, st_b, p_a, p_b, *, tk, n_kt):
    tq = q_ref.shape[2]
    H = MLA_HEADS
    qs = [q_ref[0, h] for h in range(H)]

    def scores(j, h):
        kt = k_ref[0, h, pl.ds(pl.multiple_of(j * tk, tk), tk), :]
        return lax.dot_general(kt, qs[h], (((1,), (1,)), ((), ())), preferred_element_type=F32)

    def weighted_values(j, h, p):
        vt = vt_ref[0, j, h * MLA_V:(h + 1) * MLA_V, :]
        return jnp.dot(vt, p, preferred_element_type=F32)

    def step(j, st_cur, st_next, p_prev, p_cur, carry):
        a_prev, m, l, acc = carry
        jn = jnp.minimum(j + 1, n_kt - 1)
        for h in range(H):
            st_next[h] = scores(jn, h)
        pv = [weighted_values(jnp.maximum(j - 1, 0), h, p_prev[h]) for h in range(H)]
        m_new, alpha, l_new = [], [], []
        for h in range(H):
            st = st_cur[h]
            mh = jnp.maximum(m[h], jnp.max(st, axis=0, keepdims=True))
            ah = jnp.exp2(m[h] - mh)
            ph = jnp.exp2(st - mh)
            l_new.append(ah * l[h] + jnp.sum(ph, axis=0, keepdims=True))
            m_new.append(mh)
            alpha.append(ah)
            p_cur[h] = ph.astype(BF16)
        acc_new = [a_prev[h] * acc[h] + pv[h] for h in range(H)]
        return (tuple(alpha), tuple(m_new), tuple(l_new), tuple(acc_new))

    even = lambda j, c: step(j, st_a, st_b, p_b, p_a, c)
    odd = lambda j, c: step(j, st_b, st_a, p_a, p_b, c)

    for h in range(H):
        st_a[h] = scores(0, h)
    p_b[...] = jnp.zeros(p_b.shape, BF16)
    per_head = lambda f: tuple(f() for _ in range(H))
    carry = (per_head(lambda: jnp.ones((1, tq), F32)),
             per_head(lambda: jnp.full((1, tq), -jnp.inf, F32)),
             per_head(lambda: jnp.zeros((1, tq), F32)),
             per_head(lambda: jnp.zeros((MLA_V, tq), F32)))
    carry = lax.fori_loop(0, n_kt // 2, lambda i, c: odd(2 * i + 1, even(2 * i, c)), carry)
    if n_kt % 2:
        carry = even(n_kt - 1, carry)
    p_last = p_a if n_kt % 2 else p_b
    a_last, _, l, acc = carry
    outs = [(a_last[h] * acc[h] + weighted_values(n_kt - 1, h, p_last[h])) / l[h] for h in range(H)]
    o_ref[0] = jnp.concatenate(outs, axis=0).T.astype(o_ref.dtype)


def _attention(q, k, vt):
    B, H, n, _ = q.shape
    M = k.shape[2]
    n_kt, tk = vt.shape[1], vt.shape[3]
    tq = min(256, n)
    return pl.pallas_call(
        functools.partial(_attn_kernel, tk=tk, n_kt=n_kt),
        out_shape=jax.ShapeDtypeStruct((B, n, H * MLA_V), BF16),
        grid=(B, n // tq),
        in_specs=[pl.BlockSpec((1, H, tq, HEAD_PAD), lambda b, i: (b, 0, i, 0)),
                  pl.BlockSpec((1, H, M, HEAD_PAD), lambda b, i: (b, 0, 0, 0)),
                  pl.BlockSpec((1, n_kt, H * MLA_V, tk), lambda b, i: (b, 0, 0, 0))],
        out_specs=pl.BlockSpec((1, tq, H * MLA_V), lambda b, i: (b, i, 0)),
        scratch_shapes=[pltpu.VMEM((H, tk, tq), F32), pltpu.VMEM((H, tk, tq), F32),
                        pltpu.VMEM((H, tk, tq), BF16), pltpu.VMEM((H, tk, tq), BF16)],
        compiler_params=_cparams(2),
        name="mla_attention",
    )(q, k, vt)


def _outproj_kernel(a_ref, b_ref, c_ref, d_ref, h_ref, w_ref, g1_ref, ng_ref, sh_ref, sc_ref, rw_ref, rb_ref,
                    hn_ref, f_ref, ti_ref, tg_ref):
    mixed = jnp.concatenate([a_ref[0], b_ref[0], c_ref[0], d_ref[0]], axis=-1)
    hn = h_ref[0] + g1_ref[0] * jnp.dot(mixed, w_ref[...], preferred_element_type=F32)
    hn_ref[0] = hn
    y = hn * lax.rsqrt(jnp.mean(hn * hn, axis=-1, keepdims=True) + EPS) * ng_ref[...]
    f = y * (1.0 + sc_ref[0]) + sh_ref[0]
    f_ref[0] = f
    nt = (((1,), (1,)), ((), ()))
    rw = rw_ref[...]
    rw_hi = rw.astype(BF16)
    rw_lo = (rw - rw_hi.astype(F32)).astype(BF16)
    f_hi = f.astype(BF16)
    f_lo = (f - f_hi.astype(F32)).astype(BF16)
    logits = (lax.dot_general(rw_hi, f_hi, nt, preferred_element_type=F32)
              + (lax.dot_general(rw_hi, f_lo, nt, preferred_element_type=F32)
                 + lax.dot_general(rw_lo, f_hi, nt, preferred_element_type=F32))) + rb_ref[...]
    tm = logits.shape[1]
    sub = lax.broadcasted_iota(jnp.int32, (N_EXPERTS, tm), 0)
    row = lax.broadcasted_iota(jnp.int32, (8, tm), 0)
    top_v = jnp.full((8, tm), -jnp.inf, F32)
    top_i = jnp.zeros((8, tm), jnp.int32)
    cur = logits
    m0 = None
    for kk in range(TOP_K):
        m = jnp.max(cur, axis=0, keepdims=True)
        sel = jnp.min(jnp.where(cur == m, sub, N_EXPERTS), axis=0, keepdims=True)
        if kk == 0:
            m0 = m
        top_v = jnp.where(row == kk, m, top_v)
        top_i = jnp.where(row == kk, sel, top_i)
        cur = jnp.where(sub == sel, -jnp.inf, cur)
    e = jnp.where(row < TOP_K, jnp.exp(top_v - m0), 0.0)
    ti_ref[0] = top_i
    tg_ref[0] = e / jnp.sum(e, axis=0, keepdims=True)


def _out_projection(a, b, c, d, h, w_out, g1, ng, sh, sc, rw, rb):
    B, n, D = h.shape
    tm = min(256, n)
    tile = lambda w: pl.BlockSpec((1, tm, w), lambda bb, i: (bb, i, 0))
    vec = pl.BlockSpec((1, 1, D), lambda bb, i: (bb, 0, 0))
    full = lambda *s: pl.BlockSpec(s, lambda bb, i: (0,) * len(s))
    return pl.pallas_call(
        _outproj_kernel,
        out_shape=(jax.ShapeDtypeStruct((B, n, D), F32),
                   jax.ShapeDtypeStruct((B, n, D), F32),
                   jax.ShapeDtypeStruct((B, 8, n), jnp.int32),
                   jax.ShapeDtypeStruct((B, 8, n), F32)),
        grid=(B, n // tm),
        in_specs=[tile(256), tile(256), tile(256), tile(256), tile(D),
                  full(D, D), vec, full(1, D), vec, vec, full(N_EXPERTS, D), full(N_EXPERTS, 1)],
        out_specs=(tile(D), tile(D), pl.BlockSpec((1, 8, tm), lambda bb, i: (bb, 0, i)),
                   pl.BlockSpec((1, 8, tm), lambda bb, i: (bb, 0, i))),
        compiler_params=_cparams(2),
        name="outproj_router",
    )(a, b, c, d, h, w_out, g1, ng.reshape(1, D), sh, sc, rw.T, rb.reshape(N_EXPERTS, 1))


def _rank_kernel(ti_ref, rank_ref, cnt_ref, carry_ref, *, tm):
    @pl.when(pl.program_id(0) == 0)
    def _():
        carry_ref[...] = jnp.zeros_like(carry_ref)

    ti = ti_ref[...]
    sub = lax.broadcasted_iota(jnp.int32, (N_EXPERTS, tm), 0)
    hot = jnp.zeros((N_EXPERTS, tm), F32)
    for kk in range(TOP_K):
        hot = hot + jnp.where(sub == ti[kk:kk + 1, :], 1.0, 0.0)
    r = lax.broadcasted_iota(jnp.int32, (tm, tm), 0)
    c = lax.broadcasted_iota(jnp.int32, (tm, tm), 1)
    earlier = jnp.where(r < c, 1.0, 0.0).astype(BF16)
    carry = carry_ref[:, 0:1]
    before = jnp.dot(hot.astype(BF16), earlier, preferred_element_type=F32) + carry
    row = lax.broadcasted_iota(jnp.int32, (8, tm), 0)
    out = jnp.zeros((8, tm), jnp.int32)
    for kk in range(TOP_K):
        rk = jnp.sum(jnp.where(sub == ti[kk:kk + 1, :], before, 0.0), axis=0, keepdims=True)
        out = jnp.where(row == kk, rk.astype(jnp.int32), out)
    rank_ref[...] = out
    total = carry + jnp.sum(hot, axis=1, keepdims=True)
    carry_ref[...] = jnp.broadcast_to(total, carry_ref.shape)
    cnt_ref[...] = jnp.broadcast_to(total, cnt_ref.shape).astype(jnp.int32)


def _expert_ranks(top_i_t):
    T = top_i_t.shape[1]
    tm = 512 if T % 512 == 0 else 256
    return pl.pallas_call(
        functools.partial(_rank_kernel, tm=tm),
        out_shape=(jax.ShapeDtypeStruct((8, T), jnp.int32), jax.ShapeDtypeStruct((N_EXPERTS, 128), jnp.int32)),
        grid=(T // tm,),
        in_specs=[pl.BlockSpec((8, tm), lambda i: (0, i))],
        out_specs=(pl.BlockSpec((8, tm), lambda i: (0, i)), pl.BlockSpec((N_EXPERTS, 128), lambda i: (0, 0))),
        scratch_shapes=[pltpu.VMEM((N_EXPERTS, 128), F32)],
        compiler_params=_cparams(1),
        name="moe_rank",
    )(top_i_t)


def _dispatch_kernel(cnt_ref, ps_ref, pe_ref, pos_ref, f_ref, xs_ref, buf_ref, zrow_ref, sem_ref, zsem_ref,
                     *, tm, n_slots):
    i = pl.program_id(0)
    nt = pl.num_programs(0)
    rows = tm * TOP_K

    zrows = zrow_ref.shape[0]

    def zero_fill(lo, hi, start):
        def chunk(off, size):
            cp = pltpu.make_async_copy(zrow_ref.at[pl.ds(0, size)], xs_ref.at[pl.ds(off, size)], zsem_ref)
            cp.start() if start else cp.wait()

        lo8 = jnp.minimum((lo + 7) // 8 * 8, hi)

        def single(s, _):
            chunk(s, 1)
            return 0

        lax.fori_loop(lo, lo8, single, 0)
        n = hi - lo8

        def whole(j, _):
            chunk(pl.multiple_of(lo8 + j * zrows, 8), zrows)
            return 0

        lax.fori_loop(0, n // zrows, whole, 0)
        rem_lo = lo8 + (n // zrows) * zrows
        rem = n % zrows
        size = zrows // 2
        while size >= 8:
            @pl.when((rem & size) != 0)
            def _(size=size):
                chunk(pl.multiple_of(rem_lo + (rem & ~(2 * size - 1)), 8), size)
            size //= 2

    @pl.when(i == 0)
    def _():
        zrow_ref[...] = jnp.zeros_like(zrow_ref)
        for start in (True, False):
            for e in range(N_EXPERTS):
                zero_fill(ps_ref[e] + cnt_ref[e], pe_ref[e], start)
            zero_fill(pe_ref[N_EXPERTS - 1], n_slots, start)

    slot = i % 2
    buf_ref[slot] = f_ref[...]

    def issue(r, _):
        pltpu.make_async_copy(buf_ref.at[slot, pl.ds(r // TOP_K, 1)], xs_ref.at[pl.ds(pos_ref[0, 0, r], 1)],
                              sem_ref.at[slot]).start()
        return 0

    lax.fori_loop(0, rows, issue, 0, unroll=8)

    def wait_rows(s):
        pltpu.make_async_copy(xs_ref.at[pl.ds(0, rows)], xs_ref.at[pl.ds(0, rows)], sem_ref.at[s]).wait()

    @pl.when(i > 0)
    def _():
        wait_rows(1 - slot)

    @pl.when(i == nt - 1)
    def _():
        wait_rows(slot)


def _dispatch(f, pos, counts, pad_start, pad_end, n_slots, tm):
    T, D = f.shape
    nt = T // tm
    grid_spec = pltpu.PrefetchScalarGridSpec(
        num_scalar_prefetch=3,
        grid=(nt,),
        in_specs=[pl.BlockSpec((1, 1, tm * TOP_K), lambda i, *_: (i, 0, 0), memory_space=pltpu.SMEM),
                  pl.BlockSpec((tm, D), lambda i, *_: (i, 0))],
        out_specs=pl.BlockSpec(memory_space=pl.ANY),
        scratch_shapes=[pltpu.VMEM((2, tm, D), F32), pltpu.VMEM((256, D), F32),
                        pltpu.SemaphoreType.DMA((2,)), pltpu.SemaphoreType.DMA],
    )
    return pl.pallas_call(
        functools.partial(_dispatch_kernel, tm=tm, n_slots=n_slots),
        out_shape=jax.ShapeDtypeStruct((n_slots, D), F32),
        grid_spec=grid_spec,
        compiler_params=_cparams(1),
        name="moe_dispatch",
    )(counts, pad_start, pad_end, pos.reshape(nt, 1, tm * TOP_K), f)


def _ffn_kernel(be_ref, bv_ref, x_ref, wgu_ref, bgu_ref, wd_ref, bd_ref, y_ref, wgu_s, wd_s, *, bm):
    i = pl.program_id(0)
    valid = bv_ref[i]
    half = bm // 2

    def experts_rows(rows):
        De = wd_ref.shape[2]
        gu = jnp.dot(x_ref[0:rows, :].astype(BF16), wgu_s[...], preferred_element_type=F32) + bgu_ref[0, 0]
        g = jnp.minimum(gu[:, 0:De], SWIGLU_LIMIT)
        u = jnp.clip(gu[:, De:2 * De], -SWIGLU_LIMIT, SWIGLU_LIMIT)
        act = (u + 1.0) * (g * jax.nn.sigmoid(SWIGLU_ALPHA * g))
        y_ref[0:rows, :] = jnp.dot(act.astype(BF16), wd_s[...], preferred_element_type=F32) + bd_ref[0, 0]

    @pl.when(valid > 0)
    def _():
        @pl.when((i == 0) | (be_ref[i] != be_ref[jnp.maximum(i - 1, 0)]))
        def _():
            wgu_s[...] = wgu_ref[0, 0].astype(BF16)
            wd_s[...] = wd_ref[0, 0].astype(BF16)

        @pl.when(valid > half)
        def _():
            experts_rows(bm)

        @pl.when(valid <= half)
        def _():
            experts_rows(half)
            y_ref[half:bm, :] = jnp.zeros((bm - half, y_ref.shape[1]), y_ref.dtype)

    @pl.when(valid <= 0)
    def _():
        y_ref[...] = jnp.zeros_like(y_ref)


def _expert_ffn(xs, block_exp, block_valid, wgu, bgu, wd, bd, layer, bm):
    n_slots, D = xs.shape
    L, E, _, G = wgu.shape
    nb = n_slots // bm
    wmap = lambda i, be, bv: (layer, be[i], 0, 0)
    grid_spec = pltpu.PrefetchScalarGridSpec(
        num_scalar_prefetch=2,
        grid=(nb,),
        in_specs=[pl.BlockSpec((bm, D), lambda i, be, bv: (i, 0)),
                  pl.BlockSpec((1, 1, D, G), wmap),
                  pl.BlockSpec((1, 1, 1, G), wmap),
                  pl.BlockSpec((1, 1, G // 2, D), wmap),
                  pl.BlockSpec((1, 1, 1, D), wmap)],
        out_specs=pl.BlockSpec((bm, D), lambda i, be, bv: (i, 0)),
        scratch_shapes=[pltpu.VMEM((D, G), BF16), pltpu.VMEM((G // 2, D), BF16)],
    )
    return pl.pallas_call(
        functools.partial(_ffn_kernel, bm=bm),
        out_shape=jax.ShapeDtypeStruct((n_slots, D), F32),
        grid_spec=grid_spec,
        compiler_params=_cparams(1),
        name="moe_expert_ffn",
    )(block_exp, block_valid, xs, wgu, bgu.reshape(L, E, 1, G), wd, bd.reshape(L, E, 1, D))


def _combine_kernel(pos_ref, y_ref, h_ref, g2_ref, gate_ref, o_ref, buf_ref, sem_ref, *, tm):
    def issue(r, _):
        for kk in range(TOP_K):
            slot = pos_ref[0, 0, r * TOP_K + kk]
            pltpu.make_async_copy(y_ref.at[pl.ds(slot, 1)], buf_ref.at[kk, pl.ds(r, 1)], sem_ref).start()
        return 0

    lax.fori_loop(0, tm, issue, 0)
    pltpu.make_async_copy(buf_ref, buf_ref, sem_ref).wait()
    gate = gate_ref[...]
    acc = gate[:, 0:1] * buf_ref[0]
    for kk in range(1, TOP_K):
        acc = acc + gate[:, kk:kk + 1] * buf_ref[kk]
    o_ref[...] = h_ref[...] + g2_ref[0] * acc


def _combine(y, pos, h, g2_tiles, gates, tm):
    T, D = h.shape
    nt = T // tm
    return pl.pallas_call(
        functools.partial(_combine_kernel, tm=tm),
        out_shape=jax.ShapeDtypeStruct((T, D), F32),
        grid=(nt,),
        in_specs=[pl.BlockSpec((1, 1, tm * TOP_K), lambda i: (i, 0, 0), memory_space=pltpu.SMEM),
                  pl.BlockSpec(memory_space=pl.ANY),
                  pl.BlockSpec((tm, D), lambda i: (i, 0)),
                  pl.BlockSpec((1, 1, D), lambda i: (i, 0, 0)),
                  pl.BlockSpec((tm, 8), lambda i: (i, 0))],
        out_specs=pl.BlockSpec((tm, D), lambda i: (i, 0)),
        scratch_shapes=[pltpu.VMEM((TOP_K, tm, D), F32), pltpu.SemaphoreType.DMA],
        compiler_params=_cparams(1),
        name="moe_combine",
    )(pos.reshape(nt, 1, tm * TOP_K), y, h, g2_tiles, gates)


def _moe(f, h, top_i_t, top_g_t, g2_tiles, wgu, bgu, wd, bd, layer, tm):
    T, D = f.shape
    bm = MOE_BM
    n_asg = T * TOP_K
    rank_t, cnt = _expert_ranks(top_i_t)
    counts = cnt[:, 0]
    padded = (counts + bm - 1) // bm * bm
    pad_end = jnp.cumsum(padded).astype(jnp.int32)
    pad_start = pad_end - padded
    ti = top_i_t[0:TOP_K]
    expert_ids = jnp.arange(N_EXPERTS, dtype=jnp.int32)
    start_of = jnp.sum(jnp.where(ti[:, :, None] == expert_ids[None, None, :], pad_start[None, None, :], 0), axis=-1)
    pos = (start_of + rank_t[0:TOP_K]).astype(jnp.int32).T.reshape(-1)
    nb = (n_asg + N_EXPERTS * (bm - 1) + bm - 1) // bm
    n_slots = nb * bm
    starts = jnp.arange(nb, dtype=jnp.int32) * bm
    block_exp = jnp.minimum(jnp.sum((pad_end[None, :] <= starts[:, None]).astype(jnp.int32), axis=1), N_EXPERTS - 1)
    block_valid = jnp.clip(pad_start[block_exp] + counts[block_exp] - starts, 0, bm)
    block_valid = jnp.where(starts < pad_end[-1], block_valid, 0).astype(jnp.int32)

    xs = _dispatch(f, pos, counts, pad_start, pad_end, n_slots, tm)
    y = _expert_ffn(xs, block_exp, block_valid, wgu, bgu, wd, bd, layer, bm)
    return _combine(y, pos, h, g2_tiles, top_g_t.T, tm)


def _final_norm_kernel(h_ref, g_ref, o_ref):
    x = h_ref[...]
    o_ref[...] = x * lax.rsqrt(jnp.mean(x * x, axis=-1, keepdims=True) + EPS) * g_ref[...]


def _final_norm(h, g):
    T, D = h.shape
    tm = min(512, T)
    return pl.pallas_call(
        _final_norm_kernel,
        out_shape=jax.ShapeDtypeStruct((T, D), F32),
        grid=(T // tm,),
        in_specs=[pl.BlockSpec((tm, D), lambda i: (i, 0)), pl.BlockSpec((1, D), lambda i: (0, 0))],
        out_specs=pl.BlockSpec((tm, D), lambda i: (i, 0)),
        compiler_params=_cparams(1),
        name="final_norm",
    )(h, g.reshape(1, D))


_ROPE_SWAP = np.concatenate([np.arange(8, 16), np.arange(0, 8), np.arange(24, 32), np.arange(16, 24)])


def _dft_tables(n):
    hi = n // 64
    j = jnp.arange(n, dtype=jnp.int32)[:, None]
    k = jnp.arange(64, dtype=jnp.int32)[None, :]
    k2 = jnp.arange(hi, dtype=jnp.int32)[None, :]
    alpha = ((j * k2) % hi).astype(F32) * (2.0 * math.pi / hi)
    beta = ((j * k) % n).astype(F32) * (2.0 * math.pi / n)
    ca, sa = jnp.cos(alpha)[:, :, None], jnp.sin(alpha)[:, :, None]
    cb, sb = jnp.cos(beta)[:, None, :], jnp.sin(beta)[:, None, :]
    scale = 1.0 / math.sqrt(n)
    cn = ((ca * cb - sa * sb) * scale).reshape(n, n).astype(BF16)
    sn = ((sa * cb + ca * sb) * scale).reshape(n, n).astype(BF16)
    return cn, sn


def _channel_dft():
    j = jnp.arange(GROUP_W, dtype=jnp.int32)
    same_group = (j[:, None] // FNET_CH) == (j[None, :] // FNET_CH)
    ang = ((j[:, None] * j[None, :]) % FNET_CH).astype(F32) * (2.0 * math.pi / FNET_CH)
    scale = 1.0 / math.sqrt(FNET_CH)
    c = jnp.where(same_group, jnp.cos(ang) * scale, 0.0)
    s = jnp.where(same_group, jnp.sin(ang) * scale, 0.0)
    return jnp.concatenate([c, s], axis=1).astype(BF16)


def _rope_tables(n, rotary):
    HW = MLA_HEADS * HEAD_PAD
    cos_blk = np.zeros((HEAD_PAD,), np.float32)
    cos_blk[0:MLA_NOPE + MLA_ROPE] = 1.0
    if not rotary:
        return (jnp.broadcast_to(jnp.asarray(np.tile(cos_blk, MLA_HEADS)), (n, HW)),
                jnp.zeros((n, HW), F32))
    half = MLA_ROPE // 2
    inv = ROPE_BASE ** (-jnp.arange(0, half, 2, dtype=F32) / half)
    t = jnp.arange(n, dtype=jnp.int32)
    row = (t // GRID_W).astype(F32)
    col = (t % GRID_W).astype(F32)
    ar = row[:, None] * inv[None, :]
    ac = col[:, None] * inv[None, :]
    cos32 = jnp.concatenate([jnp.cos(ar), jnp.cos(ar), jnp.cos(ac), jnp.cos(ac)], axis=1)
    sin32 = jnp.concatenate([-jnp.sin(ar), jnp.sin(ar), -jnp.sin(ac), jnp.sin(ac)], axis=1)
    ones = jnp.ones((n, MLA_NOPE), F32)
    zeros_n = jnp.zeros((n, MLA_NOPE), F32)
    zeros_p = jnp.zeros((n, HEAD_PAD - MLA_NOPE - MLA_ROPE), F32)
    cos_h = jnp.concatenate([ones, cos32, zeros_p], axis=1)
    sin_h = jnp.concatenate([zeros_n, sin32, zeros_p], axis=1)
    return jnp.tile(cos_h, (1, MLA_HEADS)), jnp.tile(sin_h, (1, MLA_HEADS))


def _relayout_w_in(w_in):
    D = w_in.shape[0]
    krope = w_in[:, 1152:1184]
    return jnp.concatenate([w_in[:, 0:1184], krope[:, _ROPE_SWAP], jnp.zeros((D, 64), w_in.dtype),
                            w_in[:, 1184:1952]], axis=1).astype(BF16)


def _relayout_mla(w_uq, w_ukv):
    dq = MLA_NOPE + MLA_ROPE
    q_main, q_swap, k_cols, v_cols = [], [], [], []
    zq = jnp.zeros((MLA_Q_LORA, HEAD_PAD - dq), w_uq.dtype)
    zn = jnp.zeros((MLA_Q_LORA, MLA_NOPE), w_uq.dtype)
    zk = jnp.zeros((MLA_KV_LORA, HEAD_PAD - MLA_NOPE), w_ukv.dtype)
    for h in range(MLA_HEADS):
        wq_h = w_uq[:, h * dq:(h + 1) * dq]
        rope = wq_h[:, MLA_NOPE:]
        q_main += [wq_h, zq]
        q_swap += [zn, rope[:, _ROPE_SWAP], zq]
        kv_h = w_ukv[:, h * (MLA_NOPE + MLA_V):(h + 1) * (MLA_NOPE + MLA_V)]
        k_cols += [kv_h[:, 0:MLA_NOPE], zk]
        v_cols += [kv_h[:, MLA_NOPE:]]
    wq2 = jnp.concatenate(q_main + q_swap, axis=1).astype(BF16)
    wk = jnp.concatenate(k_cols, axis=1).astype(BF16)
    wv = jnp.concatenate(v_cols, axis=1).T.astype(BF16)
    HW = MLA_HEADS * HEAD_PAD
    e2 = np.zeros((2 * MLA_ROPE, 2 * HW), np.float32)
    for h in range(MLA_HEADS):
        for jj in range(MLA_ROPE):
            e2[jj, h * HEAD_PAD + MLA_NOPE + jj] = 1.0
            e2[MLA_ROPE + jj, HW + h * HEAD_PAD + MLA_NOPE + jj] = 1.0
    return wq2, wk, wv, jnp.asarray(e2, BF16)


def _block_diag_heads(w):
    H, d, _ = w.shape
    eye = jnp.eye(H, dtype=w.dtype)
    return (eye[:, None, :, None] * w[:, :, None, :]).reshape(H * d, H * d)


def kernel(x, c, ctx, c_ctx, mod_w, mod_b, norm1_g, norm2_g, w_in, lru_conv_w, lru_conv_b, lru_w_a, lru_b_a, lru_w_i, lru_b_i, lru_lambda, mla_q_norm_g, mla_w_uq, mla_kv_norm_g, mla_w_ukv, sc_conv_w, sc_conv_b, w_out, router_w, router_b, exp_w_gu, exp_b_gu, exp_w_down, exp_b_down, final_g):
    B, n, D = x.shape
    n_ctx = ctx.shape[1]
    depth = mod_w.shape[0]
    tm = min(256, n, n_ctx)

    rows = 8 * ((B + 1 + 7) // 8)
    c_all = jnp.concatenate([c, c_ctx[None, :], jnp.zeros((rows - B - 1, D), F32)], axis=0)
    mod = _modulation(c_all, mod_w, mod_b)

    cs = _channel_dft()
    cn_lat, sn_lat = _dft_tables(n)
    cn_ctx, sn_ctx = _dft_tables(n_ctx)
    cos_lat, sin_lat = _rope_tables(n, True)
    cos_ctx, sin_ctx = _rope_tables(n_ctx, False)

    h_lat, h_ctx = x, ctx
    for l in range(depth):
        with_ctx_out = l < depth - 1
        ml = mod[l, 0:B].reshape(B, 1, 6, D)
        mc = jnp.broadcast_to(mod[l, B:B + 1].reshape(1, 1, 6, D), (B, 1, 6, D))
        part = lambda m, i: m[:, :, i, :]
        w_in_p = _relayout_w_in(w_in[l])
        wq2, wk, wv, e2 = _relayout_mla(mla_w_uq[l], mla_w_ukv[l])
        wai = jnp.stack([jnp.concatenate([_block_diag_heads(lru_w_a[l, d]), _block_diag_heads(lru_w_i[l, d])], axis=1)
                         for d in range(2)]).astype(BF16)
        bai = jnp.concatenate([lru_b_a[l], lru_b_i[l]], axis=1)[:, None, :]
        lru_args = (lru_conv_w[l], lru_conv_b[l][:, None, :], wai, bai, lru_lambda[l][:, None, :])
        w_out_b = w_out[l].astype(BF16)

        zf_c, lru_c, mla_c, u_c, sb_c = _in_projection(h_ctx, norm1_g[l], part(mc, 0), part(mc, 1), w_in_p, cs)
        zf_l, lru_l, mla_l, u_l, sb_l = _in_projection(h_lat, norm1_g[l], part(ml, 0), part(ml, 1), w_in_p, cs)

        h0 = jnp.zeros((B, 2, 1, GROUP_W), F32)
        b_ctx, s_ctx = _rglru(lru_c, h0, *lru_args)
        b_lat, _ = _rglru(lru_l, s_ctx, *lru_args)

        q_c, k_c, v_c = _mla_prep(mla_c, mla_q_norm_g[l], mla_kv_norm_g[l], wq2, wk, wv, e2, cos_ctx, sin_ctx)
        q_l, k_l, v_l = _mla_prep(mla_l, mla_q_norm_g[l], mla_kv_norm_g[l], wq2, wk, wv, e2, cos_lat, sin_lat)
        k_all = jnp.concatenate([k_c, k_l], axis=2)
        v_all = jnp.concatenate([v_c, v_l], axis=1)
        c_lat = _attention(q_l, k_all, v_all)

        a_lat = _fourier_positions(zf_l, cn_lat, sn_lat)
        d_lat = _short_conv(u_l, sb_l, sc_conv_w[l], sc_conv_b[l])

        hn_lat, f_lat, ti_lat, tg_lat = _out_projection(
            a_lat, b_lat, c_lat, d_lat, h_lat, w_out_b, part(ml, 2), norm2_g[l], part(ml, 3), part(ml, 4),
            router_w[l], router_b[l])

        moe_w = (exp_w_gu, exp_b_gu, exp_w_down, exp_b_down, l)
        tokens_on_lanes = lambda u: jnp.transpose(u, (1, 0, 2)).reshape(8, -1)
        g2_lat = part(ml, 5)
        if with_ctx_out:
            a_ctx = _fourier_positions(zf_c, cn_ctx, sn_ctx)
            c_ctx_out = _attention(q_c, k_c, v_c)
            d_ctx = _short_conv(u_c, sb_c, sc_conv_w[l], sc_conv_b[l])
            hn_ctx, f_ctx, ti_ctx, tg_ctx = _out_projection(
                a_ctx, b_ctx, c_ctx_out, d_ctx, h_ctx, w_out_b, part(mc, 2), norm2_g[l], part(mc, 3), part(mc, 4),
                router_w[l], router_b[l])
            S = n_ctx + n
            cat = lambda u, v: jnp.concatenate([u, v], axis=1).reshape(B * S, -1)
            g2_rows = jnp.concatenate([jnp.broadcast_to(part(mc, 5), (B, n_ctx // tm, D)),
                                       jnp.broadcast_to(g2_lat, (B, n // tm, D))], axis=1)
            cat_t = lambda u, v: tokens_on_lanes(jnp.concatenate([u, v], axis=2))
            h_all = _moe(cat(f_ctx, f_lat), cat(hn_ctx, hn_lat), cat_t(ti_ctx, ti_lat), cat_t(tg_ctx, tg_lat),
                         g2_rows.reshape(B * S // tm, 1, D), *moe_w, tm)
            h_all = h_all.reshape(B, S, D)
            h_ctx, h_lat = h_all[:, 0:n_ctx], h_all[:, n_ctx:]
        else:
            g2_rows = jnp.broadcast_to(g2_lat, (B, n // tm, D)).reshape(B * n // tm, 1, D)
            flat = lambda u: u.reshape(B * n, -1)
            h_lat = _moe(flat(f_lat), flat(hn_lat), tokens_on_lanes(ti_lat), tokens_on_lanes(tg_lat), g2_rows,
                         *moe_w, tm).reshape(B, n, D)

    return _final_norm(h_lat.reshape(B * n, D), final_g).reshape(B, n, D)
```

```python
import functools
import math

import numpy as np
import jax
import jax.numpy as jnp
from jax import lax
from jax.experimental import pallas as pl
from jax.experimental.pallas import tpu as pltpu

F32 = jnp.float32
BF16 = jnp.bfloat16

D_MODEL = 1024
GRID_W = 64
GROUP_W = 256
EPS = 1e-6
FNET_CH = 64
LRU_HEADS = 4
LRU_C = 8.0
MLA_HEADS = 4
MLA_NOPE = 64
MLA_ROPE = 32
MLA_V = 64
MLA_Q_LORA = 256
MLA_KV_LORA = 128
ROPE_BASE = 10000.0
ATTN_SCALE = (MLA_NOPE + MLA_ROPE) ** -0.5
LOG2_E = math.log2(math.e)
N_EXPERTS = 32
TOP_K = 4
SWIGLU_ALPHA = 1.702
SWIGLU_LIMIT = 7.0

HEAD_PAD = 128
P_PAD = 2048
MOE_BM = 512
VMEM_LIMIT = 56 * 1024 * 1024


def _cparams(n_axes, vmem=None):
    return pltpu.CompilerParams(dimension_semantics=("arbitrary",) * n_axes,
                                vmem_limit_bytes=vmem or VMEM_LIMIT)


def _bdot(a, b):
    return jnp.dot(a.astype(BF16), b.astype(BF16), preferred_element_type=F32)


def _mod_kernel(c_ref, w_ref, b_ref, o_ref):
    c = c_ref[...]
    s = c * jax.nn.sigmoid(c)
    o_ref[0] = _bdot(s, w_ref[0]) + b_ref[0]


def _modulation(c_all, mod_w, mod_b):
    L, D, N = mod_w.shape
    R = c_all.shape[0]
    tn = 1024
    return pl.pallas_call(
        _mod_kernel,
        out_shape=jax.ShapeDtypeStruct((L, R, N), F32),
        grid=(L, N // tn),
        in_specs=[pl.BlockSpec((R, D), lambda l, j: (0, 0)),
                  pl.BlockSpec((1, D, tn), lambda l, j: (l, 0, j)),
                  pl.BlockSpec((1, 1, tn), lambda l, j: (l, 0, j))],
        out_specs=pl.BlockSpec((1, R, tn), lambda l, j: (l, 0, j)),
        compiler_params=_cparams(2),
        name="adaln_mod",
    )(c_all, mod_w, mod_b.reshape(L, 1, N))


def _inproj_kernel(h_ref, g_ref, sh_ref, sc_ref, w_ref, cs_ref, zf_ref, lru_ref, mla_ref, u_ref, sb_ref):
    x = h_ref[0]
    y = x * lax.rsqrt(jnp.mean(x * x, axis=-1, keepdims=True) + EPS) * g_ref[...]
    a = y * (1.0 + sc_ref[0]) + sh_ref[0]
    p = _bdot(a, w_ref[...])
    zf_ref[0] = _bdot(p[:, 0:256], cs_ref[...]).astype(BF16)
    lru_ref[0] = p[:, 256:768]
    mla_ref[0] = p[:, 768:1280]
    u_ref[0] = p[:, 1280:1536] * p[:, 1792:2048]
    sb_ref[0] = p[:, 1536:1792]


def _in_projection(h, g, sh, sc, w_in_p, cs):
    B, n, D = h.shape
    tm = min(256, n)
    tile = lambda w: pl.BlockSpec((1, tm, w), lambda b, i: (b, i, 0))
    vec = pl.BlockSpec((1, 1, D), lambda b, i: (b, 0, 0))
    return pl.pallas_call(
        _inproj_kernel,
        out_shape=(jax.ShapeDtypeStruct((B, n, 512), BF16),
                   jax.ShapeDtypeStruct((B, n, 512), F32),
                   jax.ShapeDtypeStruct((B, n, 512), F32),
                   jax.ShapeDtypeStruct((B, n, 256), F32),
                   jax.ShapeDtypeStruct((B, n, 256), F32)),
        grid=(B, n // tm),
        in_specs=[tile(D),
                  pl.BlockSpec((1, D), lambda b, i: (0, 0)),
                  vec, vec,
                  pl.BlockSpec((D, P_PAD), lambda b, i: (0, 0)),
                  pl.BlockSpec((256, 512), lambda b, i: (0, 0))],
        out_specs=(tile(512), tile(512), tile(512), tile(256), tile(256)),
        compiler_params=_cparams(2),
        name="norm_inproj",
    )(h, g.reshape(1, D), sh, sc, w_in_p, cs)


def _fnet_kernel(c_ref, s_ref, z_ref, o_ref):
    z = z_ref[0]
    acc = jnp.dot(c_ref[...], z[:, 0:256], preferred_element_type=F32)
    acc = acc - jnp.dot(s_ref[...], z[:, 256:512], preferred_element_type=F32)
    o_ref[0] = acc.astype(BF16)


def _fourier_positions(zf, cn, sn):
    B, n, _ = zf.shape
    tm = min(512, n)
    return pl.pallas_call(
        _fnet_kernel,
        out_shape=jax.ShapeDtypeStruct((B, n, 256), BF16),
        grid=(n // tm, B),
        in_specs=[pl.BlockSpec((tm, n), lambda i, b: (i, 0)),
                  pl.BlockSpec((tm, n), lambda i, b: (i, 0)),
                  pl.BlockSpec((1, n, 512), lambda i, b: (b, 0, 0))],
        out_specs=pl.BlockSpec((1, tm, 256), lambda i, b: (b, i, 0)),
        compiler_params=_cparams(2),
        name="fourier_positions",
    )(cn, sn, zf)


def _shift_rows(xcat, off, tc):
    if off == 0:
        return xcat[8:8 + tc]
    rolled = pltpu.roll(xcat, (tc + 16) - (8 + off), axis=0)
    return rolled[0:tc]


def _load_halo(ref, col0, width, start, tc, n):
    main = ref[0, pl.ds(start, tc), col0:col0 + width]
    ps = jnp.maximum(start - 8, 0)
    prev = ref[0, pl.ds(pl.multiple_of(ps, 8), 8), col0:col0 + width]
    prev = jnp.where(start > 0, prev, 0.0)
    ns = jnp.minimum(start + tc, n - 8)
    nxt = ref[0, pl.ds(pl.multiple_of(ns, 8), 8), col0:col0 + width]
    nxt = jnp.where(start + tc < n, nxt, 0.0)
    return jnp.concatenate([prev, main, nxt], axis=0)


def _lru_kernel(x_ref, h0_ref, cw_ref, cb_ref, wai_ref, bai_ref, lam_ref, o_ref, hf_ref, hs_ref, *, tc, n):
    nc = n // tc
    row = lax.broadcasted_iota(jnp.int32, (tc, GROUP_W), 0)

    def direction(d, reverse):
        left = 1 if reverse else 2
        cw = cw_ref[d]
        cb = cb_ref[d]
        wai = wai_ref[d]
        bai = bai_ref[d]
        sp = jax.nn.softplus(-lam_ref[d])

        def chunk(ci, carry):
            c = (nc - 1 - ci) if reverse else ci
            start = pl.multiple_of(c * tc, tc)
            xcat = _load_halo(x_ref, 0, GROUP_W, start, tc, n)
            xc = cb
            for k in range(4):
                xc = xc + cw[k:k + 1, :] * _shift_rows(xcat, k - left, tc)
            gates = jax.nn.sigmoid(_bdot(xc, wai) + bai)
            r = gates[:, 0:GROUP_W]
            gi = gates[:, GROUP_W:2 * GROUP_W]
            a = jnp.exp(-LRU_C * r * sp)
            b = jnp.sqrt(jnp.maximum(1.0 - a * a, 0.0)) * (gi * xc)
            s = 1
            while s < tc:
                if reverse:
                    a_sh = pltpu.roll(a, tc - s, axis=0)
                    b_sh = pltpu.roll(b, tc - s, axis=0)
                    keep = row < (tc - s)
                else:
                    a_sh = pltpu.roll(a, s, axis=0)
                    b_sh = pltpu.roll(b, s, axis=0)
                    keep = row >= s
                b = jnp.where(keep, a * b_sh, 0.0) + b
                a = jnp.where(keep, a * a_sh, a)
                s *= 2
            h = a * carry + b
            if reverse:
                gate = x_ref[0, pl.ds(start, tc), GROUP_W:2 * GROUP_W]
                tot = hs_ref[pl.ds(start, tc), :] + h
                o_ref[0, pl.ds(start, tc), :] = (jax.nn.gelu(gate) * tot).astype(o_ref.dtype)
                return h[0:1, :]
            hs_ref[pl.ds(start, tc), :] = h
            return h[tc - 1:tc, :]

        return lax.fori_loop(0, nc, chunk, h0_ref[0, d])

    hf_ref[0, 0] = direction(0, False)
    hf_ref[0, 1] = direction(1, True)


def _rglru(lru, h0, cw, cb, wai, bai, lam):
    B, n, _ = lru.shape
    tc = min(256, n)
    full = lambda *s: pl.BlockSpec(s, lambda b: (0,) * len(s))
    kern = functools.partial(_lru_kernel, tc=tc, n=n)
    return pl.pallas_call(
        kern,
        out_shape=(jax.ShapeDtypeStruct((B, n, GROUP_W), BF16),
                   jax.ShapeDtypeStruct((B, 2, 1, GROUP_W), F32)),
        grid=(B,),
        in_specs=[pl.BlockSpec((1, n, 512), lambda b: (b, 0, 0)),
                  pl.BlockSpec((1, 2, 1, GROUP_W), lambda b: (b, 0, 0, 0)),
                  full(2, 4, GROUP_W), full(2, 1, GROUP_W), full(2, GROUP_W, 2 * GROUP_W),
                  full(2, 1, 2 * GROUP_W), full(2, 1, GROUP_W)],
        out_specs=(pl.BlockSpec((1, n, GROUP_W), lambda b: (b, 0, 0)),
                   pl.BlockSpec((1, 2, 1, GROUP_W), lambda b: (b, 0, 0, 0))),
        scratch_shapes=[pltpu.VMEM((n, GROUP_W), F32)],
        compiler_params=_cparams(1),
        name="rglru",
    )(lru, h0, cw, cb, wai, bai, lam)


def _sconv_kernel(u_ref, sb_ref, w_ref, b_ref, o_ref, *, tc, n):
    w = w_ref[...]
    bias = b_ref[...]

    def chunk(c, _):
        start = pl.multiple_of(c * tc, tc)
        ucat = _load_halo(u_ref, 0, GROUP_W, start, tc, n)
        y = bias
        for k in range(3):
            y = y + w[k:k + 1, :] * _shift_rows(ucat, k - 1, tc)
        o_ref[0, pl.ds(start, tc), :] = (sb_ref[0, pl.ds(start, tc), :] * y).astype(o_ref.dtype)
        return 0

    lax.fori_loop(0, n // tc, chunk, 0)


def _short_conv(u, sb, w, b):
    B, n, W = u.shape
    tc = min(256, n)
    seq = pl.BlockSpec((1, n, W), lambda i: (i, 0, 0))
    return pl.pallas_call(
        functools.partial(_sconv_kernel, tc=tc, n=n),
        out_shape=jax.ShapeDtypeStruct((B, n, W), BF16),
        grid=(B,),
        in_specs=[seq, seq, pl.BlockSpec((3, W), lambda i: (0, 0)), pl.BlockSpec((1, W), lambda i: (0, 0))],
        out_specs=seq,
        compiler_params=_cparams(1),
        name="short_conv",
    )(u, sb, w, b.reshape(1, W))


def _mla_prep_kernel(x_ref, qg_ref, kg_ref, wq_ref, wk_ref, wv_ref, e2_ref, cos_ref, sin_ref,
                     q_ref, k_ref, v_ref):
    x = x_ref[0]
    cq = x[:, 0:256]
    ckv = x[:, 256:384]
    kr = x[:, 384:448]
    cqn = cq * lax.rsqrt(jnp.mean(cq * cq, axis=-1, keepdims=True) + EPS) * qg_ref[...]
    ckvn = ckv * lax.rsqrt(jnp.mean(ckv * ckv, axis=-1, keepdims=True) + EPS) * kg_ref[...]
    cos = cos_ref[...]
    sin = sin_ref[...]
    q2 = _bdot(cqn, wq_ref[...])
    q = (q2[:, 0:512] * cos + q2[:, 512:1024] * sin) * (ATTN_SCALE * LOG2_E)
    kr2 = _bdot(kr, e2_ref[...])
    ckvb = ckvn.astype(BF16)
    k = _bdot(ckvb, wk_ref[...]) + kr2[:, 0:512] * cos + kr2[:, 512:1024] * sin
    vt = lax.dot_general(wv_ref[...], ckvb, (((1,), (1,)), ((), ())), preferred_element_type=F32)
    for h in range(MLA_HEADS):
        q_ref[0, h] = q[:, h * HEAD_PAD:(h + 1) * HEAD_PAD].astype(BF16)
        k_ref[0, h] = k[:, h * HEAD_PAD:(h + 1) * HEAD_PAD].astype(BF16)
    v_ref[0, 0] = vt.astype(BF16)


def _mla_prep(mla, qg, kg, wq2, wk, wv, e2, cos, sin):
    B, n, _ = mla.shape
    tm = min(256, n)
    HW = MLA_HEADS * HEAD_PAD
    full = lambda *s: pl.BlockSpec(s, lambda b, i: (0,) * len(s))
    return pl.pallas_call(
        _mla_prep_kernel,
        out_shape=(jax.ShapeDtypeStruct((B, MLA_HEADS, n, HEAD_PAD), BF16),
                   jax.ShapeDtypeStruct((B, MLA_HEADS, n, HEAD_PAD), BF16),
                   jax.ShapeDtypeStruct((B, n // tm, MLA_HEADS * MLA_V, tm), BF16)),
        grid=(B, n // tm),
        in_specs=[pl.BlockSpec((1, tm, 512), lambda b, i: (b, i, 0)),
                  full(1, MLA_Q_LORA), full(1, MLA_KV_LORA),
                  full(MLA_Q_LORA, 2 * HW), full(MLA_KV_LORA, HW), full(MLA_HEADS * MLA_V, MLA_KV_LORA),
                  full(2 * MLA_ROPE, 2 * HW),
                  pl.BlockSpec((tm, HW), lambda b, i: (i, 0)),
                  pl.BlockSpec((tm, HW), lambda b, i: (i, 0))],
        out_specs=(pl.BlockSpec((1, MLA_HEADS, tm, HEAD_PAD), lambda b, i: (b, 0, i, 0)),
                   pl.BlockSpec((1, MLA_HEADS, tm, HEAD_PAD), lambda b, i: (b, 0, i, 0)),
                   pl.BlockSpec((1, 1, MLA_HEADS * MLA_V, tm), lambda b, i: (b, i, 0, 0))),
        compiler_params=_cparams(2),
        name="mla_prep",
    )(mla, qg.reshape(1, -1), kg.reshape(1, -1), wq2, wk, wv, e2, cos, sin)


def _attn_kernel(q_ref, k_ref, vt_ref, o_ref, st_a, st_b, p_a, p_b, *, tk, n_kt):
    tq = q_ref.shape[2]
    H = MLA_HEADS
    qs = [q_ref[0, h] for h in range(H)]

    def scores(j, h):
        kt = k_ref[0, h, pl.ds(pl.multiple_of(j * tk, tk), tk), :]
        return lax.dot_general(kt, qs[h], (((1,), (1,)), ((), ())), preferred_element_type=F32)

    def weighted_values(j, h, p):
        vt = vt_ref[0, j, h * MLA_V:(h + 1) * MLA_V, :]
        return jnp.dot(vt, p, preferred_element_type=F32)

    def step(j, st_cur, st_next, p_prev, p_cur, carry):
        a_prev, m, l, acc = carry
        jn = jnp.minimum(j + 1, n_kt - 1)
        for h in range(H):
            st_next[h] = scores(jn, h)
        pv = [weighted_values(jnp.maximum(j - 1, 0), h, p_prev[h]) for h in range(H)]
        m_new, alpha, l_new = [], [], []
        for h in range(H):
            st = st_cur[h]
            mh = jnp.maximum(m[h], jnp.max(st, axis=0, keepdims=True))
            ah = jnp.exp2(m[h] - mh)
            ph = jnp.exp2(st - mh)
            l_new.append(ah * l[h] + jnp.sum(ph, axis=0, keepdims=True))
            m_new.append(mh)
            alpha.append(ah)
            p_cur[h] = ph.astype(BF16)
        acc_new = [a_prev[h] * acc[h] + pv[h] for h in range(H)]
        return (tuple(alpha), tuple(m_new), tuple(l_new), tuple(acc_new))

    even = lambda j, c: step(j, st_a, st_b, p_b, p_a, c)
    odd = lambda j, c: step(j, st_b, st_a, p_a, p_b, c)

    for h in range(H):
        st_a[h] = scores(0, h)
    p_b[...] = jnp.zeros(p_b.shape, BF16)
    per_head = lambda f: tuple(f() for _ in range(H))
    carry = (per_head(lambda: jnp.ones((1, tq), F32)),
             per_head(lambda: jnp.full((1, tq), -jnp.inf, F32)),
             per_head(lambda: jnp.zeros((1, tq), F32)),
             per_head(lambda: jnp.zeros((MLA_V, tq), F32)))
    carry = lax.fori_loop(0, n_kt // 2, lambda i, c: odd(2 * i + 1, even(2 * i, c)), carry)
    if n_kt % 2:
        carry = even(n_kt - 1, carry)
    p_last = p_a if n_kt % 2 else p_b
    a_last, _, l, acc = carry
    outs = [(a_last[h] * acc[h] + weighted_values(n_kt - 1, h, p_last[h])) / l[h] for h in range(H)]
    o_ref[0] = jnp.concatenate(outs, axis=0).T.astype(o_ref.dtype)


def _attention(q, k, vt):
    B, H, n, _ = q.shape
    M = k.shape[2]
    n_kt, tk = vt.shape[1], vt.shape[3]
    tq = min(256, n)
    return pl.pallas_call(
        functools.partial(_attn_kernel, tk=tk, n_kt=n_kt),
        out_shape=jax.ShapeDtypeStruct((B, n, H * MLA_V), BF16),
        grid=(B, n // tq),
        in_specs=[pl.BlockSpec((1, H, tq, HEAD_PAD), lambda b, i: (b, 0, i, 0)),
                  pl.BlockSpec((1, H, M, HEAD_PAD), lambda b, i: (b, 0, 0, 0)),
                  pl.BlockSpec((1, n_kt, H * MLA_V, tk), lambda b, i: (b, 0, 0, 0))],
        out_specs=pl.BlockSpec((1, tq, H * MLA_V), lambda b, i: (b, i, 0)),
        scratch_shapes=[pltpu.VMEM((H, tk, tq), F32), pltpu.VMEM((H, tk, tq), F32),
                        pltpu.VMEM((H, tk, tq), BF16), pltpu.VMEM((H, tk, tq), BF16)],
        compiler_params=_cparams(2),
        name="mla_attention",
    )(q, k, vt)


ROW_TILE = 8


def _store_rows_contiguous(ref, lead, x):
    rows = x.shape[0]
    for j in range(ROW_TILE):
        ref[lead + (pl.ds(j, rows, stride=ROW_TILE), slice(None))] = x[:, j * 128:(j + 1) * 128]


def _load_rows_contiguous(ref, lead, rows, dtype):
    return jnp.concatenate([ref[lead + (pl.ds(j, rows, stride=ROW_TILE), slice(None))].astype(dtype)
                            for j in range(ROW_TILE)], axis=-1)


def _outproj_kernel(a_ref, b_ref, c_ref, d_ref, h_ref, w_ref, g1_ref, ng_ref, sh_ref, sc_ref, rw_ref, rb_ref,
                    hn_ref, f_ref, ti_ref, tg_ref):
    mixed = jnp.concatenate([a_ref[0], b_ref[0], c_ref[0], d_ref[0]], axis=-1)
    hn = h_ref[0] + g1_ref[0] * jnp.dot(mixed, w_ref[...], preferred_element_type=F32)
    hn_ref[0] = hn
    y = hn * lax.rsqrt(jnp.mean(hn * hn, axis=-1, keepdims=True) + EPS) * ng_ref[...]
    f = y * (1.0 + sc_ref[0]) + sh_ref[0]
    _store_rows_contiguous(f_ref, (0,), f)
    nt = (((1,), (1,)), ((), ()))
    rw = rw_ref[...]
    rw_hi = rw.astype(BF16)
    rw_lo = (rw - rw_hi.astype(F32)).astype(BF16)
    f_hi = f.astype(BF16)
    f_lo = (f - f_hi.astype(F32)).astype(BF16)
    logits = (lax.dot_general(rw_hi, f_hi, nt, preferred_element_type=F32)
              + (lax.dot_general(rw_hi, f_lo, nt, preferred_element_type=F32)
                 + lax.dot_general(rw_lo, f_hi, nt, preferred_element_type=F32))) + rb_ref[...]
    tm = logits.shape[1]
    sub = lax.broadcasted_iota(jnp.int32, (N_EXPERTS, tm), 0)
    row = lax.broadcasted_iota(jnp.int32, (8, tm), 0)
    top_v = jnp.full((8, tm), -jnp.inf, F32)
    top_i = jnp.zeros((8, tm), jnp.int32)
    cur = logits
    m0 = None
    for kk in range(TOP_K):
        m = jnp.max(cur, axis=0, keepdims=True)
        sel = jnp.min(jnp.where(cur == m, sub, N_EXPERTS), axis=0, keepdims=True)
        if kk == 0:
            m0 = m
        top_v = jnp.where(row == kk, m, top_v)
        top_i = jnp.where(row == kk, sel, top_i)
        cur = jnp.where(sub == sel, -jnp.inf, cur)
    e = jnp.where(row < TOP_K, jnp.exp(top_v - m0), 0.0)
    ti_ref[0] = top_i
    tg_ref[0] = e / jnp.sum(e, axis=0, keepdims=True)


def _out_projection(a, b, c, d, h, w_out, g1, ng, sh, sc, rw, rb):
    B, n, D = h.shape
    tm = min(256, n)
    tile = lambda w: pl.BlockSpec((1, tm, w), lambda bb, i: (bb, i, 0))
    vec = pl.BlockSpec((1, 1, D), lambda bb, i: (bb, 0, 0))
    full = lambda *s: pl.BlockSpec(s, lambda bb, i: (0,) * len(s))
    return pl.pallas_call(
        _outproj_kernel,
        out_shape=(jax.ShapeDtypeStruct((B, n, D), F32),
                   jax.ShapeDtypeStruct((B, n * ROW_TILE, 128), F32),
                   jax.ShapeDtypeStruct((B, 8, n), jnp.int32),
                   jax.ShapeDtypeStruct((B, 8, n), F32)),
        grid=(B, n // tm),
        in_specs=[tile(256), tile(256), tile(256), tile(256), tile(D),
                  full(D, D), vec, full(1, D), vec, vec, full(N_EXPERTS, D), full(N_EXPERTS, 1)],
        out_specs=(tile(D), pl.BlockSpec((1, tm * ROW_TILE, 128), lambda bb, i: (bb, i, 0)),
                   pl.BlockSpec((1, 8, tm), lambda bb, i: (bb, 0, i)),
                   pl.BlockSpec((1, 8, tm), lambda bb, i: (bb, 0, i))),
        compiler_params=_cparams(2),
        name="outproj_router",
    )(a, b, c, d, h, w_out, g1, ng.reshape(1, D), sh, sc, rw.T, rb.reshape(N_EXPERTS, 1))


def _rank_kernel(ti_ref, rank_ref, cnt_ref, carry_ref, *, tm):
    @pl.when(pl.program_id(0) == 0)
    def _():
        carry_ref[...] = jnp.zeros_like(carry_ref)

    ti = ti_ref[...]
    sub = lax.broadcasted_iota(jnp.int32, (N_EXPERTS, tm), 0)
    hot = jnp.zeros((N_EXPERTS, tm), F32)
    for kk in range(TOP_K):
        hot = hot + jnp.where(sub == ti[kk:kk + 1, :], 1.0, 0.0)
    r = lax.broadcasted_iota(jnp.int32, (tm, tm), 0)
    c = lax.broadcasted_iota(jnp.int32, (tm, tm), 1)
    earlier = jnp.where(r < c, 1.0, 0.0).astype(BF16)
    carry = carry_ref[:, 0:1]
    before = jnp.dot(hot.astype(BF16), earlier, preferred_element_type=F32) + carry
    row = lax.broadcasted_iota(jnp.int32, (8, tm), 0)
    out = jnp.zeros((8, tm), jnp.int32)
    for kk in range(TOP_K):
        rk = jnp.sum(jnp.where(sub == ti[kk:kk + 1, :], before, 0.0), axis=0, keepdims=True)
        out = jnp.where(row == kk, rk.astype(jnp.int32), out)
    rank_ref[...] = out
    total = carry + jnp.sum(hot, axis=1, keepdims=True)
    carry_ref[...] = jnp.broadcast_to(total, carry_ref.shape)
    cnt_ref[...] = jnp.broadcast_to(total, cnt_ref.shape).astype(jnp.int32)


def _expert_ranks(top_i_t):
    T = top_i_t.shape[1]
    tm = 512 if T % 512 == 0 else 256
    return pl.pallas_call(
        functools.partial(_rank_kernel, tm=tm),
        out_shape=(jax.ShapeDtypeStruct((8, T), jnp.int32), jax.ShapeDtypeStruct((N_EXPERTS, 128), jnp.int32)),
        grid=(T // tm,),
        in_specs=[pl.BlockSpec((8, tm), lambda i: (0, i))],
        out_specs=(pl.BlockSpec((8, tm), lambda i: (0, i)), pl.BlockSpec((N_EXPERTS, 128), lambda i: (0, 0))),
        scratch_shapes=[pltpu.VMEM((N_EXPERTS, 128), F32)],
        compiler_params=_cparams(1),
        name="moe_rank",
    )(top_i_t)


def _dispatch_kernel(cnt_ref, ps_ref, pe_ref, pos_ref, f_ref, xs_ref, buf_ref, zrow_ref, sem_ref, zsem_ref,
                     *, tm, n_slots):
    i = pl.program_id(0)
    nt = pl.num_programs(0)
    copies = tm * TOP_K
    zslots = zrow_ref.shape[0] // ROW_TILE

    def tile_rows(first_row, n_rows):
        return pl.ds(pl.multiple_of(first_row * ROW_TILE, ROW_TILE), n_rows * ROW_TILE)

    def zero_fill(lo, hi, start):
        def chunk(off, size):
            cp = pltpu.make_async_copy(zrow_ref.at[tile_rows(0, size)], xs_ref.at[tile_rows(off, size)], zsem_ref)
            cp.start() if start else cp.wait()

        n = hi - lo

        def whole(j, _):
            chunk(lo + j * zslots, zslots)
            return 0

        lax.fori_loop(0, n // zslots, whole, 0)
        rem_lo = lo + (n // zslots) * zslots
        rem = n % zslots
        size = zslots // 2
        while size >= 1:
            @pl.when((rem & size) != 0)
            def _(size=size):
                chunk(rem_lo + (rem & ~(2 * size - 1)), size)
            size //= 2

    @pl.when(i == 0)
    def _():
        zrow_ref[...] = jnp.zeros_like(zrow_ref)
        for start in (True, False):
            for e in range(N_EXPERTS):
                zero_fill(ps_ref[e] + cnt_ref[e], pe_ref[e], start)
            zero_fill(pe_ref[N_EXPERTS - 1], n_slots, start)

    slot = i % 2
    buf_ref[slot] = f_ref[...]

    def issue(r, _):
        pltpu.make_async_copy(buf_ref.at[slot, tile_rows(r // TOP_K, 1)], xs_ref.at[tile_rows(pos_ref[0, 0, r], 1)],
                              sem_ref.at[slot]).start()
        return 0

    lax.fori_loop(0, copies, issue, 0, unroll=8)

    def wait_rows(s):
        pltpu.make_async_copy(xs_ref.at[tile_rows(0, copies)], xs_ref.at[tile_rows(0, copies)], sem_ref.at[s]).wait()

    @pl.when(i > 0)
    def _():
        wait_rows(1 - slot)

    @pl.when(i == nt - 1)
    def _():
        wait_rows(slot)


def _dispatch(f, pos, counts, pad_start, pad_end, n_slots, tm):
    T = f.shape[0] // ROW_TILE
    nt = T // tm
    grid_spec = pltpu.PrefetchScalarGridSpec(
        num_scalar_prefetch=3,
        grid=(nt,),
        in_specs=[pl.BlockSpec((1, 1, tm * TOP_K), lambda i, *_: (i, 0, 0), memory_space=pltpu.SMEM),
                  pl.BlockSpec((tm * ROW_TILE, 128), lambda i, *_: (i, 0))],
        out_specs=pl.BlockSpec(memory_space=pl.ANY),
        scratch_shapes=[pltpu.VMEM((2, tm * ROW_TILE, 128), F32), pltpu.VMEM((256 * ROW_TILE, 128), F32),
                        pltpu.SemaphoreType.DMA((2,)), pltpu.SemaphoreType.DMA],
    )
    return pl.pallas_call(
        functools.partial(_dispatch_kernel, tm=tm, n_slots=n_slots),
        out_shape=jax.ShapeDtypeStruct((n_slots * ROW_TILE, 128), F32),
        grid_spec=grid_spec,
        compiler_params=_cparams(1),
        name="moe_dispatch",
    )(counts, pad_start, pad_end, pos.reshape(nt, 1, tm * TOP_K), f)


def _ffn_kernel(be_ref, bv_ref, x_ref, wgu_ref, bgu_ref, wd_ref, bd_ref, y_ref, wgu_s, wd_s, *, bm):
    i = pl.program_id(0)
    valid = bv_ref[i]
    half = bm // 2

    def experts_rows(rows):
        De = wd_ref.shape[2]
        x = _load_rows_contiguous(x_ref, (), rows, BF16)
        gu = jnp.dot(x, wgu_s[...], preferred_element_type=F32) + bgu_ref[0, 0]
        g = jnp.minimum(gu[:, 0:De], SWIGLU_LIMIT)
        u = jnp.clip(gu[:, De:2 * De], -SWIGLU_LIMIT, SWIGLU_LIMIT)
        act = (u + 1.0) * (g * jax.nn.sigmoid(SWIGLU_ALPHA * g))
        y = jnp.dot(act.astype(BF16), wd_s[...], preferred_element_type=F32) + bd_ref[0, 0]
        _store_rows_contiguous(y_ref, (), y)

    @pl.when(valid > 0)
    def _():
        @pl.when((i == 0) | (be_ref[i] != be_ref[jnp.maximum(i - 1, 0)]))
        def _():
            wgu_s[...] = wgu_ref[0, 0].astype(BF16)
            wd_s[...] = wd_ref[0, 0].astype(BF16)

        @pl.when(valid > half)
        def _():
            experts_rows(bm)

        @pl.when(valid <= half)
        def _():
            experts_rows(half)
            y_ref[half * ROW_TILE:bm * ROW_TILE, :] = jnp.zeros(((bm - half) * ROW_TILE, 128), y_ref.dtype)

    @pl.when(valid <= 0)
    def _():
        y_ref[...] = jnp.zeros_like(y_ref)


def _expert_ffn(xs, block_exp, block_valid, wgu, bgu, wd, bd, layer, bm):
    n_slots = xs.shape[0] // ROW_TILE
    L, E, D, G = wgu.shape
    nb = n_slots // bm
    wmap = lambda i, be, bv: (layer, be[i], 0, 0)
    rows_spec = pl.BlockSpec((bm * ROW_TILE, 128), lambda i, be, bv: (i, 0))
    grid_spec = pltpu.PrefetchScalarGridSpec(
        num_scalar_prefetch=2,
        grid=(nb,),
        in_specs=[rows_spec,
                  pl.BlockSpec((1, 1, D, G), wmap),
                  pl.BlockSpec((1, 1, 1, G), wmap),
                  pl.BlockSpec((1, 1, G // 2, D), wmap),
                  pl.BlockSpec((1, 1, 1, D), wmap)],
        out_specs=rows_spec,
        scratch_shapes=[pltpu.VMEM((D, G), BF16), pltpu.VMEM((G // 2, D), BF16)],
    )
    return pl.pallas_call(
        functools.partial(_ffn_kernel, bm=bm),
        out_shape=jax.ShapeDtypeStruct((n_slots * ROW_TILE, 128), F32),
        grid_spec=grid_spec,
        compiler_params=_cparams(1),
        name="moe_expert_ffn",
    )(block_exp, block_valid, xs, wgu, bgu.reshape(L, E, 1, G), wd, bd.reshape(L, E, 1, D))


def _combine_kernel(pos_ref, y_ref, h_ref, g2_ref, gate_ref, fg_ref, o_ref, buf_ref, sem_ref, *, tm, final_norm):
    def tile_rows(first_row):
        return pl.ds(pl.multiple_of(first_row * ROW_TILE, ROW_TILE), ROW_TILE)

    def issue(r, _):
        for kk in range(TOP_K):
            slot = pos_ref[0, 0, r * TOP_K + kk]
            pltpu.make_async_copy(y_ref.at[tile_rows(slot)], buf_ref.at[kk, tile_rows(r)], sem_ref).start()
        return 0

    lax.fori_loop(0, tm, issue, 0, unroll=2)
    pltpu.make_async_copy(buf_ref, buf_ref, sem_ref).wait()
    gate = gate_ref[...]
    acc = gate[:, 0:1] * _load_rows_contiguous(buf_ref, (0,), tm, F32)
    for kk in range(1, TOP_K):
        acc = acc + gate[:, kk:kk + 1] * _load_rows_contiguous(buf_ref, (kk,), tm, F32)
    out = h_ref[...] + g2_ref[0] * acc
    if final_norm:
        out = out * lax.rsqrt(jnp.mean(out * out, axis=-1, keepdims=True) + EPS) * fg_ref[...]
    o_ref[...] = out


def _combine(y, pos, h, g2_tiles, gates, final_g, tm, final_norm):
    T, D = h.shape
    nt = T // tm
    return pl.pallas_call(
        functools.partial(_combine_kernel, tm=tm, final_norm=final_norm),
        out_shape=jax.ShapeDtypeStruct((T, D), F32),
        grid=(nt,),
        in_specs=[pl.BlockSpec((1, 1, tm * TOP_K), lambda i: (i, 0, 0), memory_space=pltpu.SMEM),
                  pl.BlockSpec(memory_space=pl.ANY),
                  pl.BlockSpec((tm, D), lambda i: (i, 0)),
                  pl.BlockSpec((1, 1, D), lambda i: (i, 0, 0)),
                  pl.BlockSpec((tm, 8), lambda i: (i, 0)),
                  pl.BlockSpec((1, D), lambda i: (0, 0))],
        out_specs=pl.BlockSpec((tm, D), lambda i: (i, 0)),
        scratch_shapes=[pltpu.VMEM((TOP_K, tm * ROW_TILE, 128), F32), pltpu.SemaphoreType.DMA],
        compiler_params=_cparams(1),
        name="moe_combine",
    )(pos.reshape(nt, 1, tm * TOP_K), y, h, g2_tiles, gates, final_g.reshape(1, D))


def _moe(f, h, top_i_t, top_g_t, g2_tiles, wgu, bgu, wd, bd, layer, final_g, tm, final_norm):
    T, D = h.shape
    bm = MOE_BM
    n_asg = T * TOP_K
    rank_t, cnt = _expert_ranks(top_i_t)
    counts = cnt[:, 0]
    padded = (counts + bm - 1) // bm * bm
    pad_end = jnp.cumsum(padded).astype(jnp.int32)
    pad_start = pad_end - padded
    ti = top_i_t[0:TOP_K]
    expert_ids = jnp.arange(N_EXPERTS, dtype=jnp.int32)
    start_of = jnp.sum(jnp.where(ti[:, :, None] == expert_ids[None, None, :], pad_start[None, None, :], 0), axis=-1)
    pos = (start_of + rank_t[0:TOP_K]).astype(jnp.int32).T.reshape(-1)
    nb = (n_asg + N_EXPERTS * (bm - 1) + bm - 1) // bm
    n_slots = nb * bm
    starts = jnp.arange(nb, dtype=jnp.int32) * bm
    block_exp = jnp.minimum(jnp.sum((pad_end[None, :] <= starts[:, None]).astype(jnp.int32), axis=1), N_EXPERTS - 1)
    block_valid = jnp.clip(pad_start[block_exp] + counts[block_exp] - starts, 0, bm)
    block_valid = jnp.where(starts < pad_end[-1], block_valid, 0).astype(jnp.int32)

    xs = _dispatch(f, pos, counts, pad_start, pad_end, n_slots, tm)
    y = _expert_ffn(xs, block_exp, block_valid, wgu, bgu, wd, bd, layer, bm)
    return _combine(y, pos, h, g2_tiles, top_g_t.T, final_g, tm, final_norm)


_ROPE_SWAP = np.concatenate([np.arange(8, 16), np.arange(0, 8), np.arange(24, 32), np.arange(16, 24)])


def _dft_tables(n):
    hi = n // 64
    j = jnp.arange(n, dtype=jnp.int32)[:, None]
    k = jnp.arange(64, dtype=jnp.int32)[None, :]
    k2 = jnp.arange(hi, dtype=jnp.int32)[None, :]
    alpha = ((j * k2) % hi).astype(F32) * (2.0 * math.pi / hi)
    beta = ((j * k) % n).astype(F32) * (2.0 * math.pi / n)
    ca, sa = jnp.cos(alpha)[:, :, None], jnp.sin(alpha)[:, :, None]
    cb, sb = jnp.cos(beta)[:, None, :], jnp.sin(beta)[:, None, :]
    scale = 1.0 / math.sqrt(n)
    cn = ((ca * cb - sa * sb) * scale).reshape(n, n).astype(BF16)
    sn = ((sa * cb + ca * sb) * scale).reshape(n, n).astype(BF16)
    return cn, sn


def _channel_dft():
    j = jnp.arange(GROUP_W, dtype=jnp.int32)
    same_group = (j[:, None] // FNET_CH) == (j[None, :] // FNET_CH)
    ang = ((j[:, None] * j[None, :]) % FNET_CH).astype(F32) * (2.0 * math.pi / FNET_CH)
    scale = 1.0 / math.sqrt(FNET_CH)
    c = jnp.where(same_group, jnp.cos(ang) * scale, 0.0)
    s = jnp.where(same_group, jnp.sin(ang) * scale, 0.0)
    return jnp.concatenate([c, s], axis=1).astype(BF16)


def _rope_tables(n, rotary):
    HW = MLA_HEADS * HEAD_PAD
    cos_blk = np.zeros((HEAD_PAD,), np.float32)
    cos_blk[0:MLA_NOPE + MLA_ROPE] = 1.0
    if not rotary:
        return (jnp.broadcast_to(jnp.asarray(np.tile(cos_blk, MLA_HEADS)), (n, HW)),
                jnp.zeros((n, HW), F32))
    half = MLA_ROPE // 2
    inv = ROPE_BASE ** (-jnp.arange(0, half, 2, dtype=F32) / half)
    t = jnp.arange(n, dtype=jnp.int32)
    row = (t // GRID_W).astype(F32)
    col = (t % GRID_W).astype(F32)
    ar = row[:, None] * inv[None, :]
    ac = col[:, None] * inv[None, :]
    cos32 = jnp.concatenate([jnp.cos(ar), jnp.cos(ar), jnp.cos(ac), jnp.cos(ac)], axis=1)
    sin32 = jnp.concatenate([-jnp.sin(ar), jnp.sin(ar), -jnp.sin(ac), jnp.sin(ac)], axis=1)
    ones = jnp.ones((n, MLA_NOPE), F32)
    zeros_n = jnp.zeros((n, MLA_NOPE), F32)
    zeros_p = jnp.zeros((n, HEAD_PAD - MLA_NOPE - MLA_ROPE), F32)
    cos_h = jnp.concatenate([ones, cos32, zeros_p], axis=1)
    sin_h = jnp.concatenate([zeros_n, sin32, zeros_p], axis=1)
    return jnp.tile(cos_h, (1, MLA_HEADS)), jnp.tile(sin_h, (1, MLA_HEADS))


def _relayout_w_in(w_in):
    D = w_in.shape[0]
    krope = w_in[:, 1152:1184]
    return jnp.concatenate([w_in[:, 0:1184], krope[:, _ROPE_SWAP], jnp.zeros((D, 64), w_in.dtype),
                            w_in[:, 1184:1952]], axis=1).astype(BF16)


def _relayout_mla(w_uq, w_ukv):
    dq = MLA_NOPE + MLA_ROPE
    q_main, q_swap, k_cols, v_cols = [], [], [], []
    zq = jnp.zeros((MLA_Q_LORA, HEAD_PAD - dq), w_uq.dtype)
    zn = jnp.zeros((MLA_Q_LORA, MLA_NOPE), w_uq.dtype)
    zk = jnp.zeros((MLA_KV_LORA, HEAD_PAD - MLA_NOPE), w_ukv.dtype)
    for h in range(MLA_HEADS):
        wq_h = w_uq[:, h * dq:(h + 1) * dq]
        rope = wq_h[:, MLA_NOPE:]
        q_main += [wq_h, zq]
        q_swap += [zn, rope[:, _ROPE_SWAP], zq]
        kv_h = w_ukv[:, h * (MLA_NOPE + MLA_V):(h + 1) * (MLA_NOPE + MLA_V)]
        k_cols += [kv_h[:, 0:MLA_NOPE], zk]
        v_cols += [kv_h[:, MLA_NOPE:]]
    wq2 = jnp.concatenate(q_main + q_swap, axis=1).astype(BF16)
    wk = jnp.concatenate(k_cols, axis=1).astype(BF16)
    wv = jnp.concatenate(v_cols, axis=1).T.astype(BF16)
    HW = MLA_HEADS * HEAD_PAD
    e2 = np.zeros((2 * MLA_ROPE, 2 * HW), np.float32)
    for h in range(MLA_HEADS):
        for jj in range(MLA_ROPE):
            e2[jj, h * HEAD_PAD + MLA_NOPE + jj] = 1.0
            e2[MLA_ROPE + jj, HW + h * HEAD_PAD + MLA_NOPE + jj] = 1.0
    return wq2, wk, wv, jnp.asarray(e2, BF16)


def _block_diag_heads(w):
    H, d, _ = w.shape
    eye = jnp.eye(H, dtype=w.dtype)
    return (eye[:, None, :, None] * w[:, :, None, :]).reshape(H * d, H * d)


def kernel(x, c, ctx, c_ctx, mod_w, mod_b, norm1_g, norm2_g, w_in, lru_conv_w, lru_conv_b, lru_w_a, lru_b_a, lru_w_i, lru_b_i, lru_lambda, mla_q_norm_g, mla_w_uq, mla_kv_norm_g, mla_w_ukv, sc_conv_w, sc_conv_b, w_out, router_w, router_b, exp_w_gu, exp_b_gu, exp_w_down, exp_b_down, final_g):
    B, n, D = x.shape
    n_ctx = ctx.shape[1]
    depth = mod_w.shape[0]
    tm = min(256, n, n_ctx)

    rows = 8 * ((B + 1 + 7) // 8)
    c_all = jnp.concatenate([c, c_ctx[None, :], jnp.zeros((rows - B - 1, D), F32)], axis=0)
    mod = _modulation(c_all, mod_w, mod_b)

    cs = _channel_dft()
    cn_lat, sn_lat = _dft_tables(n)
    cn_ctx, sn_ctx = _dft_tables(n_ctx)
    cos_lat, sin_lat = _rope_tables(n, True)
    cos_ctx, sin_ctx = _rope_tables(n_ctx, False)

    h_lat, h_ctx = x, ctx
    for l in range(depth):
        with_ctx_out = l < depth - 1
        ml = mod[l, 0:B].reshape(B, 1, 6, D)
        mc = jnp.broadcast_to(mod[l, B:B + 1].reshape(1, 1, 6, D), (B, 1, 6, D))
        part = lambda m, i: m[:, :, i, :]
        w_in_p = _relayout_w_in(w_in[l])
        wq2, wk, wv, e2 = _relayout_mla(mla_w_uq[l], mla_w_ukv[l])
        wai = jnp.stack([jnp.concatenate([_block_diag_heads(lru_w_a[l, d]), _block_diag_heads(lru_w_i[l, d])], axis=1)
                         for d in range(2)]).astype(BF16)
        bai = jnp.concatenate([lru_b_a[l], lru_b_i[l]], axis=1)[:, None, :]
        lru_args = (lru_conv_w[l], lru_conv_b[l][:, None, :], wai, bai, lru_lambda[l][:, None, :])
        w_out_b = w_out[l].astype(BF16)

        zf_c, lru_c, mla_c, u_c, sb_c = _in_projection(h_ctx, norm1_g[l], part(mc, 0), part(mc, 1), w_in_p, cs)
        zf_l, lru_l, mla_l, u_l, sb_l = _in_projection(h_lat, norm1_g[l], part(ml, 0), part(ml, 1), w_in_p, cs)

        h0 = jnp.zeros((B, 2, 1, GROUP_W), F32)
        b_ctx, s_ctx = _rglru(lru_c, h0, *lru_args)
        b_lat, _ = _rglru(lru_l, s_ctx, *lru_args)

        q_c, k_c, v_c = _mla_prep(mla_c, mla_q_norm_g[l], mla_kv_norm_g[l], wq2, wk, wv, e2, cos_ctx, sin_ctx)
        q_l, k_l, v_l = _mla_prep(mla_l, mla_q_norm_g[l], mla_kv_norm_g[l], wq2, wk, wv, e2, cos_lat, sin_lat)
        k_all = jnp.concatenate([k_c, k_l], axis=2)
        v_all = jnp.concatenate([v_c, v_l], axis=1)
        c_lat = _attention(q_l, k_all, v_all)

        a_lat = _fourier_positions(zf_l, cn_lat, sn_lat)
        d_lat = _short_conv(u_l, sb_l, sc_conv_w[l], sc_conv_b[l])

        hn_lat, f_lat, ti_lat, tg_lat = _out_projection(
            a_lat, b_lat, c_lat, d_lat, h_lat, w_out_b, part(ml, 2), norm2_g[l], part(ml, 3), part(ml, 4),
            router_w[l], router_b[l])

        moe_w = (exp_w_gu, exp_b_gu, exp_w_down, exp_b_down, l)
        tokens_on_lanes = lambda u: jnp.transpose(u, (1, 0, 2)).reshape(8, -1)
        g2_lat = part(ml, 5)
        if with_ctx_out:
            a_ctx = _fourier_positions(zf_c, cn_ctx, sn_ctx)
            c_ctx_out = _attention(q_c, k_c, v_c)
            d_ctx = _short_conv(u_c, sb_c, sc_conv_w[l], sc_conv_b[l])
            hn_ctx, f_ctx, ti_ctx, tg_ctx = _out_projection(
                a_ctx, b_ctx, c_ctx_out, d_ctx, h_ctx, w_out_b, part(mc, 2), norm2_g[l], part(mc, 3), part(mc, 4),
                router_w[l], router_b[l])
            S = n_ctx + n
            cat = lambda u, v: jnp.concatenate([u, v], axis=1)
            g2_rows = jnp.concatenate([jnp.broadcast_to(part(mc, 5), (B, n_ctx // tm, D)),
                                       jnp.broadcast_to(g2_lat, (B, n // tm, D))], axis=1)
            cat_t = lambda u, v: tokens_on_lanes(jnp.concatenate([u, v], axis=2))
            h_all = _moe(cat(f_ctx, f_lat).reshape(B * S * ROW_TILE, 128), cat(hn_ctx, hn_lat).reshape(B * S, D),
                         cat_t(ti_ctx, ti_lat), cat_t(tg_ctx, tg_lat),
                         g2_rows.reshape(B * S // tm, 1, D), *moe_w, final_g, tm, False)
            h_all = h_all.reshape(B, S, D)
            h_ctx, h_lat = h_all[:, 0:n_ctx], h_all[:, n_ctx:]
        else:
            g2_rows = jnp.broadcast_to(g2_lat, (B, n // tm, D)).reshape(B * n // tm, 1, D)
            h_lat = _moe(f_lat.reshape(B * n * ROW_TILE, 128), hn_lat.reshape(B * n, D),
                         tokens_on_lanes(ti_lat), tokens_on_lanes(tg_lat), g2_rows,
                         *moe_w, final_g, tm, True).reshape(B, n, D)

    return h_lat
```

```python
import functools
import math

import numpy as np
import jax
import jax.numpy as jnp
from jax import lax
from jax.experimental import pallas as pl
from jax.experimental.pallas import tpu as pltpu

F32 = jnp.float32
BF16 = jnp.bfloat16

D_MODEL = 1024
GRID_W = 64
GROUP_W = 256
EPS = 1e-6
FNET_CH = 64
LRU_HEADS = 4
LRU_C = 8.0
MLA_HEADS = 4
MLA_NOPE = 64
MLA_ROPE = 32
MLA_V = 64
MLA_Q_LORA = 256
MLA_KV_LORA = 128
ROPE_BASE = 10000.0
ATTN_SCALE = (MLA_NOPE + MLA_ROPE) ** -0.5
LOG2_E = math.log2(math.e)
N_EXPERTS = 32
TOP_K = 4
SWIGLU_ALPHA = 1.702
SWIGLU_LIMIT = 7.0

HEAD_PAD = 128
V_ROWS = 80
P_PAD = 2048
MOE_BM = 512
VMEM_LIMIT = 56 * 1024 * 1024


def _cparams(n_axes, vmem=None):
    return pltpu.CompilerParams(dimension_semantics=("arbitrary",) * n_axes,
                                vmem_limit_bytes=vmem or VMEM_LIMIT)


def _bdot(a, b):
    return jnp.dot(a.astype(BF16), b.astype(BF16), preferred_element_type=F32)


def _mod_kernel(c_ref, w_ref, b_ref, o_ref):
    c = c_ref[...]
    s = c * jax.nn.sigmoid(c)
    o_ref[0] = _bdot(s, w_ref[0]) + b_ref[0]


def _modulation(c_all, mod_w, mod_b):
    L, D, N = mod_w.shape
    R = c_all.shape[0]
    tn = 1024
    return pl.pallas_call(
        _mod_kernel,
        out_shape=jax.ShapeDtypeStruct((L, R, N), F32),
        grid=(L, N // tn),
        in_specs=[pl.BlockSpec((R, D), lambda l, j: (0, 0)),
                  pl.BlockSpec((1, D, tn), lambda l, j: (l, 0, j)),
                  pl.BlockSpec((1, 1, tn), lambda l, j: (l, 0, j))],
        out_specs=pl.BlockSpec((1, R, tn), lambda l, j: (l, 0, j)),
        compiler_params=_cparams(2),
        name="adaln_mod",
    )(c_all, mod_w, mod_b.reshape(L, 1, N))


def _inproj_kernel(h_ref, g_ref, sh_ref, sc_ref, w_ref, cs_ref, zf_ref, lru_ref, mla_ref, u_ref, sb_ref):
    x = h_ref[0]
    y = x * lax.rsqrt(jnp.mean(x * x, axis=-1, keepdims=True) + EPS) * g_ref[...]
    a = y * (1.0 + sc_ref[0]) + sh_ref[0]
    p = _bdot(a, w_ref[...])
    zf_ref[0] = _bdot(p[:, 0:256], cs_ref[...]).astype(BF16)
    lru_ref[0] = p[:, 256:768]
    mla_ref[0] = p[:, 768:1280]
    u_ref[0] = p[:, 1280:1536] * p[:, 1792:2048]
    sb_ref[0] = p[:, 1536:1792]


def _in_projection(h, g, sh, sc, w_in_p, cs):
    B, n, D = h.shape
    tm = min(512, n)
    tile = lambda w: pl.BlockSpec((1, tm, w), lambda b, i: (b, i, 0))
    vec = pl.BlockSpec((1, 1, D), lambda b, i: (b, 0, 0))
    return pl.pallas_call(
        _inproj_kernel,
        out_shape=(jax.ShapeDtypeStruct((B, n, 512), BF16),
                   jax.ShapeDtypeStruct((B, n, 512), F32),
                   jax.ShapeDtypeStruct((B, n, 512), F32),
                   jax.ShapeDtypeStruct((B, n, 256), F32),
                   jax.ShapeDtypeStruct((B, n, 256), F32)),
        grid=(B, n // tm),
        in_specs=[tile(D),
                  pl.BlockSpec((1, D), lambda b, i: (0, 0)),
                  vec, vec,
                  pl.BlockSpec((D, P_PAD), lambda b, i: (0, 0)),
                  pl.BlockSpec((256, 512), lambda b, i: (0, 0))],
        out_specs=(tile(512), tile(512), tile(512), tile(256), tile(256)),
        compiler_params=_cparams(2),
        name="norm_inproj",
    )(h, g.reshape(1, D), sh, sc, w_in_p, cs)


def _fnet_kernel(c_ref, s_ref, z_ref, o_ref):
    z = z_ref[0]
    acc = jnp.dot(c_ref[...], z[:, 0:256], preferred_element_type=F32)
    acc = acc - jnp.dot(s_ref[...], z[:, 256:512], preferred_element_type=F32)
    o_ref[0] = acc.astype(BF16)


def _fourier_positions(zf, cn, sn):
    B, n, _ = zf.shape
    tm = min(512, n)
    return pl.pallas_call(
        _fnet_kernel,
        out_shape=jax.ShapeDtypeStruct((B, n, 256), BF16),
        grid=(n // tm, B),
        in_specs=[pl.BlockSpec((tm, n), lambda i, b: (i, 0)),
                  pl.BlockSpec((tm, n), lambda i, b: (i, 0)),
                  pl.BlockSpec((1, n, 512), lambda i, b: (b, 0, 0))],
        out_specs=pl.BlockSpec((1, tm, 256), lambda i, b: (b, i, 0)),
        compiler_params=_cparams(2),
        name="fourier_positions",
    )(cn, sn, zf)


def _shift_rows(xcat, off, tc):
    if off == 0:
        return xcat[8:8 + tc]
    rolled = pltpu.roll(xcat, (tc + 16) - (8 + off), axis=0)
    return rolled[0:tc]


def _load_halo(ref, col0, width, start, tc, n):
    main = ref[0, pl.ds(start, tc), col0:col0 + width]
    ps = jnp.maximum(start - 8, 0)
    prev = ref[0, pl.ds(pl.multiple_of(ps, 8), 8), col0:col0 + width]
    prev = jnp.where(start > 0, prev, 0.0)
    ns = jnp.minimum(start + tc, n - 8)
    nxt = ref[0, pl.ds(pl.multiple_of(ns, 8), 8), col0:col0 + width]
    nxt = jnp.where(start + tc < n, nxt, 0.0)
    return jnp.concatenate([prev, main, nxt], axis=0)


def _lru_kernel(x_ref, h0_ref, cw_ref, cb_ref, wai_ref, bai_ref, lam_ref, o_ref, hf_ref, hs_ref, *, tc, n):
    nc = n // tc
    row = lax.broadcasted_iota(jnp.int32, (tc, GROUP_W), 0)

    def direction(d, reverse):
        left = 1 if reverse else 2
        cw = cw_ref[d]
        cb = cb_ref[d]
        wai = wai_ref[d]
        bai = bai_ref[d]
        sp = jax.nn.softplus(-lam_ref[d])

        def chunk(ci, carry):
            c = (nc - 1 - ci) if reverse else ci
            start = pl.multiple_of(c * tc, tc)
            xcat = _load_halo(x_ref, 0, GROUP_W, start, tc, n)
            xc = cb
            for k in range(4):
                xc = xc + cw[k:k + 1, :] * _shift_rows(xcat, k - left, tc)
            gates = jax.nn.sigmoid(_bdot(xc, wai) + bai)
            r = gates[:, 0:GROUP_W]
            gi = gates[:, GROUP_W:2 * GROUP_W]
            a = jnp.exp(-LRU_C * r * sp)
            b = jnp.sqrt(jnp.maximum(1.0 - a * a, 0.0)) * (gi * xc)
            s = 1
            while s < tc:
                if reverse:
                    a_sh = pltpu.roll(a, tc - s, axis=0)
                    b_sh = pltpu.roll(b, tc - s, axis=0)
                    keep = row < (tc - s)
                else:
                    a_sh = pltpu.roll(a, s, axis=0)
                    b_sh = pltpu.roll(b, s, axis=0)
                    keep = row >= s
                b = jnp.where(keep, a * b_sh, 0.0) + b
                a = jnp.where(keep, a * a_sh, a)
                s *= 2
            h = a * carry + b
            if reverse:
                gate = x_ref[0, pl.ds(start, tc), GROUP_W:2 * GROUP_W]
                tot = hs_ref[pl.ds(start, tc), :] + h
                o_ref[0, pl.ds(start, tc), :] = (jax.nn.gelu(gate) * tot).astype(o_ref.dtype)
                return h[0:1, :]
            hs_ref[pl.ds(start, tc), :] = h
            return h[tc - 1:tc, :]

        return lax.fori_loop(0, nc, chunk, h0_ref[0, d])

    hf_ref[0, 0] = direction(0, False)
    hf_ref[0, 1] = direction(1, True)


def _rglru(lru, h0, cw, cb, wai, bai, lam):
    B, n, _ = lru.shape
    tc = min(256, n)
    full = lambda *s: pl.BlockSpec(s, lambda b: (0,) * len(s))
    kern = functools.partial(_lru_kernel, tc=tc, n=n)
    return pl.pallas_call(
        kern,
        out_shape=(jax.ShapeDtypeStruct((B, n, GROUP_W), BF16),
                   jax.ShapeDtypeStruct((B, 2, 1, GROUP_W), F32)),
        grid=(B,),
        in_specs=[pl.BlockSpec((1, n, 512), lambda b: (b, 0, 0)),
                  pl.BlockSpec((1, 2, 1, GROUP_W), lambda b: (b, 0, 0, 0)),
                  full(2, 4, GROUP_W), full(2, 1, GROUP_W), full(2, GROUP_W, 2 * GROUP_W),
                  full(2, 1, 2 * GROUP_W), full(2, 1, GROUP_W)],
        out_specs=(pl.BlockSpec((1, n, GROUP_W), lambda b: (b, 0, 0)),
                   pl.BlockSpec((1, 2, 1, GROUP_W), lambda b: (b, 0, 0, 0))),
        scratch_shapes=[pltpu.VMEM((n, GROUP_W), F32)],
        compiler_params=_cparams(1),
        name="rglru",
    )(lru, h0, cw, cb, wai, bai, lam)


def _sconv_kernel(u_ref, sb_ref, w_ref, b_ref, o_ref, *, tc, n):
    w = w_ref[...]
    bias = b_ref[...]

    def chunk(c, _):
        start = pl.multiple_of(c * tc, tc)
        ucat = _load_halo(u_ref, 0, GROUP_W, start, tc, n)
        y = bias
        for k in range(3):
            y = y + w[k:k + 1, :] * _shift_rows(ucat, k - 1, tc)
        o_ref[0, pl.ds(start, tc), :] = (sb_ref[0, pl.ds(start, tc), :] * y).astype(o_ref.dtype)
        return 0

    lax.fori_loop(0, n // tc, chunk, 0)


def _short_conv(u, sb, w, b):
    B, n, W = u.shape
    tc = min(256, n)
    seq = pl.BlockSpec((1, n, W), lambda i: (i, 0, 0))
    return pl.pallas_call(
        functools.partial(_sconv_kernel, tc=tc, n=n),
        out_shape=jax.ShapeDtypeStruct((B, n, W), BF16),
        grid=(B,),
        in_specs=[seq, seq, pl.BlockSpec((3, W), lambda i: (0, 0)), pl.BlockSpec((1, W), lambda i: (0, 0))],
        out_specs=seq,
        compiler_params=_cparams(1),
        name="short_conv",
    )(u, sb, w, b.reshape(1, W))


def _mla_prep_kernel(x_ref, qg_ref, kg_ref, wq_ref, wk_ref, wv_ref, e2_ref, cos_ref, sin_ref,
                     q_ref, k_ref, v_ref):
    x = x_ref[0]
    cq = x[:, 0:256]
    ckv = x[:, 256:384]
    kr = x[:, 384:448]
    cqn = cq * lax.rsqrt(jnp.mean(cq * cq, axis=-1, keepdims=True) + EPS) * qg_ref[...]
    ckvn = ckv * lax.rsqrt(jnp.mean(ckv * ckv, axis=-1, keepdims=True) + EPS) * kg_ref[...]
    cos = cos_ref[...]
    sin = sin_ref[...]
    q2 = _bdot(cqn, wq_ref[...])
    q = (q2[:, 0:512] * cos + q2[:, 512:1024] * sin) * (ATTN_SCALE * LOG2_E)
    kr2 = _bdot(kr, e2_ref[...])
    ckvb = ckvn.astype(BF16)
    k = _bdot(ckvb, wk_ref[...]) + kr2[:, 0:512] * cos + kr2[:, 512:1024] * sin
    vt = lax.dot_general(wv_ref[...], ckvb, (((1,), (1,)), ((), ())), preferred_element_type=F32)
    tm = x.shape[0]
    ones_rows = jnp.where(lax.broadcasted_iota(jnp.int32, (V_ROWS - MLA_V, tm), 0) == 0, 1.0, 0.0)
    pieces = []
    for h in range(MLA_HEADS):
        q_ref[0, h] = q[:, h * HEAD_PAD:(h + 1) * HEAD_PAD].astype(BF16)
        k_ref[0, h] = k[:, h * HEAD_PAD:(h + 1) * HEAD_PAD].astype(BF16)
        pieces += [vt[h * MLA_V:(h + 1) * MLA_V], ones_rows]
    v_ref[0, 0] = jnp.concatenate(pieces, axis=0).astype(BF16)


def _mla_prep(mla, qg, kg, wq2, wk, wv, e2, cos, sin):
    B, n, _ = mla.shape
    tm = min(256, n)
    HW = MLA_HEADS * HEAD_PAD
    full = lambda *s: pl.BlockSpec(s, lambda b, i: (0,) * len(s))
    return pl.pallas_call(
        _mla_prep_kernel,
        out_shape=(jax.ShapeDtypeStruct((B, MLA_HEADS, n, HEAD_PAD), BF16),
                   jax.ShapeDtypeStruct((B, MLA_HEADS, n, HEAD_PAD), BF16),
                   jax.ShapeDtypeStruct((B, n // tm, MLA_HEADS * V_ROWS, tm), BF16)),
        grid=(B, n // tm),
        in_specs=[pl.BlockSpec((1, tm, 512), lambda b, i: (b, i, 0)),
                  full(1, MLA_Q_LORA), full(1, MLA_KV_LORA),
                  full(MLA_Q_LORA, 2 * HW), full(MLA_KV_LORA, HW), full(MLA_HEADS * MLA_V, MLA_KV_LORA),
                  full(2 * MLA_ROPE, 2 * HW),
                  pl.BlockSpec((tm, HW), lambda b, i: (i, 0)),
                  pl.BlockSpec((tm, HW), lambda b, i: (i, 0))],
        out_specs=(pl.BlockSpec((1, MLA_HEADS, tm, HEAD_PAD), lambda b, i: (b, 0, i, 0)),
                   pl.BlockSpec((1, MLA_HEADS, tm, HEAD_PAD), lambda b, i: (b, 0, i, 0)),
                   pl.BlockSpec((1, 1, MLA_HEADS * V_ROWS, tm), lambda b, i: (b, i, 0, 0))),
        compiler_params=_cparams(2),
        name="mla_prep",
    )(mla, qg.reshape(1, -1), kg.reshape(1, -1), wq2, wk, wv, e2, cos, sin)


def _attn_kernel(q_ref, k_ref, vt_ref, o_ref, st_a, st_b, p_a, p_b, *, tk, n_kt):
    tq = q_ref.shape[2]
    H = MLA_HEADS
    qs = [q_ref[0, h] for h in range(H)]

    def scores(j, h):
        kt = k_ref[0, h, pl.ds(pl.multiple_of(j * tk, tk), tk), :]
        return lax.dot_general(kt, qs[h], (((1,), (1,)), ((), ())), preferred_element_type=F32)

    def weighted_values(j, h, p):
        vt = vt_ref[0, j, h * V_ROWS:(h + 1) * V_ROWS, :]
        return jnp.dot(vt, p, preferred_element_type=F32)

    def step(j, st_cur, st_next, p_prev, p_cur, carry):
        a_prev, m, acc = carry
        jn = jnp.minimum(j + 1, n_kt - 1)
        for h in range(H):
            st_next[h] = scores(jn, h)
        pv = [weighted_values(jnp.maximum(j - 1, 0), h, p_prev[h]) for h in range(H)]
        m_new, alpha = [], []
        for h in range(H):
            st = st_cur[h]
            mh = jnp.maximum(m[h], jnp.max(st, axis=0, keepdims=True))
            alpha.append(jnp.exp2(m[h] - mh))
            m_new.append(mh)
            p_cur[h] = jnp.exp2((st - mh).astype(BF16))
        acc_new = [a_prev[h] * acc[h] + pv[h] for h in range(H)]
        return (tuple(alpha), tuple(m_new), tuple(acc_new))

    even = lambda j, c: step(j, st_a, st_b, p_b, p_a, c)
    odd = lambda j, c: step(j, st_b, st_a, p_a, p_b, c)

    for h in range(H):
        st_a[h] = scores(0, h)
    p_b[...] = jnp.zeros(p_b.shape, BF16)
    per_head = lambda f: tuple(f() for _ in range(H))
    carry = (per_head(lambda: jnp.ones((1, tq), F32)),
             per_head(lambda: jnp.full((1, tq), -jnp.inf, F32)),
             per_head(lambda: jnp.zeros((V_ROWS, tq), F32)))
    carry = lax.fori_loop(0, n_kt // 2, lambda i, c: odd(2 * i + 1, even(2 * i, c)), carry)
    if n_kt % 2:
        carry = even(n_kt - 1, carry)
    p_last = p_a if n_kt % 2 else p_b
    a_last, _, acc = carry
    outs = []
    for h in range(H):
        tot = a_last[h] * acc[h] + weighted_values(n_kt - 1, h, p_last[h])
        outs.append(tot[0:MLA_V] / tot[MLA_V:MLA_V + 1])
    o_ref[0] = jnp.concatenate(outs, axis=0).T.astype(o_ref.dtype)


def _attention(q, k, vt):
    B, H, n, _ = q.shape
    M = k.shape[2]
    n_kt, tk = vt.shape[1], vt.shape[3]
    tq = min(256, n)
    return pl.pallas_call(
        functools.partial(_attn_kernel, tk=tk, n_kt=n_kt),
        out_shape=jax.ShapeDtypeStruct((B, n, H * MLA_V), BF16),
        grid=(B, n // tq),
        in_specs=[pl.BlockSpec((1, H, tq, HEAD_PAD), lambda b, i: (b, 0, i, 0)),
                  pl.BlockSpec((1, H, M, HEAD_PAD), lambda b, i: (b, 0, 0, 0)),
                  pl.BlockSpec((1, n_kt, H * V_ROWS, tk), lambda b, i: (b, 0, 0, 0))],
        out_specs=pl.BlockSpec((1, tq, H * MLA_V), lambda b, i: (b, i, 0)),
        scratch_shapes=[pltpu.VMEM((H, tk, tq), F32), pltpu.VMEM((H, tk, tq), F32),
                        pltpu.VMEM((H, tk, tq), BF16), pltpu.VMEM((H, tk, tq), BF16)],
        compiler_params=_cparams(2),
        name="mla_attention",
    )(q, k, vt)


ROW_TILE = 8


def _store_rows_contiguous(ref, lead, x):
    rows = x.shape[0]
    for j in range(ROW_TILE):
        ref[lead + (pl.ds(j, rows, stride=ROW_TILE), slice(None))] = x[:, j * 128:(j + 1) * 128]


def _load_rows_contiguous(ref, lead, rows, dtype):
    return jnp.concatenate([ref[lead + (pl.ds(j, rows, stride=ROW_TILE), slice(None))].astype(dtype)
                            for j in range(ROW_TILE)], axis=-1)


def _outproj_kernel(a_ref, b_ref, c_ref, d_ref, h_ref, w_ref, g1_ref, ng_ref, sh_ref, sc_ref, rw_ref, rb_ref,
                    hn_ref, f_ref, ti_ref, tg_ref):
    mixed = jnp.concatenate([a_ref[0], b_ref[0], c_ref[0], d_ref[0]], axis=-1)
    hn = h_ref[0] + g1_ref[0] * jnp.dot(mixed, w_ref[...], preferred_element_type=F32)
    hn_ref[0] = hn
    y = hn * lax.rsqrt(jnp.mean(hn * hn, axis=-1, keepdims=True) + EPS) * ng_ref[...]
    f = y * (1.0 + sc_ref[0]) + sh_ref[0]
    _store_rows_contiguous(f_ref, (0,), f)
    nt = (((1,), (1,)), ((), ()))
    rw = rw_ref[...]
    rw_hi = rw.astype(BF16)
    rw_lo = (rw - rw_hi.astype(F32)).astype(BF16)
    f_hi = f.astype(BF16)
    f_lo = (f - f_hi.astype(F32)).astype(BF16)
    logits = (lax.dot_general(rw_hi, f_hi, nt, preferred_element_type=F32)
              + (lax.dot_general(rw_hi, f_lo, nt, preferred_element_type=F32)
                 + lax.dot_general(rw_lo, f_hi, nt, preferred_element_type=F32))) + rb_ref[...]
    tm = logits.shape[1]
    sub = lax.broadcasted_iota(jnp.int32, (N_EXPERTS, tm), 0)
    row = lax.broadcasted_iota(jnp.int32, (8, tm), 0)
    top_v = jnp.full((8, tm), -jnp.inf, F32)
    top_i = jnp.zeros((8, tm), jnp.int32)
    cur = logits
    m0 = None
    for kk in range(TOP_K):
        m = jnp.max(cur, axis=0, keepdims=True)
        sel = jnp.min(jnp.where(cur == m, sub, N_EXPERTS), axis=0, keepdims=True)
        if kk == 0:
            m0 = m
        top_v = jnp.where(row == kk, m, top_v)
        top_i = jnp.where(row == kk, sel, top_i)
        cur = jnp.where(sub == sel, -jnp.inf, cur)
    e = jnp.where(row < TOP_K, jnp.exp(top_v - m0), 0.0)
    ti_ref[0] = top_i
    tg_ref[0] = e / jnp.sum(e, axis=0, keepdims=True)


def _out_projection(a, b, c, d, h, w_out, g1, ng, sh, sc, rw, rb):
    B, n, D = h.shape
    tm = min(512, n)
    tile = lambda w: pl.BlockSpec((1, tm, w), lambda bb, i: (bb, i, 0))
    vec = pl.BlockSpec((1, 1, D), lambda bb, i: (bb, 0, 0))
    full = lambda *s: pl.BlockSpec(s, lambda bb, i: (0,) * len(s))
    return pl.pallas_call(
        _outproj_kernel,
        out_shape=(jax.ShapeDtypeStruct((B, n, D), F32),
                   jax.ShapeDtypeStruct((B, n * ROW_TILE, 128), F32),
                   jax.ShapeDtypeStruct((B, 8, n), jnp.int32),
                   jax.ShapeDtypeStruct((B, 8, n), F32)),
        grid=(B, n // tm),
        in_specs=[tile(256), tile(256), tile(256), tile(256), tile(D),
                  full(D, D), vec, full(1, D), vec, vec, full(N_EXPERTS, D), full(N_EXPERTS, 1)],
        out_specs=(tile(D), pl.BlockSpec((1, tm * ROW_TILE, 128), lambda bb, i: (bb, i, 0)),
                   pl.BlockSpec((1, 8, tm), lambda bb, i: (bb, 0, i)),
                   pl.BlockSpec((1, 8, tm), lambda bb, i: (bb, 0, i))),
        compiler_params=_cparams(2),
        name="outproj_router",
    )(a, b, c, d, h, w_out, g1, ng.reshape(1, D), sh, sc, rw.T, rb.reshape(N_EXPERTS, 1))


def _rank_kernel(ti_ref, rank_ref, cnt_ref, carry_ref, *, tm):
    @pl.when(pl.program_id(0) == 0)
    def _():
        carry_ref[...] = jnp.zeros_like(carry_ref)

    ti = ti_ref[...]
    sub = lax.broadcasted_iota(jnp.int32, (N_EXPERTS, tm), 0)
    hot = jnp.zeros((N_EXPERTS, tm), F32)
    for kk in range(TOP_K):
        hot = hot + jnp.where(sub == ti[kk:kk + 1, :], 1.0, 0.0)
    r = lax.broadcasted_iota(jnp.int32, (tm, tm), 0)
    c = lax.broadcasted_iota(jnp.int32, (tm, tm), 1)
    earlier = jnp.where(r < c, 1.0, 0.0).astype(BF16)
    carry = carry_ref[:, 0:1]
    before = jnp.dot(hot.astype(BF16), earlier, preferred_element_type=F32) + carry
    row = lax.broadcasted_iota(jnp.int32, (8, tm), 0)
    out = jnp.zeros((8, tm), jnp.int32)
    for kk in range(TOP_K):
        rk = jnp.sum(jnp.where(sub == ti[kk:kk + 1, :], before, 0.0), axis=0, keepdims=True)
        out = jnp.where(row == kk, rk.astype(jnp.int32), out)
    rank_ref[...] = out
    total = carry + jnp.sum(hot, axis=1, keepdims=True)
    carry_ref[...] = jnp.broadcast_to(total, carry_ref.shape)
    cnt_ref[...] = jnp.broadcast_to(total, cnt_ref.shape).astype(jnp.int32)


def _expert_ranks(top_i_t):
    T = top_i_t.shape[1]
    tm = 512 if T % 512 == 0 else 256
    return pl.pallas_call(
        functools.partial(_rank_kernel, tm=tm),
        out_shape=(jax.ShapeDtypeStruct((8, T), jnp.int32), jax.ShapeDtypeStruct((N_EXPERTS, 128), jnp.int32)),
        grid=(T // tm,),
        in_specs=[pl.BlockSpec((8, tm), lambda i: (0, i))],
        out_specs=(pl.BlockSpec((8, tm), lambda i: (0, i)), pl.BlockSpec((N_EXPERTS, 128), lambda i: (0, 0))),
        scratch_shapes=[pltpu.VMEM((N_EXPERTS, 128), F32)],
        compiler_params=_cparams(1),
        name="moe_rank",
    )(top_i_t)


def _dispatch_kernel(cnt_ref, ps_ref, pe_ref, pos_ref, f_ref, xs_ref, buf_ref, zrow_ref, sem_ref, zsem_ref,
                     *, tm, n_slots):
    i = pl.program_id(0)
    nt = pl.num_programs(0)
    copies = tm * TOP_K
    zslots = zrow_ref.shape[0] // ROW_TILE

    def tile_rows(first_row, n_rows):
        return pl.ds(pl.multiple_of(first_row * ROW_TILE, ROW_TILE), n_rows * ROW_TILE)

    def zero_fill(lo, hi, start):
        def chunk(off, size):
            cp = pltpu.make_async_copy(zrow_ref.at[tile_rows(0, size)], xs_ref.at[tile_rows(off, size)], zsem_ref)
            cp.start() if start else cp.wait()

        n = hi - lo

        def whole(j, _):
            chunk(lo + j * zslots, zslots)
            return 0

        lax.fori_loop(0, n // zslots, whole, 0)
        rem_lo = lo + (n // zslots) * zslots
        rem = n % zslots
        size = zslots // 2
        while size >= 1:
            @pl.when((rem & size) != 0)
            def _(size=size):
                chunk(rem_lo + (rem & ~(2 * size - 1)), size)
            size //= 2

    @pl.when(i == 0)
    def _():
        zrow_ref[...] = jnp.zeros_like(zrow_ref)
        for start in (True, False):
            for e in range(N_EXPERTS):
                zero_fill(ps_ref[e] + cnt_ref[e], pe_ref[e], start)
            zero_fill(pe_ref[N_EXPERTS - 1], n_slots, start)

    slot = i % 2
    buf_ref[slot] = f_ref[...]

    def issue(t, _):
        for kk in range(TOP_K):
            pltpu.make_async_copy(buf_ref.at[slot, tile_rows(t, 1)],
                                  xs_ref.at[tile_rows(pos_ref[0, 0, t * TOP_K + kk], 1)],
                                  sem_ref.at[slot]).start(priority=kk % 2)
        return 0

    lax.fori_loop(0, tm, issue, 0, unroll=2)

    def wait_rows(s):
        pltpu.make_async_copy(xs_ref.at[tile_rows(0, copies)], xs_ref.at[tile_rows(0, copies)], sem_ref.at[s]).wait()

    @pl.when(i > 0)
    def _():
        wait_rows(1 - slot)

    @pl.when(i == nt - 1)
    def _():
        wait_rows(slot)


def _dispatch(f, pos, counts, pad_start, pad_end, n_slots, tm):
    T = f.shape[0] // ROW_TILE
    nt = T // tm
    grid_spec = pltpu.PrefetchScalarGridSpec(
        num_scalar_prefetch=3,
        grid=(nt,),
        in_specs=[pl.BlockSpec((1, 1, tm * TOP_K), lambda i, *_: (i, 0, 0), memory_space=pltpu.SMEM),
                  pl.BlockSpec((tm * ROW_TILE, 128), lambda i, *_: (i, 0))],
        out_specs=pl.BlockSpec(memory_space=pl.ANY),
        scratch_shapes=[pltpu.VMEM((2, tm * ROW_TILE, 128), F32), pltpu.VMEM((256 * ROW_TILE, 128), F32),
                        pltpu.SemaphoreType.DMA((2,)), pltpu.SemaphoreType.DMA],
    )
    return pl.pallas_call(
        functools.partial(_dispatch_kernel, tm=tm, n_slots=n_slots),
        out_shape=jax.ShapeDtypeStruct((n_slots * ROW_TILE, 128), F32),
        grid_spec=grid_spec,
        compiler_params=_cparams(1),
        name="moe_dispatch",
    )(counts, pad_start, pad_end, pos.reshape(nt, 1, tm * TOP_K), f)


def _ffn_kernel(be_ref, bv_ref, x_ref, wgu_ref, bgu_ref, wd_ref, bd_ref, y_ref, wgu_s, wd_s, *, bm):
    i = pl.program_id(0)
    valid = bv_ref[i]
    half = bm // 2

    def experts_rows(rows):
        De = wd_ref.shape[2]
        x = _load_rows_contiguous(x_ref, (), rows, BF16)
        gu = jnp.dot(x, wgu_s[...], preferred_element_type=F32) + bgu_ref[0, 0]
        g = jnp.minimum(gu[:, 0:De], SWIGLU_LIMIT)
        u = jnp.clip(gu[:, De:2 * De], -SWIGLU_LIMIT, SWIGLU_LIMIT)
        act = (u + 1.0) * (g * jax.nn.sigmoid(SWIGLU_ALPHA * g))
        y = jnp.dot(act.astype(BF16), wd_s[...], preferred_element_type=F32) + bd_ref[0, 0]
        _store_rows_contiguous(y_ref, (), y)

    @pl.when(valid > 0)
    def _():
        @pl.when((i == 0) | (be_ref[i] != be_ref[jnp.maximum(i - 1, 0)]))
        def _():
            wgu_s[...] = wgu_ref[0, 0].astype(BF16)
            wd_s[...] = wd_ref[0, 0].astype(BF16)

        @pl.when(valid > half)
        def _():
            experts_rows(bm)

        @pl.when(valid <= half)
        def _():
            experts_rows(half)
            y_ref[half * ROW_TILE:bm * ROW_TILE, :] = jnp.zeros(((bm - half) * ROW_TILE, 128), y_ref.dtype)

    @pl.when(valid <= 0)
    def _():
        y_ref[...] = jnp.zeros_like(y_ref)


def _expert_ffn(xs, block_exp, block_valid, wgu, bgu, wd, bd, layer, bm):
    n_slots = xs.shape[0] // ROW_TILE
    L, E, D, G = wgu.shape
    nb = n_slots // bm
    wmap = lambda i, be, bv: (layer, be[i], 0, 0)
    rows_spec = pl.BlockSpec((bm * ROW_TILE, 128), lambda i, be, bv: (i, 0))
    grid_spec = pltpu.PrefetchScalarGridSpec(
        num_scalar_prefetch=2,
        grid=(nb,),
        in_specs=[rows_spec,
                  pl.BlockSpec((1, 1, D, G), wmap),
                  pl.BlockSpec((1, 1, 1, G), wmap),
                  pl.BlockSpec((1, 1, G // 2, D), wmap),
                  pl.BlockSpec((1, 1, 1, D), wmap)],
        out_specs=rows_spec,
        scratch_shapes=[pltpu.VMEM((D, G), BF16), pltpu.VMEM((G // 2, D), BF16)],
    )
    return pl.pallas_call(
        functools.partial(_ffn_kernel, bm=bm),
        out_shape=jax.ShapeDtypeStruct((n_slots * ROW_TILE, 128), F32),
        grid_spec=grid_spec,
        compiler_params=_cparams(1),
        name="moe_expert_ffn",
    )(block_exp, block_valid, xs, wgu, bgu.reshape(L, E, 1, G), wd, bd.reshape(L, E, 1, D))


def _combine_kernel(pos_ref, y_ref, h_ref, g2_ref, gate_ref, fg_ref, o_ref, buf_ref, sem_ref, *, tm, final_norm):
    def tile_rows(first_row):
        return pl.ds(pl.multiple_of(first_row * ROW_TILE, ROW_TILE), ROW_TILE)

    def issue(r, _):
        for kk in range(TOP_K):
            slot = pos_ref[0, 0, r * TOP_K + kk]
            pltpu.make_async_copy(y_ref.at[tile_rows(slot)], buf_ref.at[kk, tile_rows(r)],
                                  sem_ref).start(priority=kk % 2)
        return 0

    lax.fori_loop(0, tm, issue, 0, unroll=2)
    pltpu.make_async_copy(buf_ref, buf_ref, sem_ref).wait()
    gate = gate_ref[...]
    acc = gate[:, 0:1] * _load_rows_contiguous(buf_ref, (0,), tm, F32)
    for kk in range(1, TOP_K):
        acc = acc + gate[:, kk:kk + 1] * _load_rows_contiguous(buf_ref, (kk,), tm, F32)
    out = h_ref[...] + g2_ref[0] * acc
    if final_norm:
        out = out * lax.rsqrt(jnp.mean(out * out, axis=-1, keepdims=True) + EPS) * fg_ref[...]
    o_ref[...] = out


def _combine(y, pos, h, g2_tiles, gates, final_g, tm, final_norm):
    T, D = h.shape
    nt = T // tm
    return pl.pallas_call(
        functools.partial(_combine_kernel, tm=tm, final_norm=final_norm),
        out_shape=jax.ShapeDtypeStruct((T, D), F32),
        grid=(nt,),
        in_specs=[pl.BlockSpec((1, 1, tm * TOP_K), lambda i: (i, 0, 0), memory_space=pltpu.SMEM),
                  pl.BlockSpec(memory_space=pl.ANY),
                  pl.BlockSpec((tm, D), lambda i: (i, 0)),
                  pl.BlockSpec((1, 1, D), lambda i: (i, 0, 0)),
                  pl.BlockSpec((tm, 8), lambda i: (i, 0)),
                  pl.BlockSpec((1, D), lambda i: (0, 0))],
        out_specs=pl.BlockSpec((tm, D), lambda i: (i, 0)),
        scratch_shapes=[pltpu.VMEM((TOP_K, tm * ROW_TILE, 128), F32), pltpu.SemaphoreType.DMA],
        compiler_params=_cparams(1),
        name="moe_combine",
    )(pos.reshape(nt, 1, tm * TOP_K), y, h, g2_tiles, gates, final_g.reshape(1, D))


def _moe(f, h, top_i_t, top_g_t, g2_tiles, wgu, bgu, wd, bd, layer, final_g, tm, final_norm):
    T, D = h.shape
    bm = MOE_BM
    n_asg = T * TOP_K
    rank_t, cnt = _expert_ranks(top_i_t)
    counts = cnt[:, 0]
    padded = (counts + bm - 1) // bm * bm
    pad_end = jnp.cumsum(padded).astype(jnp.int32)
    pad_start = pad_end - padded
    ti = top_i_t[0:TOP_K]
    expert_ids = jnp.arange(N_EXPERTS, dtype=jnp.int32)
    start_of = jnp.sum(jnp.where(ti[:, :, None] == expert_ids[None, None, :], pad_start[None, None, :], 0), axis=-1)
    pos = (start_of + rank_t[0:TOP_K]).astype(jnp.int32).T.reshape(-1)
    nb = (n_asg + N_EXPERTS * (bm - 1) + bm - 1) // bm
    n_slots = nb * bm
    starts = jnp.arange(nb, dtype=jnp.int32) * bm
    block_exp = jnp.minimum(jnp.sum((pad_end[None, :] <= starts[:, None]).astype(jnp.int32), axis=1), N_EXPERTS - 1)
    block_valid = jnp.clip(pad_start[block_exp] + counts[block_exp] - starts, 0, bm)
    block_valid = jnp.where(starts < pad_end[-1], block_valid, 0).astype(jnp.int32)

    xs = _dispatch(f, pos, counts, pad_start, pad_end, n_slots, tm)
    y = _expert_ffn(xs, block_exp, block_valid, wgu, bgu, wd, bd, layer, bm)
    return _combine(y, pos, h, g2_tiles, top_g_t.T, final_g, tm, final_norm)


_ROPE_SWAP = np.concatenate([np.arange(8, 16), np.arange(0, 8), np.arange(24, 32), np.arange(16, 24)])


def _dft_tables(n):
    hi = n // 64
    j = jnp.arange(n, dtype=jnp.int32)[:, None]
    k = jnp.arange(64, dtype=jnp.int32)[None, :]
    k2 = jnp.arange(hi, dtype=jnp.int32)[None, :]
    alpha = ((j * k2) % hi).astype(F32) * (2.0 * math.pi / hi)
    beta = ((j * k) % n).astype(F32) * (2.0 * math.pi / n)
    ca, sa = jnp.cos(alpha)[:, :, None], jnp.sin(alpha)[:, :, None]
    cb, sb = jnp.cos(beta)[:, None, :], jnp.sin(beta)[:, None, :]
    scale = 1.0 / math.sqrt(n)
    cn = ((ca * cb - sa * sb) * scale).reshape(n, n).astype(BF16)
    sn = ((sa * cb + ca * sb) * scale).reshape(n, n).astype(BF16)
    return cn, sn


def _channel_dft():
    j = jnp.arange(GROUP_W, dtype=jnp.int32)
    same_group = (j[:, None] // FNET_CH) == (j[None, :] // FNET_CH)
    ang = ((j[:, None] * j[None, :]) % FNET_CH).astype(F32) * (2.0 * math.pi / FNET_CH)
    scale = 1.0 / math.sqrt(FNET_CH)
    c = jnp.where(same_group, jnp.cos(ang) * scale, 0.0)
    s = jnp.where(same_group, jnp.sin(ang) * scale, 0.0)
    return jnp.concatenate([c, s], axis=1).astype(BF16)


def _rope_tables(n, rotary):
    HW = MLA_HEADS * HEAD_PAD
    cos_blk = np.zeros((HEAD_PAD,), np.float32)
    cos_blk[0:MLA_NOPE + MLA_ROPE] = 1.0
    if not rotary:
        return (jnp.broadcast_to(jnp.asarray(np.tile(cos_blk, MLA_HEADS)), (n, HW)),
                jnp.zeros((n, HW), F32))
    half = MLA_ROPE // 2
    inv = ROPE_BASE ** (-jnp.arange(0, half, 2, dtype=F32) / half)
    t = jnp.arange(n, dtype=jnp.int32)
    row = (t // GRID_W).astype(F32)
    col = (t % GRID_W).astype(F32)
    ar = row[:, None] * inv[None, :]
    ac = col[:, None] * inv[None, :]
    cos32 = jnp.concatenate([jnp.cos(ar), jnp.cos(ar), jnp.cos(ac), jnp.cos(ac)], axis=1)
    sin32 = jnp.concatenate([-jnp.sin(ar), jnp.sin(ar), -jnp.sin(ac), jnp.sin(ac)], axis=1)
    ones = jnp.ones((n, MLA_NOPE), F32)
    zeros_n = jnp.zeros((n, MLA_NOPE), F32)
    zeros_p = jnp.zeros((n, HEAD_PAD - MLA_NOPE - MLA_ROPE), F32)
    cos_h = jnp.concatenate([ones, cos32, zeros_p], axis=1)
    sin_h = jnp.concatenate([zeros_n, sin32, zeros_p], axis=1)
    return jnp.tile(cos_h, (1, MLA_HEADS)), jnp.tile(sin_h, (1, MLA_HEADS))


def _relayout_w_in(w_in):
    D = w_in.shape[0]
    krope = w_in[:, 1152:1184]
    return jnp.concatenate([w_in[:, 0:1184], krope[:, _ROPE_SWAP], jnp.zeros((D, 64), w_in.dtype),
                            w_in[:, 1184:1952]], axis=1).astype(BF16)


def _relayout_mla(w_uq, w_ukv):
    dq = MLA_NOPE + MLA_ROPE
    q_main, q_swap, k_cols, v_cols = [], [], [], []
    zq = jnp.zeros((MLA_Q_LORA, HEAD_PAD - dq), w_uq.dtype)
    zn = jnp.zeros((MLA_Q_LORA, MLA_NOPE), w_uq.dtype)
    zk = jnp.zeros((MLA_KV_LORA, HEAD_PAD - MLA_NOPE), w_ukv.dtype)
    for h in range(MLA_HEADS):
        wq_h = w_uq[:, h * dq:(h + 1) * dq]
        rope = wq_h[:, MLA_NOPE:]
        q_main += [wq_h, zq]
        q_swap += [zn, rope[:, _ROPE_SWAP], zq]
        kv_h = w_ukv[:, h * (MLA_NOPE + MLA_V):(h + 1) * (MLA_NOPE + MLA_V)]
        k_cols += [kv_h[:, 0:MLA_NOPE], zk]
        v_cols += [kv_h[:, MLA_NOPE:]]
    wq2 = jnp.concatenate(q_main + q_swap, axis=1).astype(BF16)
    wk = jnp.concatenate(k_cols, axis=1).astype(BF16)
    wv = jnp.concatenate(v_cols, axis=1).T.astype(BF16)
    HW = MLA_HEADS * HEAD_PAD
    e2 = np.zeros((2 * MLA_ROPE, 2 * HW), np.float32)
    for h in range(MLA_HEADS):
        for jj in range(MLA_ROPE):
            e2[jj, h * HEAD_PAD + MLA_NOPE + jj] = 1.0
            e2[MLA_ROPE + jj, HW + h * HEAD_PAD + MLA_NOPE + jj] = 1.0
    return wq2, wk, wv, jnp.asarray(e2, BF16)


def _block_diag_heads(w):
    H, d, _ = w.shape
    eye = jnp.eye(H, dtype=w.dtype)
    return (eye[:, None, :, None] * w[:, :, None, :]).reshape(H * d, H * d)


def kernel(x, c, ctx, c_ctx, mod_w, mod_b, norm1_g, norm2_g, w_in, lru_conv_w, lru_conv_b, lru_w_a, lru_b_a, lru_w_i, lru_b_i, lru_lambda, mla_q_norm_g, mla_w_uq, mla_kv_norm_g, mla_w_ukv, sc_conv_w, sc_conv_b, w_out, router_w, router_b, exp_w_gu, exp_b_gu, exp_w_down, exp_b_down, final_g):
    B, n, D = x.shape
    n_ctx = ctx.shape[1]
    depth = mod_w.shape[0]
    tm = min(256, n, n_ctx)

    rows = 8 * ((B + 1 + 7) // 8)
    c_all = jnp.concatenate([c, c_ctx[None, :], jnp.zeros((rows - B - 1, D), F32)], axis=0)
    mod = _modulation(c_all, mod_w, mod_b)

    cs = _channel_dft()
    cn_lat, sn_lat = _dft_tables(n)
    cn_ctx, sn_ctx = _dft_tables(n_ctx)
    cos_lat, sin_lat = _rope_tables(n, True)
    cos_ctx, sin_ctx = _rope_tables(n_ctx, False)

    h_lat, h_ctx = x, ctx
    for l in range(depth):
        with_ctx_out = l < depth - 1
        ml = mod[l, 0:B].reshape(B, 1, 6, D)
        mc = jnp.broadcast_to(mod[l, B:B + 1].reshape(1, 1, 6, D), (B, 1, 6, D))
        part = lambda m, i: m[:, :, i, :]
        w_in_p = _relayout_w_in(w_in[l])
        wq2, wk, wv, e2 = _relayout_mla(mla_w_uq[l], mla_w_ukv[l])
        wai = jnp.stack([jnp.concatenate([_block_diag_heads(lru_w_a[l, d]), _block_diag_heads(lru_w_i[l, d])], axis=1)
                         for d in range(2)]).astype(BF16)
        bai = jnp.concatenate([lru_b_a[l], lru_b_i[l]], axis=1)[:, None, :]
        lru_args = (lru_conv_w[l], lru_conv_b[l][:, None, :], wai, bai, lru_lambda[l][:, None, :])
        w_out_b = w_out[l].astype(BF16)

        zf_c, lru_c, mla_c, u_c, sb_c = _in_projection(h_ctx, norm1_g[l], part(mc, 0), part(mc, 1), w_in_p, cs)
        zf_l, lru_l, mla_l, u_l, sb_l = _in_projection(h_lat, norm1_g[l], part(ml, 0), part(ml, 1), w_in_p, cs)

        h0 = jnp.zeros((B, 2, 1, GROUP_W), F32)
        b_ctx, s_ctx = _rglru(lru_c, h0, *lru_args)
        b_lat, _ = _rglru(lru_l, s_ctx, *lru_args)

        q_c, k_c, v_c = _mla_prep(mla_c, mla_q_norm_g[l], mla_kv_norm_g[l], wq2, wk, wv, e2, cos_ctx, sin_ctx)
        q_l, k_l, v_l = _mla_prep(mla_l, mla_q_norm_g[l], mla_kv_norm_g[l], wq2, wk, wv, e2, cos_lat, sin_lat)
        k_all = jnp.concatenate([k_c, k_l], axis=2)
        v_all = jnp.concatenate([v_c, v_l], axis=1)
        c_lat = _attention(q_l, k_all, v_all)

        a_lat = _fourier_positions(zf_l, cn_lat, sn_lat)
        d_lat = _short_conv(u_l, sb_l, sc_conv_w[l], sc_conv_b[l])

        hn_lat, f_lat, ti_lat, tg_lat = _out_projection(
            a_lat, b_lat, c_lat, d_lat, h_lat, w_out_b, part(ml, 2), norm2_g[l], part(ml, 3), part(ml, 4),
            router_w[l], router_b[l])

        moe_w = (exp_w_gu, exp_b_gu, exp_w_down, exp_b_down, l)
        tokens_on_lanes = lambda u: jnp.transpose(u, (1, 0, 2)).reshape(8, -1)
        g2_lat = part(ml, 5)
        if with_ctx_out:
            a_ctx = _fourier_positions(zf_c, cn_ctx, sn_ctx)
            c_ctx_out = _attention(q_c, k_c, v_c)
            d_ctx = _short_conv(u_c, sb_c, sc_conv_w[l], sc_conv_b[l])
            hn_ctx, f_ctx, ti_ctx, tg_ctx = _out_projection(
                a_ctx, b_ctx, c_ctx_out, d_ctx, h_ctx, w_out_b, part(mc, 2), norm2_g[l], part(mc, 3), part(mc, 4),
                router_w[l], router_b[l])
            S = n_ctx + n
            cat = lambda u, v: jnp.concatenate([u, v], axis=1)
            g2_rows = jnp.concatenate([jnp.broadcast_to(part(mc, 5), (B, n_ctx // tm, D)),
                                       jnp.broadcast_to(g2_lat, (B, n // tm, D))], axis=1)
            cat_t = lambda u, v: tokens_on_lanes(jnp.concatenate([u, v], axis=2))
            h_all = _moe(cat(f_ctx, f_lat).reshape(B * S * ROW_TILE, 128), cat(hn_ctx, hn_lat).reshape(B * S, D),
                         cat_t(ti_ctx, ti_lat), cat_t(tg_ctx, tg_lat),
                         g2_rows.reshape(B * S // tm, 1, D), *moe_w, final_g, tm, False)
            h_all = h_all.reshape(B, S, D)
            h_ctx, h_lat = h_all[:, 0:n_ctx], h_all[:, n_ctx:]
        else:
            g2_rows = jnp.broadcast_to(g2_lat, (B, n // tm, D)).reshape(B * n // tm, 1, D)
            h_lat = _moe(f_lat.reshape(B * n * ROW_TILE, 128), hn_lat.reshape(B * n, D),
                         tokens_on_lanes(ti_lat), tokens_on_lanes(tg_lat), g2_rows,
                         *moe_w, final_g, tm, True).reshape(B, n, D)

    return h_lat
```

```python
import functools
import math

import numpy as np
import jax
import jax.numpy as jnp
from jax import lax
from jax.experimental import pallas as pl
from jax.experimental.pallas import tpu as pltpu

F32 = jnp.float32
BF16 = jnp.bfloat16

D_MODEL = 1024
GRID_W = 64
GROUP_W = 256
EPS = 1e-6
FNET_CH = 64
LRU_HEADS = 4
LRU_C = 8.0
MLA_HEADS = 4
MLA_NOPE = 64
MLA_ROPE = 32
MLA_V = 64
MLA_Q_LORA = 256
MLA_KV_LORA = 128
ROPE_BASE = 10000.0
ATTN_SCALE = (MLA_NOPE + MLA_ROPE) ** -0.5
LOG2_E = math.log2(math.e)
N_EXPERTS = 32
TOP_K = 4
SWIGLU_ALPHA = 1.702
SWIGLU_LIMIT = 7.0

HEAD_PAD = 128
V_ROWS = 80
P_PAD = 2048
MOE_BM = 512
VMEM_LIMIT = 56 * 1024 * 1024


def _cparams(n_axes, vmem=None):
    return pltpu.CompilerParams(dimension_semantics=("arbitrary",) * n_axes,
                                vmem_limit_bytes=vmem or VMEM_LIMIT)


def _bdot(a, b):
    return jnp.dot(a.astype(BF16), b.astype(BF16), preferred_element_type=F32)


def _mod_kernel(c_ref, w_ref, b_ref, o_ref):
    c = c_ref[...]
    s = c * jax.nn.sigmoid(c)
    o_ref[0] = _bdot(s, w_ref[0]) + b_ref[0]


def _modulation(c_all, mod_w, mod_b):
    L, D, N = mod_w.shape
    R = c_all.shape[0]
    tn = 1024
    return pl.pallas_call(
        _mod_kernel,
        out_shape=jax.ShapeDtypeStruct((L, R, N), F32),
        grid=(L, N // tn),
        in_specs=[pl.BlockSpec((R, D), lambda l, j: (0, 0)),
                  pl.BlockSpec((1, D, tn), lambda l, j: (l, 0, j)),
                  pl.BlockSpec((1, 1, tn), lambda l, j: (l, 0, j))],
        out_specs=pl.BlockSpec((1, R, tn), lambda l, j: (l, 0, j)),
        compiler_params=_cparams(2),
        name="adaln_mod",
    )(c_all, mod_w, mod_b.reshape(L, 1, N))


def _inproj_kernel(h_ref, g_ref, sh_ref, sc_ref, w_ref, cs_ref, zf_ref, lru_ref, mla_ref, u_ref, sb_ref):
    x = h_ref[0]
    y = x * lax.rsqrt(jnp.mean(x * x, axis=-1, keepdims=True) + EPS) * g_ref[...]
    a = y * (1.0 + sc_ref[0]) + sh_ref[0]
    p = _bdot(a, w_ref[...])
    zf_ref[0] = _bdot(p[:, 0:256], cs_ref[...]).astype(BF16)
    lru_ref[0] = p[:, 256:768]
    mla_ref[0] = p[:, 768:1280]
    u_ref[0] = p[:, 1280:1536] * p[:, 1792:2048]
    sb_ref[0] = p[:, 1536:1792]


def _in_projection(h, g, sh, sc, w_in_p, cs):
    B, n, D = h.shape
    tm = min(512, n)
    tile = lambda w: pl.BlockSpec((1, tm, w), lambda b, i: (b, i, 0))
    vec = pl.BlockSpec((1, 1, D), lambda b, i: (b, 0, 0))
    return pl.pallas_call(
        _inproj_kernel,
        out_shape=(jax.ShapeDtypeStruct((B, n, 512), BF16),
                   jax.ShapeDtypeStruct((B, n, 512), F32),
                   jax.ShapeDtypeStruct((B, n, 512), F32),
                   jax.ShapeDtypeStruct((B, n, 256), F32),
                   jax.ShapeDtypeStruct((B, n, 256), F32)),
        grid=(B, n // tm),
        in_specs=[tile(D),
                  pl.BlockSpec((1, D), lambda b, i: (0, 0)),
                  vec, vec,
                  pl.BlockSpec((D, P_PAD), lambda b, i: (0, 0)),
                  pl.BlockSpec((256, 512), lambda b, i: (0, 0))],
        out_specs=(tile(512), tile(512), tile(512), tile(256), tile(256)),
        compiler_params=_cparams(2),
        name="norm_inproj",
    )(h, g.reshape(1, D), sh, sc, w_in_p, cs)


def _fnet_kernel(c_ref, s_ref, z_ref, o_ref):
    z = z_ref[0]
    acc = jnp.dot(c_ref[...], z[:, 0:256], preferred_element_type=F32)
    acc = acc - jnp.dot(s_ref[...], z[:, 256:512], preferred_element_type=F32)
    o_ref[0] = acc.astype(BF16)


def _fourier_positions(zf, cn, sn):
    B, n, _ = zf.shape
    tm = min(512, n)
    return pl.pallas_call(
        _fnet_kernel,
        out_shape=jax.ShapeDtypeStruct((B, n, 256), BF16),
        grid=(n // tm, B),
        in_specs=[pl.BlockSpec((tm, n), lambda i, b: (i, 0)),
                  pl.BlockSpec((tm, n), lambda i, b: (i, 0)),
                  pl.BlockSpec((1, n, 512), lambda i, b: (b, 0, 0))],
        out_specs=pl.BlockSpec((1, tm, 256), lambda i, b: (b, i, 0)),
        compiler_params=_cparams(2),
        name="fourier_positions",
    )(cn, sn, zf)


def _shift_rows(xcat, off, tc):
    if off == 0:
        return xcat[8:8 + tc]
    rolled = pltpu.roll(xcat, (tc + 16) - (8 + off), axis=0)
    return rolled[0:tc]


def _load_halo(ref, col0, width, start, tc, n):
    main = ref[0, pl.ds(start, tc), col0:col0 + width]
    ps = jnp.maximum(start - 8, 0)
    prev = ref[0, pl.ds(pl.multiple_of(ps, 8), 8), col0:col0 + width]
    prev = jnp.where(start > 0, prev, 0.0)
    ns = jnp.minimum(start + tc, n - 8)
    nxt = ref[0, pl.ds(pl.multiple_of(ns, 8), 8), col0:col0 + width]
    nxt = jnp.where(start + tc < n, nxt, 0.0)
    return jnp.concatenate([prev, main, nxt], axis=0)


def _lru_kernel(x_ref, h0_ref, cw_ref, cb_ref, wai_ref, bai_ref, lam_ref, o_ref, hf_ref, hs_ref, *, tc, n):
    nc = n // tc
    row = lax.broadcasted_iota(jnp.int32, (tc, GROUP_W), 0)

    def direction(d, reverse):
        left = 1 if reverse else 2
        cw = cw_ref[d]
        cb = cb_ref[d]
        wai = wai_ref[d]
        bai = bai_ref[d]
        sp = jax.nn.softplus(-lam_ref[d])

        def chunk(ci, carry):
            c = (nc - 1 - ci) if reverse else ci
            start = pl.multiple_of(c * tc, tc)
            xcat = _load_halo(x_ref, 0, GROUP_W, start, tc, n)
            xc = cb
            for k in range(4):
                xc = xc + cw[k:k + 1, :] * _shift_rows(xcat, k - left, tc)
            gates = jax.nn.sigmoid(_bdot(xc, wai) + bai)
            r = gates[:, 0:GROUP_W]
            gi = gates[:, GROUP_W:2 * GROUP_W]
            a = jnp.exp(-LRU_C * r * sp)
            b = jnp.sqrt(jnp.maximum(1.0 - a * a, 0.0)) * (gi * xc)
            s = 1
            while s < tc:
                if reverse:
                    a_sh = pltpu.roll(a, tc - s, axis=0)
                    b_sh = pltpu.roll(b, tc - s, axis=0)
                    keep = row < (tc - s)
                else:
                    a_sh = pltpu.roll(a, s, axis=0)
                    b_sh = pltpu.roll(b, s, axis=0)
                    keep = row >= s
                b = jnp.where(keep, a * b_sh, 0.0) + b
                a = jnp.where(keep, a * a_sh, a)
                s *= 2
            h = a * carry + b
            if reverse:
                gate = x_ref[0, pl.ds(start, tc), GROUP_W:2 * GROUP_W]
                tot = hs_ref[pl.ds(start, tc), :] + h
                o_ref[0, pl.ds(start, tc), :] = (jax.nn.gelu(gate) * tot).astype(o_ref.dtype)
                return h[0:1, :]
            hs_ref[pl.ds(start, tc), :] = h
            return h[tc - 1:tc, :]

        return lax.fori_loop(0, nc, chunk, h0_ref[0, d])

    hf_ref[0, 0] = direction(0, False)
    hf_ref[0, 1] = direction(1, True)


def _rglru(lru, h0, cw, cb, wai, bai, lam):
    B, n, _ = lru.shape
    tc = min(256, n)
    full = lambda *s: pl.BlockSpec(s, lambda b: (0,) * len(s))
    kern = functools.partial(_lru_kernel, tc=tc, n=n)
    return pl.pallas_call(
        kern,
        out_shape=(jax.ShapeDtypeStruct((B, n, GROUP_W), BF16),
                   jax.ShapeDtypeStruct((B, 2, 1, GROUP_W), F32)),
        grid=(B,),
        in_specs=[pl.BlockSpec((1, n, 512), lambda b: (b, 0, 0)),
                  pl.BlockSpec((1, 2, 1, GROUP_W), lambda b: (b, 0, 0, 0)),
                  full(2, 4, GROUP_W), full(2, 1, GROUP_W), full(2, GROUP_W, 2 * GROUP_W),
                  full(2, 1, 2 * GROUP_W), full(2, 1, GROUP_W)],
        out_specs=(pl.BlockSpec((1, n, GROUP_W), lambda b: (b, 0, 0)),
                   pl.BlockSpec((1, 2, 1, GROUP_W), lambda b: (b, 0, 0, 0))),
        scratch_shapes=[pltpu.VMEM((n, GROUP_W), F32)],
        compiler_params=_cparams(1),
        name="rglru",
    )(lru, h0, cw, cb, wai, bai, lam)


def _sconv_kernel(u_ref, sb_ref, w_ref, b_ref, o_ref, *, tc, n):
    w = w_ref[...]
    bias = b_ref[...]

    def chunk(c, _):
        start = pl.multiple_of(c * tc, tc)
        ucat = _load_halo(u_ref, 0, GROUP_W, start, tc, n)
        y = bias
        for k in range(3):
            y = y + w[k:k + 1, :] * _shift_rows(ucat, k - 1, tc)
        o_ref[0, pl.ds(start, tc), :] = (sb_ref[0, pl.ds(start, tc), :] * y).astype(o_ref.dtype)
        return 0

    lax.fori_loop(0, n // tc, chunk, 0)


def _short_conv(u, sb, w, b):
    B, n, W = u.shape
    tc = min(256, n)
    seq = pl.BlockSpec((1, n, W), lambda i: (i, 0, 0))
    return pl.pallas_call(
        functools.partial(_sconv_kernel, tc=tc, n=n),
        out_shape=jax.ShapeDtypeStruct((B, n, W), BF16),
        grid=(B,),
        in_specs=[seq, seq, pl.BlockSpec((3, W), lambda i: (0, 0)), pl.BlockSpec((1, W), lambda i: (0, 0))],
        out_specs=seq,
        compiler_params=_cparams(1),
        name="short_conv",
    )(u, sb, w, b.reshape(1, W))


def _mla_prep_kernel(x_ref, qg_ref, kg_ref, wq_ref, wk_ref, wv_ref, e2_ref, cos_ref, sin_ref,
                     q_ref, k_ref, v_ref):
    x = x_ref[0]
    cq = x[:, 0:256]
    ckv = x[:, 256:384]
    kr = x[:, 384:448]
    cqn = cq * lax.rsqrt(jnp.mean(cq * cq, axis=-1, keepdims=True) + EPS) * qg_ref[...]
    ckvn = ckv * lax.rsqrt(jnp.mean(ckv * ckv, axis=-1, keepdims=True) + EPS) * kg_ref[...]
    cos = cos_ref[...]
    sin = sin_ref[...]
    q2 = _bdot(cqn, wq_ref[...])
    q = (q2[:, 0:512] * cos + q2[:, 512:1024] * sin) * (ATTN_SCALE * LOG2_E)
    kr2 = _bdot(kr, e2_ref[...])
    ckvb = ckvn.astype(BF16)
    k = _bdot(ckvb, wk_ref[...]) + kr2[:, 0:512] * cos + kr2[:, 512:1024] * sin
    vt = lax.dot_general(wv_ref[...], ckvb, (((1,), (1,)), ((), ())), preferred_element_type=F32)
    tm = x.shape[0]
    ones_rows = jnp.where(lax.broadcasted_iota(jnp.int32, (V_ROWS - MLA_V, tm), 0) == 0, 1.0, 0.0)
    pieces = []
    for h in range(MLA_HEADS):
        q_ref[0, h] = q[:, h * HEAD_PAD:(h + 1) * HEAD_PAD].astype(BF16)
        k_ref[0, h] = k[:, h * HEAD_PAD:(h + 1) * HEAD_PAD].astype(BF16)
        pieces += [vt[h * MLA_V:(h + 1) * MLA_V], ones_rows]
    v_ref[0, 0] = jnp.concatenate(pieces, axis=0).astype(BF16)


def _mla_prep(mla, qg, kg, wq2, wk, wv, e2, cos, sin):
    B, n, _ = mla.shape
    tm = min(256, n)
    HW = MLA_HEADS * HEAD_PAD
    full = lambda *s: pl.BlockSpec(s, lambda b, i: (0,) * len(s))
    return pl.pallas_call(
        _mla_prep_kernel,
        out_shape=(jax.ShapeDtypeStruct((B, MLA_HEADS, n, HEAD_PAD), BF16),
                   jax.ShapeDtypeStruct((B, MLA_HEADS, n, HEAD_PAD), BF16),
                   jax.ShapeDtypeStruct((B, n // tm, MLA_HEADS * V_ROWS, tm), BF16)),
        grid=(B, n // tm),
        in_specs=[pl.BlockSpec((1, tm, 512), lambda b, i: (b, i, 0)),
                  full(1, MLA_Q_LORA), full(1, MLA_KV_LORA),
                  full(MLA_Q_LORA, 2 * HW), full(MLA_KV_LORA, HW), full(MLA_HEADS * MLA_V, MLA_KV_LORA),
                  full(2 * MLA_ROPE, 2 * HW),
                  pl.BlockSpec((tm, HW), lambda b, i: (i, 0)),
                  pl.BlockSpec((tm, HW), lambda b, i: (i, 0))],
        out_specs=(pl.BlockSpec((1, MLA_HEADS, tm, HEAD_PAD), lambda b, i: (b, 0, i, 0)),
                   pl.BlockSpec((1, MLA_HEADS, tm, HEAD_PAD), lambda b, i: (b, 0, i, 0)),
                   pl.BlockSpec((1, 1, MLA_HEADS * V_ROWS, tm), lambda b, i: (b, i, 0, 0))),
        compiler_params=_cparams(2),
        name="mla_prep",
    )(mla, qg.reshape(1, -1), kg.reshape(1, -1), wq2, wk, wv, e2, cos, sin)


def _attn_kernel(q_ref, k_ref, vt_ref, o_ref, st_a, st_b, p_a, p_b, *, tk, n_kt):
    tq = q_ref.shape[2]
    H = MLA_HEADS
    qs = [q_ref[0, h] for h in range(H)]

    def scores(j, h):
        kt = k_ref[0, h, pl.ds(pl.multiple_of(j * tk, tk), tk), :]
        return lax.dot_general(kt, qs[h], (((1,), (1,)), ((), ())), preferred_element_type=F32)

    def weighted_values(j, h, p):
        vt = vt_ref[0, j, h * V_ROWS:(h + 1) * V_ROWS, :]
        return jnp.dot(vt, p, preferred_element_type=F32)

    def step(j, st_cur, st_next, p_prev, p_cur, carry):
        a_prev, m, acc = carry
        jn = jnp.minimum(j + 1, n_kt - 1)
        for h in range(H):
            st_next[h] = scores(jn, h)
        pv = [weighted_values(jnp.maximum(j - 1, 0), h, p_prev[h]) for h in range(H)]
        m_new, alpha = [], []
        for h in range(H):
            st = st_cur[h]
            mh = jnp.maximum(m[h], jnp.max(st, axis=0, keepdims=True))
            alpha.append(jnp.exp2(m[h] - mh))
            m_new.append(mh)
            p_cur[h] = jnp.exp2((st - mh).astype(BF16))
        acc_new = [a_prev[h] * acc[h] + pv[h] for h in range(H)]
        return (tuple(alpha), tuple(m_new), tuple(acc_new))

    even = lambda j, c: step(j, st_a, st_b, p_b, p_a, c)
    odd = lambda j, c: step(j, st_b, st_a, p_a, p_b, c)

    for h in range(H):
        st_a[h] = scores(0, h)
    p_b[...] = jnp.zeros(p_b.shape, BF16)
    per_head = lambda f: tuple(f() for _ in range(H))
    carry = (per_head(lambda: jnp.ones((1, tq), F32)),
             per_head(lambda: jnp.full((1, tq), -jnp.inf, F32)),
             per_head(lambda: jnp.zeros((V_ROWS, tq), F32)))
    carry = lax.fori_loop(0, n_kt // 2, lambda i, c: odd(2 * i + 1, even(2 * i, c)), carry)
    if n_kt % 2:
        carry = even(n_kt - 1, carry)
    p_last = p_a if n_kt % 2 else p_b
    a_last, _, acc = carry
    outs = []
    for h in range(H):
        tot = a_last[h] * acc[h] + weighted_values(n_kt - 1, h, p_last[h])
        outs.append(tot[0:MLA_V] / tot[MLA_V:MLA_V + 1])
    o_ref[0] = jnp.concatenate(outs, axis=0).T.astype(o_ref.dtype)


def _attention(q, k, vt):
    B, H, n, _ = q.shape
    M = k.shape[2]
    n_kt, tk = vt.shape[1], vt.shape[3]
    tq = min(256, n)
    return pl.pallas_call(
        functools.partial(_attn_kernel, tk=tk, n_kt=n_kt),
        out_shape=jax.ShapeDtypeStruct((B, n, H * MLA_V), BF16),
        grid=(B, n // tq),
        in_specs=[pl.BlockSpec((1, H, tq, HEAD_PAD), lambda b, i: (b, 0, i, 0)),
                  pl.BlockSpec((1, H, M, HEAD_PAD), lambda b, i: (b, 0, 0, 0)),
                  pl.BlockSpec((1, n_kt, H * V_ROWS, tk), lambda b, i: (b, 0, 0, 0))],
        out_specs=pl.BlockSpec((1, tq, H * MLA_V), lambda b, i: (b, i, 0)),
        scratch_shapes=[pltpu.VMEM((H, tk, tq), F32), pltpu.VMEM((H, tk, tq), F32),
                        pltpu.VMEM((H, tk, tq), BF16), pltpu.VMEM((H, tk, tq), BF16)],
        compiler_params=_cparams(2),
        name="mla_attention",
    )(q, k, vt)


ROW_TILE = 8


def _store_rows_contiguous(ref, lead, x):
    rows = x.shape[0]
    for j in range(ROW_TILE):
        ref[lead + (pl.ds(j, rows, stride=ROW_TILE), slice(None))] = x[:, j * 128:(j + 1) * 128]


def _load_rows_contiguous(ref, lead, rows, dtype):
    return jnp.concatenate([ref[lead + (pl.ds(j, rows, stride=ROW_TILE), slice(None))].astype(dtype)
                            for j in range(ROW_TILE)], axis=-1)


def _outproj_kernel(a_ref, b_ref, c_ref, d_ref, h_ref, w_ref, g1_ref, ng_ref, sh_ref, sc_ref, rw_ref, rb_ref,
                    hn_ref, f_ref, ti_ref, tg_ref):
    mixed = jnp.concatenate([a_ref[0], b_ref[0], c_ref[0], d_ref[0]], axis=-1)
    hn = h_ref[0] + g1_ref[0] * jnp.dot(mixed, w_ref[...], preferred_element_type=F32)
    hn_ref[0] = hn
    y = hn * lax.rsqrt(jnp.mean(hn * hn, axis=-1, keepdims=True) + EPS) * ng_ref[...]
    f = y * (1.0 + sc_ref[0]) + sh_ref[0]
    _store_rows_contiguous(f_ref, (0,), f)
    nt = (((1,), (1,)), ((), ()))
    rw = rw_ref[...]
    rw_hi = rw.astype(BF16)
    rw_lo = (rw - rw_hi.astype(F32)).astype(BF16)
    f_hi = f.astype(BF16)
    f_lo = (f - f_hi.astype(F32)).astype(BF16)
    logits = (lax.dot_general(rw_hi, f_hi, nt, preferred_element_type=F32)
              + (lax.dot_general(rw_hi, f_lo, nt, preferred_element_type=F32)
                 + lax.dot_general(rw_lo, f_hi, nt, preferred_element_type=F32))) + rb_ref[...]
    tm = logits.shape[1]
    sub = lax.broadcasted_iota(jnp.int32, (N_EXPERTS, tm), 0)
    row = lax.broadcasted_iota(jnp.int32, (8, tm), 0)
    top_v = jnp.full((8, tm), -jnp.inf, F32)
    top_i = jnp.zeros((8, tm), jnp.int32)
    cur = logits
    m0 = None
    for kk in range(TOP_K):
        m = jnp.max(cur, axis=0, keepdims=True)
        sel = jnp.min(jnp.where(cur == m, sub, N_EXPERTS), axis=0, keepdims=True)
        if kk == 0:
            m0 = m
        top_v = jnp.where(row == kk, m, top_v)
        top_i = jnp.where(row == kk, sel, top_i)
        cur = jnp.where(sub == sel, -jnp.inf, cur)
    e = jnp.where(row < TOP_K, jnp.exp(top_v - m0), 0.0)
    ti_ref[0] = top_i
    tg_ref[0] = e / jnp.sum(e, axis=0, keepdims=True)


def _out_projection(a, b, c, d, h, w_out, g1, ng, sh, sc, rw, rb):
    B, n, D = h.shape
    tm = min(512, n)
    tile = lambda w: pl.BlockSpec((1, tm, w), lambda bb, i: (bb, i, 0))
    vec = pl.BlockSpec((1, 1, D), lambda bb, i: (bb, 0, 0))
    full = lambda *s: pl.BlockSpec(s, lambda bb, i: (0,) * len(s))
    return pl.pallas_call(
        _outproj_kernel,
        out_shape=(jax.ShapeDtypeStruct((B, n, D), F32),
                   jax.ShapeDtypeStruct((B, n * ROW_TILE, 128), F32),
                   jax.ShapeDtypeStruct((B, 8, n), jnp.int32),
                   jax.ShapeDtypeStruct((B, 8, n), F32)),
        grid=(B, n // tm),
        in_specs=[tile(256), tile(256), tile(256), tile(256), tile(D),
                  full(D, D), vec, full(1, D), vec, vec, full(N_EXPERTS, D), full(N_EXPERTS, 1)],
        out_specs=(tile(D), pl.BlockSpec((1, tm * ROW_TILE, 128), lambda bb, i: (bb, i, 0)),
                   pl.BlockSpec((1, 8, tm), lambda bb, i: (bb, 0, i)),
                   pl.BlockSpec((1, 8, tm), lambda bb, i: (bb, 0, i))),
        compiler_params=_cparams(2),
        name="outproj_router",
    )(a, b, c, d, h, w_out, g1, ng.reshape(1, D), sh, sc, rw.T, rb.reshape(N_EXPERTS, 1))


def _rank_kernel(ti_ref, rank_ref, cnt_ref, carry_ref, *, tm):
    @pl.when(pl.program_id(0) == 0)
    def _():
        carry_ref[...] = jnp.zeros_like(carry_ref)

    ti = ti_ref[...]
    sub = lax.broadcasted_iota(jnp.int32, (N_EXPERTS, tm), 0)
    hot = jnp.zeros((N_EXPERTS, tm), F32)
    for kk in range(TOP_K):
        hot = hot + jnp.where(sub == ti[kk:kk + 1, :], 1.0, 0.0)
    r = lax.broadcasted_iota(jnp.int32, (tm, tm), 0)
    c = lax.broadcasted_iota(jnp.int32, (tm, tm), 1)
    earlier = jnp.where(r < c, 1.0, 0.0).astype(BF16)
    carry = carry_ref[:, 0:1]
    before = jnp.dot(hot.astype(BF16), earlier, preferred_element_type=F32) + carry
    row = lax.broadcasted_iota(jnp.int32, (8, tm), 0)
    out = jnp.zeros((8, tm), jnp.int32)
    for kk in range(TOP_K):
        rk = jnp.sum(jnp.where(sub == ti[kk:kk + 1, :], before, 0.0), axis=0, keepdims=True)
        out = jnp.where(row == kk, rk.astype(jnp.int32), out)
    rank_ref[...] = out
    total = carry + jnp.sum(hot, axis=1, keepdims=True)
    carry_ref[...] = jnp.broadcast_to(total, carry_ref.shape)
    cnt_ref[...] = jnp.broadcast_to(total, cnt_ref.shape).astype(jnp.int32)


def _expert_ranks(top_i_t):
    T = top_i_t.shape[1]
    tm = 512 if T % 512 == 0 else 256
    return pl.pallas_call(
        functools.partial(_rank_kernel, tm=tm),
        out_shape=(jax.ShapeDtypeStruct((8, T), jnp.int32), jax.ShapeDtypeStruct((N_EXPERTS, 128), jnp.int32)),
        grid=(T // tm,),
        in_specs=[pl.BlockSpec((8, tm), lambda i: (0, i))],
        out_specs=(pl.BlockSpec((8, tm), lambda i: (0, i)), pl.BlockSpec((N_EXPERTS, 128), lambda i: (0, 0))),
        scratch_shapes=[pltpu.VMEM((N_EXPERTS, 128), F32)],
        compiler_params=_cparams(1),
        name="moe_rank",
    )(top_i_t)


def _stream_maps(nc, nl):
    per_batch = nc + nl
    ctx_map = lambda i, *_: ((i // per_batch) * nc + jnp.minimum(i % per_batch, nc - 1), 0)
    lat_map = lambda i, *_: ((i // per_batch) * nl + jnp.maximum(i % per_batch - nc, 0), 0)
    return [ctx_map, lat_map] if nc > 0 else [lat_map]


def _dispatch_kernel(cnt_ref, ps_ref, pe_ref, pos_ref, *refs, tm, n_slots, nc, nl):
    n_in = 2 if nc > 0 else 1
    f_refs = refs[:n_in]
    xs_ref, buf_ref, zrow_ref, sem_ref, zsem_ref = refs[n_in:]
    i = pl.program_id(0)
    nt = pl.num_programs(0)
    copies = tm * TOP_K
    zslots = zrow_ref.shape[0] // ROW_TILE

    def tile_rows(first_row, n_rows):
        return pl.ds(pl.multiple_of(first_row * ROW_TILE, ROW_TILE), n_rows * ROW_TILE)

    def zero_fill(lo, hi, start):
        def chunk(off, size):
            cp = pltpu.make_async_copy(zrow_ref.at[tile_rows(0, size)], xs_ref.at[tile_rows(off, size)], zsem_ref)
            cp.start() if start else cp.wait()

        n = hi - lo

        def whole(j, _):
            chunk(lo + j * zslots, zslots)
            return 0

        lax.fori_loop(0, n // zslots, whole, 0)
        rem_lo = lo + (n // zslots) * zslots
        rem = n % zslots
        size = zslots // 2
        while size >= 1:
            @pl.when((rem & size) != 0)
            def _(size=size):
                chunk(rem_lo + (rem & ~(2 * size - 1)), size)
            size //= 2

    @pl.when(i == 0)
    def _():
        zrow_ref[...] = jnp.zeros_like(zrow_ref)
        for start in (True, False):
            for e in range(N_EXPERTS):
                zero_fill(ps_ref[e] + cnt_ref[e], pe_ref[e], start)
            zero_fill(pe_ref[N_EXPERTS - 1], n_slots, start)

    slot = i % 2
    if nc > 0:
        j = i % (nc + nl)

        @pl.when(j < nc)
        def _():
            buf_ref[slot] = f_refs[0][...]

        @pl.when(j >= nc)
        def _():
            buf_ref[slot] = f_refs[1][...]
    else:
        buf_ref[slot] = f_refs[0][...]

    def issue(t, _):
        for kk in range(TOP_K):
            pltpu.make_async_copy(buf_ref.at[slot, tile_rows(t, 1)],
                                  xs_ref.at[tile_rows(pos_ref[0, 0, t * TOP_K + kk], 1)],
                                  sem_ref.at[slot]).start(priority=kk % 2)
        return 0

    lax.fori_loop(0, tm, issue, 0, unroll=2)

    def wait_rows(s):
        pltpu.make_async_copy(xs_ref.at[tile_rows(0, copies)], xs_ref.at[tile_rows(0, copies)], sem_ref.at[s]).wait()

    @pl.when(i > 0)
    def _():
        wait_rows(1 - slot)

    @pl.when(i == nt - 1)
    def _():
        wait_rows(slot)


def _dispatch(fs, pos, counts, pad_start, pad_end, n_slots, tm, nc, nl):
    nt = pos.shape[0] // (tm * TOP_K)
    grid_spec = pltpu.PrefetchScalarGridSpec(
        num_scalar_prefetch=3,
        grid=(nt,),
        in_specs=[pl.BlockSpec((1, 1, tm * TOP_K), lambda i, *_: (i, 0, 0), memory_space=pltpu.SMEM)]
        + [pl.BlockSpec((tm * ROW_TILE, 128), m) for m in _stream_maps(nc, nl)],
        out_specs=pl.BlockSpec(memory_space=pl.ANY),
        scratch_shapes=[pltpu.VMEM((2, tm * ROW_TILE, 128), F32), pltpu.VMEM((256 * ROW_TILE, 128), F32),
                        pltpu.SemaphoreType.DMA((2,)), pltpu.SemaphoreType.DMA],
    )
    return pl.pallas_call(
        functools.partial(_dispatch_kernel, tm=tm, n_slots=n_slots, nc=nc, nl=nl),
        out_shape=jax.ShapeDtypeStruct((n_slots * ROW_TILE, 128), F32),
        grid_spec=grid_spec,
        compiler_params=_cparams(1),
        name="moe_dispatch",
    )(counts, pad_start, pad_end, pos.reshape(nt, 1, tm * TOP_K), *fs)


def _ffn_kernel(be_ref, bv_ref, x_ref, wgu_ref, bgu_ref, wd_ref, bd_ref, y_ref, wgu_s, wd_s, *, bm):
    i = pl.program_id(0)
    valid = bv_ref[i]
    half = bm // 2

    def experts_rows(rows):
        De = wd_ref.shape[2]
        x = _load_rows_contiguous(x_ref, (), rows, BF16)
        gu = jnp.dot(x, wgu_s[...], preferred_element_type=F32) + bgu_ref[0, 0]
        g = jnp.minimum(gu[:, 0:De], SWIGLU_LIMIT)
        u = jnp.clip(gu[:, De:2 * De], -SWIGLU_LIMIT, SWIGLU_LIMIT)
        act = (u + 1.0) * (g * jax.nn.sigmoid(SWIGLU_ALPHA * g))
        y = jnp.dot(act.astype(BF16), wd_s[...], preferred_element_type=F32) + bd_ref[0, 0]
        _store_rows_contiguous(y_ref, (), y)

    @pl.when(valid > 0)
    def _():
        @pl.when((i == 0) | (be_ref[i] != be_ref[jnp.maximum(i - 1, 0)]))
        def _():
            wgu_s[...] = wgu_ref[0, 0].astype(BF16)
            wd_s[...] = wd_ref[0, 0].astype(BF16)

        @pl.when(valid > half)
        def _():
            experts_rows(bm)

        @pl.when(valid <= half)
        def _():
            experts_rows(half)
            y_ref[half * ROW_TILE:bm * ROW_TILE, :] = jnp.zeros(((bm - half) * ROW_TILE, 128), y_ref.dtype)

    @pl.when(valid <= 0)
    def _():
        y_ref[...] = jnp.zeros_like(y_ref)


def _expert_ffn(xs, block_exp, block_valid, wgu, bgu, wd, bd, layer, bm):
    n_slots = xs.shape[0] // ROW_TILE
    L, E, D, G = wgu.shape
    nb = n_slots // bm
    wmap = lambda i, be, bv: (layer, be[i], 0, 0)
    rows_spec = pl.BlockSpec((bm * ROW_TILE, 128), lambda i, be, bv: (i, 0))
    grid_spec = pltpu.PrefetchScalarGridSpec(
        num_scalar_prefetch=2,
        grid=(nb,),
        in_specs=[rows_spec,
                  pl.BlockSpec((1, 1, D, G), wmap),
                  pl.BlockSpec((1, 1, 1, G), wmap),
                  pl.BlockSpec((1, 1, G // 2, D), wmap),
                  pl.BlockSpec((1, 1, 1, D), wmap)],
        out_specs=rows_spec,
        scratch_shapes=[pltpu.VMEM((D, G), BF16), pltpu.VMEM((G // 2, D), BF16)],
    )
    return pl.pallas_call(
        functools.partial(_ffn_kernel, bm=bm),
        out_shape=jax.ShapeDtypeStruct((n_slots * ROW_TILE, 128), F32),
        grid_spec=grid_spec,
        compiler_params=_cparams(1),
        name="moe_expert_ffn",
    )(block_exp, block_valid, xs, wgu, bgu.reshape(L, E, 1, G), wd, bd.reshape(L, E, 1, D))


def _combine_kernel(pos_ref, pos_next_ref, y_ref, *refs, tm, nc, nl, final_norm):
    n_s = 2 if nc > 0 else 1
    h_refs = refs[:n_s]
    g2_ref, gate_ref, fg_ref = refs[n_s:n_s + 3]
    o_refs = refs[n_s + 3:2 * n_s + 3]
    buf_ref, sem_ref = refs[2 * n_s + 3:]
    i = pl.program_id(0)
    nt = pl.num_programs(0)
    slot = i % 2

    def tile_rows(first_row):
        return pl.ds(pl.multiple_of(first_row * ROW_TILE, ROW_TILE), ROW_TILE)

    def gather_tile(p_ref, s):
        def issue(r, _):
            for kk in range(TOP_K):
                pltpu.make_async_copy(y_ref.at[tile_rows(p_ref[0, 0, r * TOP_K + kk])],
                                      buf_ref.at[s, kk, tile_rows(r)], sem_ref.at[s]).start(priority=kk % 2)
            return 0

        lax.fori_loop(0, tm, issue, 0, unroll=2)

    @pl.when(i == 0)
    def _():
        gather_tile(pos_ref, 0)

    @pl.when(i + 1 < nt)
    def _():
        gather_tile(pos_next_ref, 1 - slot)

    pltpu.make_async_copy(buf_ref.at[slot], buf_ref.at[slot], sem_ref.at[slot]).wait()
    gate = gate_ref[...]
    acc = gate[:, 0:1] * _load_rows_contiguous(buf_ref, (slot, 0), tm, F32)
    for kk in range(1, TOP_K):
        acc = acc + gate[:, kk:kk + 1] * _load_rows_contiguous(buf_ref, (slot, kk), tm, F32)
    moe = g2_ref[0] * acc

    def finish(h_ref, o_ref):
        out = h_ref[...] + moe
        if final_norm:
            out = out * lax.rsqrt(jnp.mean(out * out, axis=-1, keepdims=True) + EPS) * fg_ref[...]
        o_ref[...] = out

    if nc > 0:
        j = i % (nc + nl)
        pl.when(j < nc)(lambda: finish(h_refs[0], o_refs[0]))
        pl.when(j >= nc)(lambda: finish(h_refs[1], o_refs[1]))
    else:
        finish(h_refs[0], o_refs[0])


def _combine(y, pos, hs, g2_tiles, gates, final_g, tm, nc, nl, final_norm):
    D = hs[0].shape[1]
    nt = pos.shape[0] // (tm * TOP_K)
    maps = _stream_maps(nc, nl)
    streams = [pl.BlockSpec((tm, D), m) for m in maps]
    pos3 = pos.reshape(nt, 1, tm * TOP_K)
    outs = pl.pallas_call(
        functools.partial(_combine_kernel, tm=tm, nc=nc, nl=nl, final_norm=final_norm),
        out_shape=[jax.ShapeDtypeStruct(h.shape, F32) for h in hs],
        grid=(nt,),
        in_specs=[pl.BlockSpec((1, 1, tm * TOP_K), lambda i: (i, 0, 0), memory_space=pltpu.SMEM),
                  pl.BlockSpec((1, 1, tm * TOP_K), lambda i: (jnp.minimum(i + 1, nt - 1), 0, 0),
                               memory_space=pltpu.SMEM),
                  pl.BlockSpec(memory_space=pl.ANY)]
        + streams
        + [pl.BlockSpec((1, 1, D), lambda i: (i, 0, 0)),
           pl.BlockSpec((tm, 8), lambda i: (i, 0)),
           pl.BlockSpec((1, D), lambda i: (0, 0))],
        out_specs=streams,
        scratch_shapes=[pltpu.VMEM((2, TOP_K, tm * ROW_TILE, 128), F32), pltpu.SemaphoreType.DMA((2,))],
        compiler_params=_cparams(1),
        name="moe_combine",
    )(pos3, pos3, y, *hs, g2_tiles, gates, final_g.reshape(1, D))
    return outs


def _slots_kernel(ps_ref, ti_ref, rank_ref, pos_ref):
    ti = ti_ref[...]
    start = jnp.zeros(ti.shape, jnp.int32)
    for e in range(N_EXPERTS):
        start = jnp.where(ti == e, ps_ref[e], start)
    pos_ref[...] = start + rank_ref[...]


def _slots(pad_start, top_i_t, rank_t):
    shape = top_i_t.shape
    whole = pl.BlockSpec(shape, lambda i, ps: (0, 0))
    return pl.pallas_call(
        _slots_kernel,
        out_shape=jax.ShapeDtypeStruct(shape, jnp.int32),
        grid_spec=pltpu.PrefetchScalarGridSpec(num_scalar_prefetch=1, grid=(1,), in_specs=[whole, whole],
                                               out_specs=whole),
        compiler_params=_cparams(1),
        name="moe_slots",
    )(pad_start, top_i_t, rank_t)


def _moe(fs, hs, top_i_t, top_g_t, g2_tiles, wgu, bgu, wd, bd, layer, final_g, tm, nc, nl, final_norm):
    T = top_i_t.shape[1]
    bm = MOE_BM
    n_asg = T * TOP_K
    rank_t, cnt = _expert_ranks(top_i_t)
    counts = cnt[:, 0]
    padded = (counts + bm - 1) // bm * bm
    pad_end = jnp.cumsum(padded).astype(jnp.int32)
    pad_start = pad_end - padded
    pos = _slots(pad_start, top_i_t, rank_t)[0:TOP_K].T.reshape(-1)
    nb = (n_asg + N_EXPERTS * (bm - 1) + bm - 1) // bm
    n_slots = nb * bm
    starts = jnp.arange(nb, dtype=jnp.int32) * bm
    block_exp = jnp.minimum(jnp.sum((pad_end[None, :] <= starts[:, None]).astype(jnp.int32), axis=1), N_EXPERTS - 1)
    block_valid = jnp.clip(pad_start[block_exp] + counts[block_exp] - starts, 0, bm)
    block_valid = jnp.where(starts < pad_end[-1], block_valid, 0).astype(jnp.int32)

    xs = _dispatch(fs, pos, counts, pad_start, pad_end, n_slots, tm, nc, nl)
    y = _expert_ffn(xs, block_exp, block_valid, wgu, bgu, wd, bd, layer, bm)
    return _combine(y, pos, hs, g2_tiles, top_g_t.T, final_g, tm, nc, nl, final_norm)


_ROPE_SWAP = np.concatenate([np.arange(8, 16), np.arange(0, 8), np.arange(24, 32), np.arange(16, 24)])


def _dft_tables(n):
    hi = n // 64
    j = jnp.arange(n, dtype=jnp.int32)[:, None]
    k = jnp.arange(64, dtype=jnp.int32)[None, :]
    k2 = jnp.arange(hi, dtype=jnp.int32)[None, :]
    alpha = ((j * k2) % hi).astype(F32) * (2.0 * math.pi / hi)
    beta = ((j * k) % n).astype(F32) * (2.0 * math.pi / n)
    ca, sa, cb, sb = lax.optimization_barrier((jnp.cos(alpha), jnp.sin(alpha), jnp.cos(beta), jnp.sin(beta)))
    ca, sa = ca[:, :, None], sa[:, :, None]
    cb, sb = cb[:, None, :], sb[:, None, :]
    scale = 1.0 / math.sqrt(n)
    cn = ((ca * cb - sa * sb) * scale).reshape(n, n).astype(BF16)
    sn = ((sa * cb + ca * sb) * scale).reshape(n, n).astype(BF16)
    return cn, sn


def _channel_dft():
    j = jnp.arange(GROUP_W, dtype=jnp.int32)
    same_group = (j[:, None] // FNET_CH) == (j[None, :] // FNET_CH)
    ang = ((j[:, None] * j[None, :]) % FNET_CH).astype(F32) * (2.0 * math.pi / FNET_CH)
    scale = 1.0 / math.sqrt(FNET_CH)
    c = jnp.where(same_group, jnp.cos(ang) * scale, 0.0)
    s = jnp.where(same_group, jnp.sin(ang) * scale, 0.0)
    return jnp.concatenate([c, s], axis=1).astype(BF16)


def _rope_tables(n, rotary):
    HW = MLA_HEADS * HEAD_PAD
    cos_blk = np.zeros((HEAD_PAD,), np.float32)
    cos_blk[0:MLA_NOPE + MLA_ROPE] = 1.0
    if not rotary:
        return (jnp.broadcast_to(jnp.asarray(np.tile(cos_blk, MLA_HEADS)), (n, HW)),
                jnp.zeros((n, HW), F32))
    half = MLA_ROPE // 2
    inv = ROPE_BASE ** (-jnp.arange(0, half, 2, dtype=F32) / half)
    t = jnp.arange(n, dtype=jnp.int32)
    row = (t // GRID_W).astype(F32)
    col = (t % GRID_W).astype(F32)
    ar = row[:, None] * inv[None, :]
    ac = col[:, None] * inv[None, :]
    cos32 = jnp.concatenate([jnp.cos(ar), jnp.cos(ar), jnp.cos(ac), jnp.cos(ac)], axis=1)
    sin32 = jnp.concatenate([-jnp.sin(ar), jnp.sin(ar), -jnp.sin(ac), jnp.sin(ac)], axis=1)
    ones = jnp.ones((n, MLA_NOPE), F32)
    zeros_n = jnp.zeros((n, MLA_NOPE), F32)
    zeros_p = jnp.zeros((n, HEAD_PAD - MLA_NOPE - MLA_ROPE), F32)
    cos_h = jnp.concatenate([ones, cos32, zeros_p], axis=1)
    sin_h = jnp.concatenate([zeros_n, sin32, zeros_p], axis=1)
    return jnp.tile(cos_h, (1, MLA_HEADS)), jnp.tile(sin_h, (1, MLA_HEADS))


def _relayout_w_in(w_in):
    D = w_in.shape[0]
    krope = w_in[:, 1152:1184]
    return jnp.concatenate([w_in[:, 0:1184], krope[:, _ROPE_SWAP], jnp.zeros((D, 64), w_in.dtype),
                            w_in[:, 1184:1952]], axis=1).astype(BF16)


def _relayout_mla(w_uq, w_ukv):
    dq = MLA_NOPE + MLA_ROPE
    q_main, q_swap, k_cols, v_cols = [], [], [], []
    zq = jnp.zeros((MLA_Q_LORA, HEAD_PAD - dq), w_uq.dtype)
    zn = jnp.zeros((MLA_Q_LORA, MLA_NOPE), w_uq.dtype)
    zk = jnp.zeros((MLA_KV_LORA, HEAD_PAD - MLA_NOPE), w_ukv.dtype)
    for h in range(MLA_HEADS):
        wq_h = w_uq[:, h * dq:(h + 1) * dq]
        rope = wq_h[:, MLA_NOPE:]
        q_main += [wq_h, zq]
        q_swap += [zn, rope[:, _ROPE_SWAP], zq]
        kv_h = w_ukv[:, h * (MLA_NOPE + MLA_V):(h + 1) * (MLA_NOPE + MLA_V)]
        k_cols += [kv_h[:, 0:MLA_NOPE], zk]
        v_cols += [kv_h[:, MLA_NOPE:]]
    wq2 = jnp.concatenate(q_main + q_swap, axis=1).astype(BF16)
    wk = jnp.concatenate(k_cols, axis=1).astype(BF16)
    wv = jnp.concatenate(v_cols, axis=1).T.astype(BF16)
    HW = MLA_HEADS * HEAD_PAD
    e2 = np.zeros((2 * MLA_ROPE, 2 * HW), np.float32)
    for h in range(MLA_HEADS):
        for jj in range(MLA_ROPE):
            e2[jj, h * HEAD_PAD + MLA_NOPE + jj] = 1.0
            e2[MLA_ROPE + jj, HW + h * HEAD_PAD + MLA_NOPE + jj] = 1.0
    return wq2, wk, wv, jnp.asarray(e2, BF16)


def _block_diag_heads(w):
    H, d, _ = w.shape
    eye = jnp.eye(H, dtype=w.dtype)
    return (eye[:, None, :, None] * w[:, :, None, :]).reshape(H * d, H * d)


def kernel(x, c, ctx, c_ctx, mod_w, mod_b, norm1_g, norm2_g, w_in, lru_conv_w, lru_conv_b, lru_w_a, lru_b_a, lru_w_i, lru_b_i, lru_lambda, mla_q_norm_g, mla_w_uq, mla_kv_norm_g, mla_w_ukv, sc_conv_w, sc_conv_b, w_out, router_w, router_b, exp_w_gu, exp_b_gu, exp_w_down, exp_b_down, final_g):
    B, n, D = x.shape
    n_ctx = ctx.shape[1]
    depth = mod_w.shape[0]
    tm = min(256, n, n_ctx)

    rows = 8 * ((B + 1 + 7) // 8)
    c_all = jnp.concatenate([c, c_ctx[None, :], jnp.zeros((rows - B - 1, D), F32)], axis=0)
    mod = _modulation(c_all, mod_w, mod_b)

    cs = _channel_dft()
    cn_lat, sn_lat = _dft_tables(n)
    cn_ctx, sn_ctx = _dft_tables(n_ctx)
    cos_lat, sin_lat = _rope_tables(n, True)
    cos_ctx, sin_ctx = _rope_tables(n_ctx, False)

    h_lat, h_ctx = x, ctx
    for l in range(depth):
        with_ctx_out = l < depth - 1
        ml = mod[l, 0:B].reshape(B, 1, 6, D)
        mc = jnp.broadcast_to(mod[l, B:B + 1].reshape(1, 1, 6, D), (B, 1, 6, D))
        part = lambda m, i: m[:, :, i, :]
        w_in_p = _relayout_w_in(w_in[l])
        wq2, wk, wv, e2 = _relayout_mla(mla_w_uq[l], mla_w_ukv[l])
        wai = jnp.stack([jnp.concatenate([_block_diag_heads(lru_w_a[l, d]), _block_diag_heads(lru_w_i[l, d])], axis=1)
                         for d in range(2)]).astype(BF16)
        bai = jnp.concatenate([lru_b_a[l], lru_b_i[l]], axis=1)[:, None, :]
        lru_args = (lru_conv_w[l], lru_conv_b[l][:, None, :], wai, bai, lru_lambda[l][:, None, :])
        w_out_b = w_out[l].astype(BF16)

        zf_c, lru_c, mla_c, u_c, sb_c = _in_projection(h_ctx, norm1_g[l], part(mc, 0), part(mc, 1), w_in_p, cs)
        zf_l, lru_l, mla_l, u_l, sb_l = _in_projection(h_lat, norm1_g[l], part(ml, 0), part(ml, 1), w_in_p, cs)

        h0 = jnp.zeros((B, 2, 1, GROUP_W), F32)
        b_ctx, s_ctx = _rglru(lru_c, h0, *lru_args)
        b_lat, _ = _rglru(lru_l, s_ctx, *lru_args)

        q_c, k_c, v_c = _mla_prep(mla_c, mla_q_norm_g[l], mla_kv_norm_g[l], wq2, wk, wv, e2, cos_ctx, sin_ctx)
        q_l, k_l, v_l = _mla_prep(mla_l, mla_q_norm_g[l], mla_kv_norm_g[l], wq2, wk, wv, e2, cos_lat, sin_lat)
        k_all = jnp.concatenate([k_c, k_l], axis=2)
        v_all = jnp.concatenate([v_c, v_l], axis=1)
        c_lat = _attention(q_l, k_all, v_all)

        a_lat = _fourier_positions(zf_l, cn_lat, sn_lat)
        d_lat = _short_conv(u_l, sb_l, sc_conv_w[l], sc_conv_b[l])

        hn_lat, f_lat, ti_lat, tg_lat = _out_projection(
            a_lat, b_lat, c_lat, d_lat, h_lat, w_out_b, part(ml, 2), norm2_g[l], part(ml, 3), part(ml, 4),
            router_w[l], router_b[l])

        moe_w = (exp_w_gu, exp_b_gu, exp_w_down, exp_b_down, l)
        tokens_on_lanes = lambda u: jnp.transpose(u, (1, 0, 2)).reshape(8, -1)
        rows = lambda u, w: u.reshape(-1, w)
        g2_lat = part(ml, 5)
        if with_ctx_out:
            a_ctx = _fourier_positions(zf_c, cn_ctx, sn_ctx)
            c_ctx_out = _attention(q_c, k_c, v_c)
            d_ctx = _short_conv(u_c, sb_c, sc_conv_w[l], sc_conv_b[l])
            hn_ctx, f_ctx, ti_ctx, tg_ctx = _out_projection(
                a_ctx, b_ctx, c_ctx_out, d_ctx, h_ctx, w_out_b, part(mc, 2), norm2_g[l], part(mc, 3), part(mc, 4),
                router_w[l], router_b[l])
            S = n_ctx + n
            g2_rows = jnp.concatenate([jnp.broadcast_to(part(mc, 5), (B, n_ctx // tm, D)),
                                       jnp.broadcast_to(g2_lat, (B, n // tm, D))], axis=1)
            cat_t = lambda u, v: tokens_on_lanes(jnp.concatenate([u, v], axis=2))
            h_ctx, h_lat = _moe([rows(f_ctx, 128), rows(f_lat, 128)], [rows(hn_ctx, D), rows(hn_lat, D)],
                                cat_t(ti_ctx, ti_lat), cat_t(tg_ctx, tg_lat),
                                g2_rows.reshape(B * S // tm, 1, D), *moe_w, final_g, tm, n_ctx // tm, n // tm, False)
            h_ctx, h_lat = h_ctx.reshape(B, n_ctx, D), h_lat.reshape(B, n, D)
        else:
            g2_rows = jnp.broadcast_to(g2_lat, (B, n // tm, D)).reshape(B * n // tm, 1, D)
            (h_lat,) = _moe([rows(f_lat, 128)], [rows(hn_lat, D)],
                            tokens_on_lanes(ti_lat), tokens_on_lanes(tg_lat), g2_rows,
                            *moe_w, final_g, tm, 0, n // tm, True)
            h_lat = h_lat.reshape(B, n, D)

    return h_lat
```

```python
import functools
import math

import numpy as np
import jax
import jax.numpy as jnp
from jax import lax
from jax.experimental import pallas as pl
from jax.experimental.pallas import tpu as pltpu

F32 = jnp.float32
BF16 = jnp.bfloat16

D_MODEL = 1024
GRID_W = 64
GROUP_W = 256
EPS = 1e-6
FNET_CH = 64
LRU_HEADS = 4
LRU_C = 8.0
MLA_HEADS = 4
MLA_NOPE = 64
MLA_ROPE = 32
MLA_V = 64
MLA_Q_LORA = 256
MLA_KV_LORA = 128
ROPE_BASE = 10000.0
ATTN_SCALE = (MLA_NOPE + MLA_ROPE) ** -0.5
LOG2_E = math.log2(math.e)
N_EXPERTS = 32
TOP_K = 4
SWIGLU_ALPHA = 1.702
SWIGLU_LIMIT = 7.0

HEAD_PAD = 128
V_ROWS = 80
P_PAD = 2048
MOE_BM = 768
FFN_ROWS = 256
SCAN_GROUP = 8
VMEM_LIMIT = 56 * 1024 * 1024


def _cparams(n_axes, vmem=None):
    return pltpu.CompilerParams(dimension_semantics=("arbitrary",) * n_axes,
                                vmem_limit_bytes=vmem or VMEM_LIMIT)


def _bdot(a, b):
    return jnp.dot(a.astype(BF16), b.astype(BF16), preferred_element_type=F32)


def _mod_kernel(c_ref, w_ref, b_ref, o_ref):
    c = c_ref[...]
    s = c * jax.nn.sigmoid(c)
    o_ref[0] = _bdot(s, w_ref[0]) + b_ref[0]


def _modulation(c_all, mod_w, mod_b):
    L, D, N = mod_w.shape
    R = c_all.shape[0]
    tn = 1024
    return pl.pallas_call(
        _mod_kernel,
        out_shape=jax.ShapeDtypeStruct((L, R, N), F32),
        grid=(L, N // tn),
        in_specs=[pl.BlockSpec((R, D), lambda l, j: (0, 0)),
                  pl.BlockSpec((1, D, tn), lambda l, j: (l, 0, j)),
                  pl.BlockSpec((1, 1, tn), lambda l, j: (l, 0, j))],
        out_specs=pl.BlockSpec((1, R, tn), lambda l, j: (l, 0, j)),
        compiler_params=_cparams(2),
        name="adaln_mod",
    )(c_all, mod_w, mod_b.reshape(L, 1, N))


def _inproj_kernel(h_ref, g_ref, sh_ref, sc_ref, w_ref, cs_ref, zf_ref, lru_ref, mla_ref, u_ref, sb_ref):
    x = h_ref[0]
    y = x * lax.rsqrt(jnp.mean(x * x, axis=-1, keepdims=True) + EPS) * g_ref[...]
    a = y * (1.0 + sc_ref[0]) + sh_ref[0]
    p = _bdot(a, w_ref[...])
    zf_ref[0] = _bdot(p[:, 0:256], cs_ref[...]).astype(BF16)
    lru_ref[0] = p[:, 256:768]
    mla_ref[0] = p[:, 768:1280]
    u_ref[0] = p[:, 1280:1536] * p[:, 1792:2048]
    sb_ref[0] = p[:, 1536:1792]


def _in_projection(h, g, sh, sc, w_in_p, cs):
    B, n, D = h.shape
    tm = min(512, n)
    tile = lambda w: pl.BlockSpec((1, tm, w), lambda b, i: (b, i, 0))
    vec = pl.BlockSpec((1, 1, D), lambda b, i: (b, 0, 0))
    return pl.pallas_call(
        _inproj_kernel,
        out_shape=(jax.ShapeDtypeStruct((B, n, 512), BF16),
                   jax.ShapeDtypeStruct((B, n, 512), F32),
                   jax.ShapeDtypeStruct((B, n, 512), F32),
                   jax.ShapeDtypeStruct((B, n, 256), F32),
                   jax.ShapeDtypeStruct((B, n, 256), F32)),
        grid=(B, n // tm),
        in_specs=[tile(D),
                  pl.BlockSpec((1, D), lambda b, i: (0, 0)),
                  vec, vec,
                  pl.BlockSpec((D, P_PAD), lambda b, i: (0, 0)),
                  pl.BlockSpec((256, 512), lambda b, i: (0, 0))],
        out_specs=(tile(512), tile(512), tile(512), tile(256), tile(256)),
        compiler_params=_cparams(2),
        name="norm_inproj",
    )(h, g.reshape(1, D), sh, sc, w_in_p, cs)


def _fnet_kernel(c_ref, s_ref, z_ref, o_ref):
    z = z_ref[0]
    acc = jnp.dot(c_ref[...], z[:, 0:256], preferred_element_type=F32)
    acc = acc - jnp.dot(s_ref[...], z[:, 256:512], preferred_element_type=F32)
    o_ref[0] = acc.astype(BF16)


def _fourier_positions(zf, cn, sn):
    B, n, _ = zf.shape
    tm = min(512, n)
    return pl.pallas_call(
        _fnet_kernel,
        out_shape=jax.ShapeDtypeStruct((B, n, 256), BF16),
        grid=(n // tm, B),
        in_specs=[pl.BlockSpec((tm, n), lambda i, b: (i, 0)),
                  pl.BlockSpec((tm, n), lambda i, b: (i, 0)),
                  pl.BlockSpec((1, n, 512), lambda i, b: (b, 0, 0))],
        out_specs=pl.BlockSpec((1, tm, 256), lambda i, b: (b, i, 0)),
        compiler_params=_cparams(2),
        name="fourier_positions",
    )(cn, sn, zf)


def _shift_rows(xcat, off, tc):
    if off == 0:
        return xcat[8:8 + tc]
    rolled = pltpu.roll(xcat, (tc + 16) - (8 + off), axis=0)
    return rolled[0:tc]


def _load_halo(ref, col0, width, start, tc, n):
    main = ref[0, pl.ds(start, tc), col0:col0 + width]
    ps = jnp.maximum(start - 8, 0)
    prev = ref[0, pl.ds(pl.multiple_of(ps, 8), 8), col0:col0 + width]
    prev = jnp.where(start > 0, prev, 0.0)
    ns = jnp.minimum(start + tc, n - 8)
    nxt = ref[0, pl.ds(pl.multiple_of(ns, 8), 8), col0:col0 + width]
    nxt = jnp.where(start + tc < n, nxt, 0.0)
    return jnp.concatenate([prev, main, nxt], axis=0)


def _lru_kernel(x_ref, h0_ref, cw_ref, cb_ref, wai_ref, bai_ref, lam_ref, o_ref, hf_ref, hs_ref, *, tc, n):
    nc = n // tc
    in_group = lax.broadcasted_iota(jnp.int32, (tc, GROUP_W), 0) % SCAN_GROUP

    def direction(d, reverse):
        left = 1 if reverse else 2
        cw = cw_ref[d]
        cb = cb_ref[d]
        wai = wai_ref[d]
        bai = bai_ref[d]
        sp = jax.nn.softplus(-lam_ref[d])

        def chunk(ci, carry):
            c = (nc - 1 - ci) if reverse else ci
            start = pl.multiple_of(c * tc, tc)
            xcat = _load_halo(x_ref, 0, GROUP_W, start, tc, n)
            xc = cb
            for k in range(4):
                xc = xc + cw[k:k + 1, :] * _shift_rows(xcat, k - left, tc)
            gates = jax.nn.sigmoid(_bdot(xc, wai) + bai)
            r = gates[:, 0:GROUP_W]
            gi = gates[:, GROUP_W:2 * GROUP_W]
            a = jnp.exp(-LRU_C * r * sp)
            b = jnp.sqrt(jnp.maximum(1.0 - a * a, 0.0)) * (gi * xc)
            s = 1
            while s < SCAN_GROUP:
                if reverse:
                    a_sh = pltpu.roll(a, tc - s, axis=0)
                    b_sh = pltpu.roll(b, tc - s, axis=0)
                    keep = in_group < (SCAN_GROUP - s)
                else:
                    a_sh = pltpu.roll(a, s, axis=0)
                    b_sh = pltpu.roll(b, s, axis=0)
                    keep = in_group >= s
                b = jnp.where(keep, a * b_sh, 0.0) + b
                a = jnp.where(keep, a * a_sh, a)
                s *= 2
            groups = range(tc // SCAN_GROUP)
            pieces = [None] * len(groups)
            state = carry
            for gi in (reversed(groups) if reverse else groups):
                rows_g = slice(gi * SCAN_GROUP, (gi + 1) * SCAN_GROUP)
                hg = a[rows_g] * state + b[rows_g]
                pieces[gi] = hg
                state = hg[0:1, :] if reverse else hg[SCAN_GROUP - 1:SCAN_GROUP, :]
            h = jnp.concatenate(pieces, axis=0)
            if reverse:
                gate = x_ref[0, pl.ds(start, tc), GROUP_W:2 * GROUP_W]
                tot = hs_ref[pl.ds(start, tc), :] + h
                o_ref[0, pl.ds(start, tc), :] = (jax.nn.gelu(gate) * tot).astype(o_ref.dtype)
                return state
            hs_ref[pl.ds(start, tc), :] = h
            return state

        return lax.fori_loop(0, nc, chunk, h0_ref[0, d])

    hf_ref[0, 0] = direction(0, False)
    hf_ref[0, 1] = direction(1, True)


def _rglru(lru, h0, cw, cb, wai, bai, lam):
    B, n, _ = lru.shape
    tc = min(256, n)
    full = lambda *s: pl.BlockSpec(s, lambda b: (0,) * len(s))
    kern = functools.partial(_lru_kernel, tc=tc, n=n)
    return pl.pallas_call(
        kern,
        out_shape=(jax.ShapeDtypeStruct((B, n, GROUP_W), BF16),
                   jax.ShapeDtypeStruct((B, 2, 1, GROUP_W), F32)),
        grid=(B,),
        in_specs=[pl.BlockSpec((1, n, 512), lambda b: (b, 0, 0)),
                  pl.BlockSpec((1, 2, 1, GROUP_W), lambda b: (b, 0, 0, 0)),
                  full(2, 4, GROUP_W), full(2, 1, GROUP_W), full(2, GROUP_W, 2 * GROUP_W),
                  full(2, 1, 2 * GROUP_W), full(2, 1, GROUP_W)],
        out_specs=(pl.BlockSpec((1, n, GROUP_W), lambda b: (b, 0, 0)),
                   pl.BlockSpec((1, 2, 1, GROUP_W), lambda b: (b, 0, 0, 0))),
        scratch_shapes=[pltpu.VMEM((n, GROUP_W), F32)],
        compiler_params=_cparams(1),
        name="rglru",
    )(lru, h0, cw, cb, wai, bai, lam)


def _sconv_kernel(u_ref, sb_ref, w_ref, b_ref, o_ref, *, tc, n):
    w = w_ref[...]
    bias = b_ref[...]

    def chunk(c, _):
        start = pl.multiple_of(c * tc, tc)
        ucat = _load_halo(u_ref, 0, GROUP_W, start, tc, n)
        y = bias
        for k in range(3):
            y = y + w[k:k + 1, :] * _shift_rows(ucat, k - 1, tc)
        o_ref[0, pl.ds(start, tc), :] = (sb_ref[0, pl.ds(start, tc), :] * y).astype(o_ref.dtype)
        return 0

    lax.fori_loop(0, n // tc, chunk, 0)


def _short_conv(u, sb, w, b):
    B, n, W = u.shape
    tc = min(256, n)
    seq = pl.BlockSpec((1, n, W), lambda i: (i, 0, 0))
    return pl.pallas_call(
        functools.partial(_sconv_kernel, tc=tc, n=n),
        out_shape=jax.ShapeDtypeStruct((B, n, W), BF16),
        grid=(B,),
        in_specs=[seq, seq, pl.BlockSpec((3, W), lambda i: (0, 0)), pl.BlockSpec((1, W), lambda i: (0, 0))],
        out_specs=seq,
        compiler_params=_cparams(1),
        name="short_conv",
    )(u, sb, w, b.reshape(1, W))


def _mla_prep_kernel(x_ref, qg_ref, kg_ref, wq_ref, wk_ref, wv_ref, e2_ref, cos_ref, sin_ref,
                     q_ref, k_ref, v_ref):
    x = x_ref[0]
    cq = x[:, 0:256]
    ckv = x[:, 256:384]
    kr = x[:, 384:448]
    cqn = cq * lax.rsqrt(jnp.mean(cq * cq, axis=-1, keepdims=True) + EPS) * qg_ref[...]
    ckvn = ckv * lax.rsqrt(jnp.mean(ckv * ckv, axis=-1, keepdims=True) + EPS) * kg_ref[...]
    cos = cos_ref[...]
    sin = sin_ref[...]
    q2 = _bdot(cqn, wq_ref[...])
    q = (q2[:, 0:512] * cos + q2[:, 512:1024] * sin) * (ATTN_SCALE * LOG2_E)
    kr2 = _bdot(kr, e2_ref[...])
    ckvb = ckvn.astype(BF16)
    k = _bdot(ckvb, wk_ref[...]) + kr2[:, 0:512] * cos + kr2[:, 512:1024] * sin
    vt = lax.dot_general(wv_ref[...], ckvb, (((1,), (1,)), ((), ())), preferred_element_type=F32)
    tm = x.shape[0]
    ones_rows = jnp.where(lax.broadcasted_iota(jnp.int32, (V_ROWS - MLA_V, tm), 0) == 0, 1.0, 0.0)
    pieces = []
    for h in range(MLA_HEADS):
        q_ref[0, h] = q[:, h * HEAD_PAD:(h + 1) * HEAD_PAD].astype(BF16)
        k_ref[0, h] = k[:, h * HEAD_PAD:(h + 1) * HEAD_PAD].astype(BF16)
        pieces += [vt[h * MLA_V:(h + 1) * MLA_V], ones_rows]
    v_ref[0, 0] = jnp.concatenate(pieces, axis=0).astype(BF16)


def _mla_prep(mla, qg, kg, wq2, wk, wv, e2, cos, sin):
    B, n, _ = mla.shape
    tm = min(256, n)
    HW = MLA_HEADS * HEAD_PAD
    full = lambda *s: pl.BlockSpec(s, lambda b, i: (0,) * len(s))
    return pl.pallas_call(
        _mla_prep_kernel,
        out_shape=(jax.ShapeDtypeStruct((B, MLA_HEADS, n, HEAD_PAD), BF16),
                   jax.ShapeDtypeStruct((B, MLA_HEADS, n, HEAD_PAD), BF16),
                   jax.ShapeDtypeStruct((B, n // tm, MLA_HEADS * V_ROWS, tm), BF16)),
        grid=(B, n // tm),
        in_specs=[pl.BlockSpec((1, tm, 512), lambda b, i: (b, i, 0)),
                  full(1, MLA_Q_LORA), full(1, MLA_KV_LORA),
                  full(MLA_Q_LORA, 2 * HW), full(MLA_KV_LORA, HW), full(MLA_HEADS * MLA_V, MLA_KV_LORA),
                  full(2 * MLA_ROPE, 2 * HW),
                  pl.BlockSpec((tm, HW), lambda b, i: (i, 0)),
                  pl.BlockSpec((tm, HW), lambda b, i: (i, 0))],
        out_specs=(pl.BlockSpec((1, MLA_HEADS, tm, HEAD_PAD), lambda b, i: (b, 0, i, 0)),
                   pl.BlockSpec((1, MLA_HEADS, tm, HEAD_PAD), lambda b, i: (b, 0, i, 0)),
                   pl.BlockSpec((1, 1, MLA_HEADS * V_ROWS, tm), lambda b, i: (b, i, 0, 0))),
        compiler_params=_cparams(2),
        name="mla_prep",
    )(mla, qg.reshape(1, -1), kg.reshape(1, -1), wq2, wk, wv, e2, cos, sin)


def _attn_kernel(q_ref, k_ref, vt_ref, o_ref, st_a, st_b, p_a, p_b, *, tk, n_kt):
    tq = q_ref.shape[2]
    H = MLA_HEADS
    qs = [q_ref[0, h] for h in range(H)]

    def scores(j, h):
        kt = k_ref[0, h, pl.ds(pl.multiple_of(j * tk, tk), tk), :]
        return lax.dot_general(kt, qs[h], (((1,), (1,)), ((), ())), preferred_element_type=F32)

    def weighted_values(j, h, p):
        vt = vt_ref[0, j, h * V_ROWS:(h + 1) * V_ROWS, :]
        return jnp.dot(vt, p, preferred_element_type=F32)

    def step(j, st_cur, st_next, p_prev, p_cur, carry):
        a_prev, m, acc = carry
        jn = jnp.minimum(j + 1, n_kt - 1)
        for h in range(H):
            st_next[h] = scores(jn, h)
        pv = [weighted_values(jnp.maximum(j - 1, 0), h, p_prev[h]) for h in range(H)]
        m_new, alpha = [], []
        for h in range(H):
            st = st_cur[h]
            mh = jnp.maximum(m[h], jnp.max(st, axis=0, keepdims=True))
            alpha.append(jnp.exp2(m[h] - mh))
            m_new.append(mh)
            p_cur[h] = jnp.exp2((st - mh).astype(BF16))
        acc_new = [a_prev[h] * acc[h] + pv[h] for h in range(H)]
        return (tuple(alpha), tuple(m_new), tuple(acc_new))

    even = lambda j, c: step(j, st_a, st_b, p_b, p_a, c)
    odd = lambda j, c: step(j, st_b, st_a, p_a, p_b, c)

    for h in range(H):
        st_a[h] = scores(0, h)
    p_b[...] = jnp.zeros(p_b.shape, BF16)
    per_head = lambda f: tuple(f() for _ in range(H))
    carry = (per_head(lambda: jnp.ones((1, tq), F32)),
             per_head(lambda: jnp.full((1, tq), -jnp.inf, F32)),
             per_head(lambda: jnp.zeros((V_ROWS, tq), F32)))
    carry = lax.fori_loop(0, n_kt // 2, lambda i, c: odd(2 * i + 1, even(2 * i, c)), carry)
    if n_kt % 2:
        carry = even(n_kt - 1, carry)
    p_last = p_a if n_kt % 2 else p_b
    a_last, _, acc = carry
    outs = []
    for h in range(H):
        tot = a_last[h] * acc[h] + weighted_values(n_kt - 1, h, p_last[h])
        outs.append(tot[0:MLA_V] / tot[MLA_V:MLA_V + 1])
    o_ref[0] = jnp.concatenate(outs, axis=0).T.astype(o_ref.dtype)


def _attention(q, k, vt):
    B, H, n, _ = q.shape
    M = k.shape[2]
    n_kt, tk = vt.shape[1], vt.shape[3]
    tq = min(256, n)
    return pl.pallas_call(
        functools.partial(_attn_kernel, tk=tk, n_kt=n_kt),
        out_shape=jax.ShapeDtypeStruct((B, n, H * MLA_V), BF16),
        grid=(B, n // tq),
        in_specs=[pl.BlockSpec((1, H, tq, HEAD_PAD), lambda b, i: (b, 0, i, 0)),
                  pl.BlockSpec((1, H, M, HEAD_PAD), lambda b, i: (b, 0, 0, 0)),
                  pl.BlockSpec((1, n_kt, H * V_ROWS, tk), lambda b, i: (b, 0, 0, 0))],
        out_specs=pl.BlockSpec((1, tq, H * MLA_V), lambda b, i: (b, i, 0)),
        scratch_shapes=[pltpu.VMEM((H, tk, tq), F32), pltpu.VMEM((H, tk, tq), F32),
                        pltpu.VMEM((H, tk, tq), BF16), pltpu.VMEM((H, tk, tq), BF16)],
        compiler_params=_cparams(2),
        name="mla_attention",
    )(q, k, vt)


ROW_TILE = 8


def _store_rows_contiguous(ref, lead, x):
    rows = x.shape[0]
    for j in range(ROW_TILE):
        ref[lead + (pl.ds(j, rows, stride=ROW_TILE), slice(None))] = x[:, j * 128:(j + 1) * 128]


def _load_rows_contiguous(ref, lead, rows, dtype):
    return jnp.concatenate([ref[lead + (pl.ds(j, rows, stride=ROW_TILE), slice(None))].astype(dtype)
                            for j in range(ROW_TILE)], axis=-1)


def _outproj_kernel(a_ref, b_ref, c_ref, d_ref, h_ref, w_ref, g1_ref, ng_ref, sh_ref, sc_ref, rw_ref, rb_ref,
                    hn_ref, f_ref, ti_ref, tg_ref):
    mixed = jnp.concatenate([a_ref[0], b_ref[0], c_ref[0], d_ref[0]], axis=-1)
    hn = h_ref[0] + g1_ref[0] * jnp.dot(mixed, w_ref[...], preferred_element_type=F32)
    hn_ref[0] = hn
    y = hn * lax.rsqrt(jnp.mean(hn * hn, axis=-1, keepdims=True) + EPS) * ng_ref[...]
    f = y * (1.0 + sc_ref[0]) + sh_ref[0]
    _store_rows_contiguous(f_ref, (0,), f)
    nt = (((1,), (1,)), ((), ()))
    rw = rw_ref[...]
    rw_hi = rw.astype(BF16)
    rw_lo = (rw - rw_hi.astype(F32)).astype(BF16)
    f_hi = f.astype(BF16)
    f_lo = (f - f_hi.astype(F32)).astype(BF16)
    logits = (lax.dot_general(rw_hi, f_hi, nt, preferred_element_type=F32)
              + (lax.dot_general(rw_hi, f_lo, nt, preferred_element_type=F32)
                 + lax.dot_general(rw_lo, f_hi, nt, preferred_element_type=F32))) + rb_ref[...]
    tm = logits.shape[1]
    sub = lax.broadcasted_iota(jnp.int32, (N_EXPERTS, tm), 0)
    row = lax.broadcasted_iota(jnp.int32, (8, tm), 0)
    top_v = jnp.full((8, tm), -jnp.inf, F32)
    top_i = jnp.zeros((8, tm), jnp.int32)
    cur = logits
    m0 = None
    for kk in range(TOP_K):
        m = jnp.max(cur, axis=0, keepdims=True)
        sel = jnp.min(jnp.where(cur == m, sub, N_EXPERTS), axis=0, keepdims=True)
        if kk == 0:
            m0 = m
        top_v = jnp.where(row == kk, m, top_v)
        top_i = jnp.where(row == kk, sel, top_i)
        cur = jnp.where(sub == sel, -jnp.inf, cur)
    e = jnp.where(row < TOP_K, jnp.exp(top_v - m0), 0.0)
    ti_ref[0] = top_i
    tg_ref[0] = e / jnp.sum(e, axis=0, keepdims=True)


def _out_projection(a, b, c, d, h, w_out, g1, ng, sh, sc, rw, rb):
    B, n, D = h.shape
    tm = min(512, n)
    tile = lambda w: pl.BlockSpec((1, tm, w), lambda bb, i: (bb, i, 0))
    vec = pl.BlockSpec((1, 1, D), lambda bb, i: (bb, 0, 0))
    full = lambda *s: pl.BlockSpec(s, lambda bb, i: (0,) * len(s))
    return pl.pallas_call(
        _outproj_kernel,
        out_shape=(jax.ShapeDtypeStruct((B, n, D), F32),
                   jax.ShapeDtypeStruct((B, n * ROW_TILE, 128), F32),
                   jax.ShapeDtypeStruct((B, 8, n), jnp.int32),
                   jax.ShapeDtypeStruct((B, 8, n), F32)),
        grid=(B, n // tm),
        in_specs=[tile(256), tile(256), tile(256), tile(256), tile(D),
                  full(D, D), vec, full(1, D), vec, vec, full(N_EXPERTS, D), full(N_EXPERTS, 1)],
        out_specs=(tile(D), pl.BlockSpec((1, tm * ROW_TILE, 128), lambda bb, i: (bb, i, 0)),
                   pl.BlockSpec((1, 8, tm), lambda bb, i: (bb, 0, i)),
                   pl.BlockSpec((1, 8, tm), lambda bb, i: (bb, 0, i))),
        compiler_params=_cparams(2),
        name="outproj_router",
    )(a, b, c, d, h, w_out, g1, ng.reshape(1, D), sh, sc, rw.T, rb.reshape(N_EXPERTS, 1))


def _rank_kernel(ti_ref, rank_ref, cnt_ref, carry_ref, *, tm):
    @pl.when(pl.program_id(0) == 0)
    def _():
        carry_ref[...] = jnp.zeros_like(carry_ref)

    ti = ti_ref[...]
    sub = lax.broadcasted_iota(jnp.int32, (N_EXPERTS, tm), 0)
    hot = jnp.zeros((N_EXPERTS, tm), F32)
    for kk in range(TOP_K):
        hot = hot + jnp.where(sub == ti[kk:kk + 1, :], 1.0, 0.0)
    r = lax.broadcasted_iota(jnp.int32, (tm, tm), 0)
    c = lax.broadcasted_iota(jnp.int32, (tm, tm), 1)
    earlier = jnp.where(r < c, 1.0, 0.0).astype(BF16)
    carry = carry_ref[:, 0:1]
    before = jnp.dot(hot.astype(BF16), earlier, preferred_element_type=F32) + carry
    row = lax.broadcasted_iota(jnp.int32, (8, tm), 0)
    out = jnp.zeros((8, tm), jnp.int32)
    for kk in range(TOP_K):
        rk = jnp.sum(jnp.where(sub == ti[kk:kk + 1, :], before, 0.0), axis=0, keepdims=True)
        out = jnp.where(row == kk, rk.astype(jnp.int32), out)
    rank_ref[...] = out
    total = carry + jnp.sum(hot, axis=1, keepdims=True)
    carry_ref[...] = jnp.broadcast_to(total, carry_ref.shape)
    cnt_ref[...] = jnp.broadcast_to(total, cnt_ref.shape).astype(jnp.int32)


def _expert_ranks(top_i_t):
    T = top_i_t.shape[1]
    tm = 512 if T % 512 == 0 else 256
    return pl.pallas_call(
        functools.partial(_rank_kernel, tm=tm),
        out_shape=(jax.ShapeDtypeStruct((8, T), jnp.int32), jax.ShapeDtypeStruct((N_EXPERTS, 128), jnp.int32)),
        grid=(T // tm,),
        in_specs=[pl.BlockSpec((8, tm), lambda i: (0, i))],
        out_specs=(pl.BlockSpec((8, tm), lambda i: (0, i)), pl.BlockSpec((N_EXPERTS, 128), lambda i: (0, 0))),
        scratch_shapes=[pltpu.VMEM((N_EXPERTS, 128), F32)],
        compiler_params=_cparams(1),
        name="moe_rank",
    )(top_i_t)


def _stream_maps(nc, nl):
    per_batch = nc + nl
    ctx_map = lambda i, *_: ((i // per_batch) * nc + jnp.minimum(i % per_batch, nc - 1), 0)
    lat_map = lambda i, *_: ((i // per_batch) * nl + jnp.maximum(i % per_batch - nc, 0), 0)
    return [ctx_map, lat_map] if nc > 0 else [lat_map]


def _dispatch_kernel(cnt_ref, ps_ref, pe_ref, pos_ref, *refs, tm, n_slots, nc, nl):
    n_in = 2 if nc > 0 else 1
    f_refs = refs[:n_in]
    xs_ref, buf_ref, zrow_ref, sem_ref, zsem_ref = refs[n_in:]
    i = pl.program_id(0)
    nt = pl.num_programs(0)
    copies = tm * TOP_K
    zslots = zrow_ref.shape[0] // ROW_TILE

    def tile_rows(first_row, n_rows):
        return pl.ds(pl.multiple_of(first_row * ROW_TILE, ROW_TILE), n_rows * ROW_TILE)

    def zero_fill(lo, hi, start):
        def chunk(off, size):
            cp = pltpu.make_async_copy(zrow_ref.at[tile_rows(0, size)], xs_ref.at[tile_rows(off, size)], zsem_ref)
            cp.start() if start else cp.wait()

        n = hi - lo

        def whole(j, _):
            chunk(lo + j * zslots, zslots)
            return 0

        lax.fori_loop(0, n // zslots, whole, 0)
        rem_lo = lo + (n // zslots) * zslots
        rem = n % zslots
        size = zslots // 2
        while size >= 1:
            @pl.when((rem & size) != 0)
            def _(size=size):
                chunk(rem_lo + (rem & ~(2 * size - 1)), size)
            size //= 2

    @pl.when(i == 0)
    def _():
        zrow_ref[...] = jnp.zeros_like(zrow_ref)
        for start in (True, False):
            for e in range(N_EXPERTS):
                zero_fill(ps_ref[e] + cnt_ref[e], pe_ref[e], start)
            zero_fill(pe_ref[N_EXPERTS - 1], n_slots, start)

    slot = i % 2
    if nc > 0:
        j = i % (nc + nl)

        @pl.when(j < nc)
        def _():
            buf_ref[slot] = f_refs[0][...]

        @pl.when(j >= nc)
        def _():
            buf_ref[slot] = f_refs[1][...]
    else:
        buf_ref[slot] = f_refs[0][...]

    def issue(t, _):
        for kk in range(TOP_K):
            pltpu.make_async_copy(buf_ref.at[slot, tile_rows(t, 1)],
                                  xs_ref.at[tile_rows(pos_ref[0, 0, t * TOP_K + kk], 1)],
                                  sem_ref.at[slot]).start(priority=kk % 2)
        return 0

    lax.fori_loop(0, tm, issue, 0, unroll=2)

    def wait_rows(s):
        pltpu.make_async_copy(xs_ref.at[tile_rows(0, copies)], xs_ref.at[tile_rows(0, copies)], sem_ref.at[s]).wait()

    @pl.when(i > 0)
    def _():
        wait_rows(1 - slot)

    @pl.when(i == nt - 1)
    def _():
        wait_rows(slot)


def _dispatch(fs, pos, counts, pad_start, pad_end, n_slots, tm, nc, nl):
    nt = pos.shape[0] // (tm * TOP_K)
    grid_spec = pltpu.PrefetchScalarGridSpec(
        num_scalar_prefetch=3,
        grid=(nt,),
        in_specs=[pl.BlockSpec((1, 1, tm * TOP_K), lambda i, *_: (i, 0, 0), memory_space=pltpu.SMEM)]
        + [pl.BlockSpec((tm * ROW_TILE, 128), m) for m in _stream_maps(nc, nl)],
        out_specs=pl.BlockSpec(memory_space=pl.ANY),
        scratch_shapes=[pltpu.VMEM((2, tm * ROW_TILE, 128), F32), pltpu.VMEM((256 * ROW_TILE, 128), F32),
                        pltpu.SemaphoreType.DMA((2,)), pltpu.SemaphoreType.DMA],
    )
    return pl.pallas_call(
        functools.partial(_dispatch_kernel, tm=tm, n_slots=n_slots, nc=nc, nl=nl),
        out_shape=jax.ShapeDtypeStruct((n_slots * ROW_TILE, 128), F32),
        grid_spec=grid_spec,
        compiler_params=_cparams(1),
        name="moe_dispatch",
    )(counts, pad_start, pad_end, pos.reshape(nt, 1, tm * TOP_K), *fs)


def _ffn_kernel(be_ref, bv_ref, x_ref, wgu_ref, bgu_ref, wd_ref, bd_ref, y_ref, wgu_s, wd_s, *, bm):
    i = pl.program_id(0)
    valid = bv_ref[i]

    def experts_rows(rows):
        De = wd_ref.shape[2]
        x = _load_rows_contiguous(x_ref, (), rows, BF16)
        gu = jnp.dot(x, wgu_s[...], preferred_element_type=F32) + bgu_ref[0, 0]
        g = jnp.minimum(gu[:, 0:De], SWIGLU_LIMIT)
        u = jnp.clip(gu[:, De:2 * De], -SWIGLU_LIMIT, SWIGLU_LIMIT)
        act = (u + 1.0) * (g * jax.nn.sigmoid(SWIGLU_ALPHA * g))
        y = jnp.dot(act.astype(BF16), wd_s[...], preferred_element_type=F32) + bd_ref[0, 0]
        _store_rows_contiguous(y_ref, (), y)
        if rows < bm:
            y_ref[rows * ROW_TILE:bm * ROW_TILE, :] = jnp.zeros(((bm - rows) * ROW_TILE, 128), y_ref.dtype)

    @pl.when(valid > 0)
    def _():
        @pl.when((i == 0) | (be_ref[i] != be_ref[jnp.maximum(i - 1, 0)]))
        def _():
            wgu_s[...] = wgu_ref[0, 0].astype(BF16)
            wd_s[...] = wd_ref[0, 0].astype(BF16)

        for rows in range(FFN_ROWS, bm + 1, FFN_ROWS):
            pl.when((valid > rows - FFN_ROWS) & (valid <= rows))(functools.partial(experts_rows, rows))

    @pl.when(valid <= 0)
    def _():
        y_ref[...] = jnp.zeros_like(y_ref)


def _expert_ffn(xs, block_exp, block_valid, wgu, bgu, wd, bd, layer, bm):
    n_slots = xs.shape[0] // ROW_TILE
    L, E, D, G = wgu.shape
    nb = n_slots // bm
    wmap = lambda i, be, bv: (layer, be[i], 0, 0)
    rows_spec = pl.BlockSpec((bm * ROW_TILE, 128), lambda i, be, bv: (i, 0))
    grid_spec = pltpu.PrefetchScalarGridSpec(
        num_scalar_prefetch=2,
        grid=(nb,),
        in_specs=[rows_spec,
                  pl.BlockSpec((1, 1, D, G), wmap),
                  pl.BlockSpec((1, 1, 1, G), wmap),
                  pl.BlockSpec((1, 1, G // 2, D), wmap),
                  pl.BlockSpec((1, 1, 1, D), wmap)],
        out_specs=rows_spec,
        scratch_shapes=[pltpu.VMEM((D, G), BF16), pltpu.VMEM((G // 2, D), BF16)],
    )
    return pl.pallas_call(
        functools.partial(_ffn_kernel, bm=bm),
        out_shape=jax.ShapeDtypeStruct((n_slots * ROW_TILE, 128), F32),
        grid_spec=grid_spec,
        compiler_params=_cparams(1),
        name="moe_expert_ffn",
    )(block_exp, block_valid, xs, wgu, bgu.reshape(L, E, 1, G), wd, bd.reshape(L, E, 1, D))


def _combine_kernel(pos_ref, pos_next_ref, y_ref, *refs, tm, nc, nl, final_norm):
    n_s = 2 if nc > 0 else 1
    h_refs = refs[:n_s]
    g2_ref, gate_ref, fg_ref = refs[n_s:n_s + 3]
    o_refs = refs[n_s + 3:2 * n_s + 3]
    buf_ref, sem_ref = refs[2 * n_s + 3:]
    i = pl.program_id(0)
    nt = pl.num_programs(0)
    slot = i % 2

    def tile_rows(first_row):
        return pl.ds(pl.multiple_of(first_row * ROW_TILE, ROW_TILE), ROW_TILE)

    def gather_tile(p_ref, s):
        def issue(r, _):
            for kk in range(TOP_K):
                pltpu.make_async_copy(y_ref.at[tile_rows(p_ref[0, 0, r * TOP_K + kk])],
                                      buf_ref.at[s, kk, tile_rows(r)], sem_ref.at[s]).start(priority=kk % 2)
            return 0

        lax.fori_loop(0, tm, issue, 0, unroll=2)

    @pl.when(i == 0)
    def _():
        gather_tile(pos_ref, 0)

    @pl.when(i + 1 < nt)
    def _():
        gather_tile(pos_next_ref, 1 - slot)

    pltpu.make_async_copy(buf_ref.at[slot], buf_ref.at[slot], sem_ref.at[slot]).wait()
    gate = gate_ref[...]
    acc = gate[:, 0:1] * _load_rows_contiguous(buf_ref, (slot, 0), tm, F32)
    for kk in range(1, TOP_K):
        acc = acc + gate[:, kk:kk + 1] * _load_rows_contiguous(buf_ref, (slot, kk), tm, F32)
    moe = g2_ref[0] * acc

    def finish(h_ref, o_ref):
        out = h_ref[...] + moe
        if final_norm:
            out = out * lax.rsqrt(jnp.mean(out * out, axis=-1, keepdims=True) + EPS) * fg_ref[...]
        o_ref[...] = out

    if nc > 0:
        j = i % (nc + nl)
        pl.when(j < nc)(lambda: finish(h_refs[0], o_refs[0]))
        pl.when(j >= nc)(lambda: finish(h_refs[1], o_refs[1]))
    else:
        finish(h_refs[0], o_refs[0])


def _combine(y, pos, hs, g2_tiles, gates, final_g, tm, nc, nl, final_norm):
    D = hs[0].shape[1]
    nt = pos.shape[0] // (tm * TOP_K)
    maps = _stream_maps(nc, nl)
    streams = [pl.BlockSpec((tm, D), m) for m in maps]
    pos3 = pos.reshape(nt, 1, tm * TOP_K)
    outs = pl.pallas_call(
        functools.partial(_combine_kernel, tm=tm, nc=nc, nl=nl, final_norm=final_norm),
        out_shape=[jax.ShapeDtypeStruct(h.shape, F32) for h in hs],
        grid=(nt,),
        in_specs=[pl.BlockSpec((1, 1, tm * TOP_K), lambda i: (i, 0, 0), memory_space=pltpu.SMEM),
                  pl.BlockSpec((1, 1, tm * TOP_K), lambda i: (jnp.minimum(i + 1, nt - 1), 0, 0),
                               memory_space=pltpu.SMEM),
                  pl.BlockSpec(memory_space=pl.ANY)]
        + streams
        + [pl.BlockSpec((1, 1, D), lambda i: (i, 0, 0)),
           pl.BlockSpec((tm, 8), lambda i: (i, 0)),
           pl.BlockSpec((1, D), lambda i: (0, 0))],
        out_specs=streams,
        scratch_shapes=[pltpu.VMEM((2, TOP_K, tm * ROW_TILE, 128), F32), pltpu.SemaphoreType.DMA((2,))],
        compiler_params=_cparams(1),
        name="moe_combine",
    )(pos3, pos3, y, *hs, g2_tiles, gates, final_g.reshape(1, D))
    return outs


def _slots_kernel(ps_ref, ti_ref, rank_ref, pos_ref):
    ti = ti_ref[...]
    start = jnp.zeros(ti.shape, jnp.int32)
    for e in range(N_EXPERTS):
        start = jnp.where(ti == e, ps_ref[e], start)
    pos_ref[...] = start + rank_ref[...]


def _slots(pad_start, top_i_t, rank_t):
    shape = top_i_t.shape
    whole = pl.BlockSpec(shape, lambda i, ps: (0, 0))
    return pl.pallas_call(
        _slots_kernel,
        out_shape=jax.ShapeDtypeStruct(shape, jnp.int32),
        grid_spec=pltpu.PrefetchScalarGridSpec(num_scalar_prefetch=1, grid=(1,), in_specs=[whole, whole],
                                               out_specs=whole),
        compiler_params=_cparams(1),
        name="moe_slots",
    )(pad_start, top_i_t, rank_t)


def _moe(fs, hs, top_i_t, top_g_t, g2_tiles, wgu, bgu, wd, bd, layer, final_g, tm, nc, nl, final_norm):
    T = top_i_t.shape[1]
    bm = MOE_BM
    n_asg = T * TOP_K
    rank_t, cnt = _expert_ranks(top_i_t)
    counts = cnt[:, 0]
    padded = (counts + bm - 1) // bm * bm
    pad_end = jnp.cumsum(padded).astype(jnp.int32)
    pad_start = pad_end - padded
    pos = _slots(pad_start, top_i_t, rank_t)[0:TOP_K].T.reshape(-1)
    nb = (n_asg + N_EXPERTS * (bm - 1) + bm - 1) // bm
    n_slots = nb * bm
    starts = jnp.arange(nb, dtype=jnp.int32) * bm
    block_exp = jnp.minimum(jnp.sum((pad_end[None, :] <= starts[:, None]).astype(jnp.int32), axis=1), N_EXPERTS - 1)
    block_valid = jnp.clip(pad_start[block_exp] + counts[block_exp] - starts, 0, bm)
    block_valid = jnp.where(starts < pad_end[-1], block_valid, 0).astype(jnp.int32)

    xs = _dispatch(fs, pos, counts, pad_start, pad_end, n_slots, tm, nc, nl)
    y = _expert_ffn(xs, block_exp, block_valid, wgu, bgu, wd, bd, layer, bm)
    return _combine(y, pos, hs, g2_tiles, top_g_t.T, final_g, tm, nc, nl, final_norm)


_ROPE_SWAP = np.concatenate([np.arange(8, 16), np.arange(0, 8), np.arange(24, 32), np.arange(16, 24)])


def _dft_tables(n):
    hi = n // 64
    j = jnp.arange(n, dtype=jnp.int32)[:, None]
    k = jnp.arange(64, dtype=jnp.int32)[None, :]
    k2 = jnp.arange(hi, dtype=jnp.int32)[None, :]
    alpha = ((j * k2) % hi).astype(F32) * (2.0 * math.pi / hi)
    beta = ((j * k) % n).astype(F32) * (2.0 * math.pi / n)
    ca, sa, cb, sb = lax.optimization_barrier((jnp.cos(alpha), jnp.sin(alpha), jnp.cos(beta), jnp.sin(beta)))
    ca, sa = ca[:, :, None], sa[:, :, None]
    cb, sb = cb[:, None, :], sb[:, None, :]
    scale = 1.0 / math.sqrt(n)
    cn = ((ca * cb - sa * sb) * scale).reshape(n, n).astype(BF16)
    sn = ((sa * cb + ca * sb) * scale).reshape(n, n).astype(BF16)
    return cn, sn


def _channel_dft():
    j = jnp.arange(GROUP_W, dtype=jnp.int32)
    same_group = (j[:, None] // FNET_CH) == (j[None, :] // FNET_CH)
    ang = ((j[:, None] * j[None, :]) % FNET_CH).astype(F32) * (2.0 * math.pi / FNET_CH)
    scale = 1.0 / math.sqrt(FNET_CH)
    c = jnp.where(same_group, jnp.cos(ang) * scale, 0.0)
    s = jnp.where(same_group, jnp.sin(ang) * scale, 0.0)
    return jnp.concatenate([c, s], axis=1).astype(BF16)


def _rope_tables(n, rotary):
    HW = MLA_HEADS * HEAD_PAD
    cos_blk = np.zeros((HEAD_PAD,), np.float32)
    cos_blk[0:MLA_NOPE + MLA_ROPE] = 1.0
    if not rotary:
        return (jnp.broadcast_to(jnp.asarray(np.tile(cos_blk, MLA_HEADS)), (n, HW)),
                jnp.zeros((n, HW), F32))
    half = MLA_ROPE // 2
    inv = ROPE_BASE ** (-jnp.arange(0, half, 2, dtype=F32) / half)
    t = jnp.arange(n, dtype=jnp.int32)
    row = (t // GRID_W).astype(F32)
    col = (t % GRID_W).astype(F32)
    ar = row[:, None] * inv[None, :]
    ac = col[:, None] * inv[None, :]
    cos32 = jnp.concatenate([jnp.cos(ar), jnp.cos(ar), jnp.cos(ac), jnp.cos(ac)], axis=1)
    sin32 = jnp.concatenate([-jnp.sin(ar), jnp.sin(ar), -jnp.sin(ac), jnp.sin(ac)], axis=1)
    ones = jnp.ones((n, MLA_NOPE), F32)
    zeros_n = jnp.zeros((n, MLA_NOPE), F32)
    zeros_p = jnp.zeros((n, HEAD_PAD - MLA_NOPE - MLA_ROPE), F32)
    cos_h = jnp.concatenate([ones, cos32, zeros_p], axis=1)
    sin_h = jnp.concatenate([zeros_n, sin32, zeros_p], axis=1)
    return jnp.tile(cos_h, (1, MLA_HEADS)), jnp.tile(sin_h, (1, MLA_HEADS))


def _relayout_w_in(w_in):
    D = w_in.shape[0]
    krope = w_in[:, 1152:1184]
    return jnp.concatenate([w_in[:, 0:1184], krope[:, _ROPE_SWAP], jnp.zeros((D, 64), w_in.dtype),
                            w_in[:, 1184:1952]], axis=1).astype(BF16)


def _relayout_mla(w_uq, w_ukv):
    dq = MLA_NOPE + MLA_ROPE
    q_main, q_swap, k_cols, v_cols = [], [], [], []
    zq = jnp.zeros((MLA_Q_LORA, HEAD_PAD - dq), w_uq.dtype)
    zn = jnp.zeros((MLA_Q_LORA, MLA_NOPE), w_uq.dtype)
    zk = jnp.zeros((MLA_KV_LORA, HEAD_PAD - MLA_NOPE), w_ukv.dtype)
    for h in range(MLA_HEADS):
        wq_h = w_uq[:, h * dq:(h + 1) * dq]
        rope = wq_h[:, MLA_NOPE:]
        q_main += [wq_h, zq]
        q_swap += [zn, rope[:, _ROPE_SWAP], zq]
        kv_h = w_ukv[:, h * (MLA_NOPE + MLA_V):(h + 1) * (MLA_NOPE + MLA_V)]
        k_cols += [kv_h[:, 0:MLA_NOPE], zk]
        v_cols += [kv_h[:, MLA_NOPE:]]
    wq2 = jnp.concatenate(q_main + q_swap, axis=1).astype(BF16)
    wk = jnp.concatenate(k_cols, axis=1).astype(BF16)
    wv = jnp.concatenate(v_cols, axis=1).T.astype(BF16)
    HW = MLA_HEADS * HEAD_PAD
    e2 = np.zeros((2 * MLA_ROPE, 2 * HW), np.float32)
    for h in range(MLA_HEADS):
        for jj in range(MLA_ROPE):
            e2[jj, h * HEAD_PAD + MLA_NOPE + jj] = 1.0
            e2[MLA_ROPE + jj, HW + h * HEAD_PAD + MLA_NOPE + jj] = 1.0
    return wq2, wk, wv, jnp.asarray(e2, BF16)


def _block_diag_heads(w):
    H, d, _ = w.shape
    eye = jnp.eye(H, dtype=w.dtype)
    return (eye[:, None, :, None] * w[:, :, None, :]).reshape(H * d, H * d)


def kernel(x, c, ctx, c_ctx, mod_w, mod_b, norm1_g, norm2_g, w_in, lru_conv_w, lru_conv_b, lru_w_a, lru_b_a, lru_w_i, lru_b_i, lru_lambda, mla_q_norm_g, mla_w_uq, mla_kv_norm_g, mla_w_ukv, sc_conv_w, sc_conv_b, w_out, router_w, router_b, exp_w_gu, exp_b_gu, exp_w_down, exp_b_down, final_g):
    B, n, D = x.shape
    n_ctx = ctx.shape[1]
    depth = mod_w.shape[0]
    tm = min(256, n, n_ctx)

    rows = 8 * ((B + 1 + 7) // 8)
    c_all = jnp.concatenate([c, c_ctx[None, :], jnp.zeros((rows - B - 1, D), F32)], axis=0)
    mod = _modulation(c_all, mod_w, mod_b)

    cs = _channel_dft()
    cn_lat, sn_lat = _dft_tables(n)
    cn_ctx, sn_ctx = _dft_tables(n_ctx)
    cos_lat, sin_lat = _rope_tables(n, True)
    cos_ctx, sin_ctx = _rope_tables(n_ctx, False)

    h_lat, h_ctx = x, ctx
    for l in range(depth):
        with_ctx_out = l < depth - 1
        ml = mod[l, 0:B].reshape(B, 1, 6, D)
        mc = jnp.broadcast_to(mod[l, B:B + 1].reshape(1, 1, 6, D), (B, 1, 6, D))
        part = lambda m, i: m[:, :, i, :]
        w_in_p = _relayout_w_in(w_in[l])
        wq2, wk, wv, e2 = _relayout_mla(mla_w_uq[l], mla_w_ukv[l])
        wai = jnp.stack([jnp.concatenate([_block_diag_heads(lru_w_a[l, d]), _block_diag_heads(lru_w_i[l, d])], axis=1)
                         for d in range(2)]).astype(BF16)
        bai = jnp.concatenate([lru_b_a[l], lru_b_i[l]], axis=1)[:, None, :]
        lru_args = (lru_conv_w[l], lru_conv_b[l][:, None, :], wai, bai, lru_lambda[l][:, None, :])
        w_out_b = w_out[l].astype(BF16)

        zf_c, lru_c, mla_c, u_c, sb_c = _in_projection(h_ctx, norm1_g[l], part(mc, 0), part(mc, 1), w_in_p, cs)
        zf_l, lru_l, mla_l, u_l, sb_l = _in_projection(h_lat, norm1_g[l], part(ml, 0), part(ml, 1), w_in_p, cs)

        h0 = jnp.zeros((B, 2, 1, GROUP_W), F32)
        b_ctx, s_ctx = _rglru(lru_c, h0, *lru_args)
        b_lat, _ = _rglru(lru_l, s_ctx, *lru_args)

        q_c, k_c, v_c = _mla_prep(mla_c, mla_q_norm_g[l], mla_kv_norm_g[l], wq2, wk, wv, e2, cos_ctx, sin_ctx)
        q_l, k_l, v_l = _mla_prep(mla_l, mla_q_norm_g[l], mla_kv_norm_g[l], wq2, wk, wv, e2, cos_lat, sin_lat)
        k_all = jnp.concatenate([k_c, k_l], axis=2)
        v_all = jnp.concatenate([v_c, v_l], axis=1)
        c_lat = _attention(q_l, k_all, v_all)

        a_lat = _fourier_positions(zf_l, cn_lat, sn_lat)
        d_lat = _short_conv(u_l, sb_l, sc_conv_w[l], sc_conv_b[l])

        hn_lat, f_lat, ti_lat, tg_lat = _out_projection(
            a_lat, b_lat, c_lat, d_lat, h_lat, w_out_b, part(ml, 2), norm2_g[l], part(ml, 3), part(ml, 4),
            router_w[l], router_b[l])

        moe_w = (exp_w_gu, exp_b_gu, exp_w_down, exp_b_down, l)
        tokens_on_lanes = lambda u: jnp.transpose(u, (1, 0, 2)).reshape(8, -1)
        rows = lambda u, w: u.reshape(-1, w)
        g2_lat = part(ml, 5)
        if with_ctx_out:
            a_ctx = _fourier_positions(zf_c, cn_ctx, sn_ctx)
            c_ctx_out = _attention(q_c, k_c, v_c)
            d_ctx = _short_conv(u_c, sb_c, sc_conv_w[l], sc_conv_b[l])
            hn_ctx, f_ctx, ti_ctx, tg_ctx = _out_projection(
                a_ctx, b_ctx, c_ctx_out, d_ctx, h_ctx, w_out_b, part(mc, 2), norm2_g[l], part(mc, 3), part(mc, 4),
                router_w[l], router_b[l])
            S = n_ctx + n
            g2_rows = jnp.concatenate([jnp.broadcast_to(part(mc, 5), (B, n_ctx // tm, D)),
                                       jnp.broadcast_to(g2_lat, (B, n // tm, D))], axis=1)
            cat_t = lambda u, v: tokens_on_lanes(jnp.concatenate([u, v], axis=2))
            h_ctx, h_lat = _moe([rows(f_ctx, 128), rows(f_lat, 128)], [rows(hn_ctx, D), rows(hn_lat, D)],
                                cat_t(ti_ctx, ti_lat), cat_t(tg_ctx, tg_lat),
                                g2_rows.reshape(B * S // tm, 1, D), *moe_w, final_g, tm, n_ctx // tm, n // tm, False)
            h_ctx, h_lat = h_ctx.reshape(B, n_ctx, D), h_lat.reshape(B, n, D)
        else:
            g2_rows = jnp.broadcast_to(g2_lat, (B, n // tm, D)).reshape(B * n // tm, 1, D)
            (h_lat,) = _moe([rows(f_lat, 128)], [rows(hn_lat, D)],
                            tokens_on_lanes(ti_lat), tokens_on_lanes(tg_lat), g2_rows,
                            *moe_w, final_g, tm, 0, n // tm, True)
            h_lat = h_lat.reshape(B, n, D)

    return h_lat
```

```python
import functools
import math

import numpy as np
import jax
import jax.numpy as jnp
from jax import lax
from jax.experimental import pallas as pl
from jax.experimental.pallas import tpu as pltpu

F32 = jnp.float32
BF16 = jnp.bfloat16

D_MODEL = 1024
GRID_W = 64
GROUP_W = 256
EPS = 1e-6
FNET_CH = 64
LRU_HEADS = 4
LRU_C = 8.0
MLA_HEADS = 4
MLA_NOPE = 64
MLA_ROPE = 32
MLA_V = 64
MLA_Q_LORA = 256
MLA_KV_LORA = 128
ROPE_BASE = 10000.0
ATTN_SCALE = (MLA_NOPE + MLA_ROPE) ** -0.5
LOG2_E = math.log2(math.e)
N_EXPERTS = 32
TOP_K = 4
SWIGLU_ALPHA = 1.702
SWIGLU_LIMIT = 7.0

HEAD_PAD = 128
V_ROWS = 80
P_PAD = 2048
MOE_BM = 768
FFN_ROWS = 256
SCAN_GROUP = 8
VMEM_LIMIT = 56 * 1024 * 1024


def _cparams(n_axes, vmem=None):
    return pltpu.CompilerParams(dimension_semantics=("arbitrary",) * n_axes,
                                vmem_limit_bytes=vmem or VMEM_LIMIT)


def _bdot(a, b):
    return jnp.dot(a.astype(BF16), b.astype(BF16), preferred_element_type=F32)


def _mod_kernel(c_ref, w_ref, b_ref, o_ref):
    c = c_ref[...]
    s = c * jax.nn.sigmoid(c)
    o_ref[0] = _bdot(s, w_ref[0]) + b_ref[0]


def _modulation(c_all, mod_w, mod_b):
    L, D, N = mod_w.shape
    R = c_all.shape[0]
    tn = 1024
    return pl.pallas_call(
        _mod_kernel,
        out_shape=jax.ShapeDtypeStruct((L, R, N), F32),
        grid=(L, N // tn),
        in_specs=[pl.BlockSpec((R, D), lambda l, j: (0, 0)),
                  pl.BlockSpec((1, D, tn), lambda l, j: (l, 0, j)),
                  pl.BlockSpec((1, 1, tn), lambda l, j: (l, 0, j))],
        out_specs=pl.BlockSpec((1, R, tn), lambda l, j: (l, 0, j)),
        compiler_params=_cparams(2),
        name="adaln_mod",
    )(c_all, mod_w, mod_b.reshape(L, 1, N))


def _inproj_kernel(h_ref, g_ref, sh_ref, sc_ref, w_ref, cs_ref, zf_ref, lru_ref, mla_ref, u_ref, sb_ref):
    x = h_ref[0]
    y = x * lax.rsqrt(jnp.mean(x * x, axis=-1, keepdims=True) + EPS) * g_ref[...]
    a = y * (1.0 + sc_ref[0]) + sh_ref[0]
    p = _bdot(a, w_ref[...])
    zf_ref[0] = _bdot(p[:, 0:256], cs_ref[...]).astype(BF16)
    lru_ref[0] = p[:, 256:768]
    mla_ref[0] = p[:, 768:1280]
    u_ref[0] = p[:, 1280:1536] * p[:, 1792:2048]
    sb_ref[0] = p[:, 1536:1792]


def _in_projection(h, g, sh, sc, w_in_p, cs):
    B, n, D = h.shape
    tm = min(512, n)
    tile = lambda w: pl.BlockSpec((1, tm, w), lambda b, i: (b, i, 0))
    vec = pl.BlockSpec((1, 1, D), lambda b, i: (b, 0, 0))
    return pl.pallas_call(
        _inproj_kernel,
        out_shape=(jax.ShapeDtypeStruct((B, n, 512), BF16),
                   jax.ShapeDtypeStruct((B, n, 512), F32),
                   jax.ShapeDtypeStruct((B, n, 512), F32),
                   jax.ShapeDtypeStruct((B, n, 256), F32),
                   jax.ShapeDtypeStruct((B, n, 256), F32)),
        grid=(B, n // tm),
        in_specs=[tile(D),
                  pl.BlockSpec((1, D), lambda b, i: (0, 0)),
                  vec, vec,
                  pl.BlockSpec((D, P_PAD), lambda b, i: (0, 0)),
                  pl.BlockSpec((256, 512), lambda b, i: (0, 0))],
        out_specs=(tile(512), tile(512), tile(512), tile(256), tile(256)),
        compiler_params=_cparams(2),
        name="norm_inproj",
    )(h, g.reshape(1, D), sh, sc, w_in_p, cs)


def _fnet_kernel(c_ref, s_ref, z_ref, o_ref):
    z = z_ref[0]
    acc = jnp.dot(c_ref[...], z[:, 0:256], preferred_element_type=F32)
    acc = acc - jnp.dot(s_ref[...], z[:, 256:512], preferred_element_type=F32)
    o_ref[0] = acc.astype(BF16)


def _fourier_positions(zf, cn, sn):
    B, n, _ = zf.shape
    tm = min(512, n)
    return pl.pallas_call(
        _fnet_kernel,
        out_shape=jax.ShapeDtypeStruct((B, n, 256), BF16),
        grid=(n // tm, B),
        in_specs=[pl.BlockSpec((tm, n), lambda i, b: (i, 0)),
                  pl.BlockSpec((tm, n), lambda i, b: (i, 0)),
                  pl.BlockSpec((1, n, 512), lambda i, b: (b, 0, 0))],
        out_specs=pl.BlockSpec((1, tm, 256), lambda i, b: (b, i, 0)),
        compiler_params=_cparams(2),
        name="fourier_positions",
    )(cn, sn, zf)


def _shift_rows(xcat, off, tc):
    if off == 0:
        return xcat[8:8 + tc]
    rolled = pltpu.roll(xcat, (tc + 16) - (8 + off), axis=0)
    return rolled[0:tc]


def _load_halo(ref, col0, width, start, tc, n):
    main = ref[0, pl.ds(start, tc), col0:col0 + width]
    ps = jnp.maximum(start - 8, 0)
    prev = ref[0, pl.ds(pl.multiple_of(ps, 8), 8), col0:col0 + width]
    prev = jnp.where(start > 0, prev, 0.0)
    ns = jnp.minimum(start + tc, n - 8)
    nxt = ref[0, pl.ds(pl.multiple_of(ns, 8), 8), col0:col0 + width]
    nxt = jnp.where(start + tc < n, nxt, 0.0)
    return jnp.concatenate([prev, main, nxt], axis=0)


def _lru_kernel(x_ref, h0_ref, cw_ref, cb_ref, wai_ref, bai_ref, lam_ref, o_ref, hf_ref, hs_ref, *, tc, n):
    nc = n // tc
    in_group = lax.broadcasted_iota(jnp.int32, (tc, GROUP_W), 0) % SCAN_GROUP

    def direction(d, reverse):
        left = 1 if reverse else 2
        cw = cw_ref[d]
        cb = cb_ref[d]
        wai = wai_ref[d]
        bai = bai_ref[d]
        sp = jax.nn.softplus(-lam_ref[d])

        def chunk(ci, carry):
            c = (nc - 1 - ci) if reverse else ci
            start = pl.multiple_of(c * tc, tc)
            xcat = _load_halo(x_ref, 0, GROUP_W, start, tc, n)
            xc = cb
            for k in range(4):
                xc = xc + cw[k:k + 1, :] * _shift_rows(xcat, k - left, tc)
            gates = jax.nn.sigmoid(_bdot(xc, wai) + bai)
            r = gates[:, 0:GROUP_W]
            gi = gates[:, GROUP_W:2 * GROUP_W]
            a = jnp.exp(-LRU_C * r * sp)
            b = jnp.sqrt(jnp.maximum(1.0 - a * a, 0.0)) * (gi * xc)
            s = 1
            while s < SCAN_GROUP:
                if reverse:
                    a_sh = pltpu.roll(a, tc - s, axis=0)
                    b_sh = pltpu.roll(b, tc - s, axis=0)
                    keep = in_group < (SCAN_GROUP - s)
                else:
                    a_sh = pltpu.roll(a, s, axis=0)
                    b_sh = pltpu.roll(b, s, axis=0)
                    keep = in_group >= s
                b = jnp.where(keep, a * b_sh, 0.0) + b
                a = jnp.where(keep, a * a_sh, a)
                s *= 2
            groups = range(tc // SCAN_GROUP)
            pieces = [None] * len(groups)
            state = carry
            for gi in (reversed(groups) if reverse else groups):
                rows_g = slice(gi * SCAN_GROUP, (gi + 1) * SCAN_GROUP)
                hg = a[rows_g] * state + b[rows_g]
                pieces[gi] = hg
                state = hg[0:1, :] if reverse else hg[SCAN_GROUP - 1:SCAN_GROUP, :]
            h = jnp.concatenate(pieces, axis=0)
            if reverse:
                gate = x_ref[0, pl.ds(start, tc), GROUP_W:2 * GROUP_W]
                tot = hs_ref[pl.ds(start, tc), :] + h
                o_ref[0, pl.ds(start, tc), :] = (jax.nn.gelu(gate) * tot).astype(o_ref.dtype)
                return state
            hs_ref[pl.ds(start, tc), :] = h
            return state

        return lax.fori_loop(0, nc, chunk, h0_ref[0, d])

    hf_ref[0, 0] = direction(0, False)
    hf_ref[0, 1] = direction(1, True)


def _rglru(lru, h0, cw, cb, wai, bai, lam):
    B, n, _ = lru.shape
    tc = min(256, n)
    full = lambda *s: pl.BlockSpec(s, lambda b: (0,) * len(s))
    kern = functools.partial(_lru_kernel, tc=tc, n=n)
    return pl.pallas_call(
        kern,
        out_shape=(jax.ShapeDtypeStruct((B, n, GROUP_W), BF16),
                   jax.ShapeDtypeStruct((B, 2, 1, GROUP_W), F32)),
        grid=(B,),
        in_specs=[pl.BlockSpec((1, n, 512), lambda b: (b, 0, 0)),
                  pl.BlockSpec((1, 2, 1, GROUP_W), lambda b: (b, 0, 0, 0)),
                  full(2, 4, GROUP_W), full(2, 1, GROUP_W), full(2, GROUP_W, 2 * GROUP_W),
                  full(2, 1, 2 * GROUP_W), full(2, 1, GROUP_W)],
        out_specs=(pl.BlockSpec((1, n, GROUP_W), lambda b: (b, 0, 0)),
                   pl.BlockSpec((1, 2, 1, GROUP_W), lambda b: (b, 0, 0, 0))),
        scratch_shapes=[pltpu.VMEM((n, GROUP_W), F32)],
        compiler_params=_cparams(1),
        name="rglru",
    )(lru, h0, cw, cb, wai, bai, lam)


def _sconv_kernel(u_ref, sb_ref, w_ref, b_ref, o_ref, *, tc, n):
    w = w_ref[...]
    bias = b_ref[...]

    def chunk(c, _):
        start = pl.multiple_of(c * tc, tc)
        ucat = _load_halo(u_ref, 0, GROUP_W, start, tc, n)
        y = bias
        for k in range(3):
            y = y + w[k:k + 1, :] * _shift_rows(ucat, k - 1, tc)
        o_ref[0, pl.ds(start, tc), :] = (sb_ref[0, pl.ds(start, tc), :] * y).astype(o_ref.dtype)
        return 0

    lax.fori_loop(0, n // tc, chunk, 0)


def _short_conv(u, sb, w, b):
    B, n, W = u.shape
    tc = min(256, n)
    seq = pl.BlockSpec((1, n, W), lambda i: (i, 0, 0))
    return pl.pallas_call(
        functools.partial(_sconv_kernel, tc=tc, n=n),
        out_shape=jax.ShapeDtypeStruct((B, n, W), BF16),
        grid=(B,),
        in_specs=[seq, seq, pl.BlockSpec((3, W), lambda i: (0, 0)), pl.BlockSpec((1, W), lambda i: (0, 0))],
        out_specs=seq,
        compiler_params=_cparams(1),
        name="short_conv",
    )(u, sb, w, b.reshape(1, W))


def _mla_prep_kernel(x_ref, qg_ref, kg_ref, wq_ref, wk_ref, wv_ref, e2_ref, cos_ref, sin_ref,
                     q_ref, k_ref, v_ref):
    x = x_ref[0]
    cq = x[:, 0:256]
    ckv = x[:, 256:384]
    kr = x[:, 384:448]
    cqn = cq * lax.rsqrt(jnp.mean(cq * cq, axis=-1, keepdims=True) + EPS) * qg_ref[...]
    ckvn = ckv * lax.rsqrt(jnp.mean(ckv * ckv, axis=-1, keepdims=True) + EPS) * kg_ref[...]
    cos = cos_ref[...]
    sin = sin_ref[...]
    q2 = _bdot(cqn, wq_ref[...])
    q = (q2[:, 0:512] * cos + q2[:, 512:1024] * sin) * (ATTN_SCALE * LOG2_E)
    kr2 = _bdot(kr, e2_ref[...])
    ckvb = ckvn.astype(BF16)
    k = _bdot(ckvb, wk_ref[...]) + kr2[:, 0:512] * cos + kr2[:, 512:1024] * sin
    vt = lax.dot_general(wv_ref[...], ckvb, (((1,), (1,)), ((), ())), preferred_element_type=F32)
    tm = x.shape[0]
    ones_rows = jnp.where(lax.broadcasted_iota(jnp.int32, (V_ROWS - MLA_V, tm), 0) == 0, 1.0, 0.0)
    pieces = []
    for h in range(MLA_HEADS):
        q_ref[0, h] = q[:, h * HEAD_PAD:(h + 1) * HEAD_PAD].astype(BF16)
        k_ref[0, h] = k[:, h * HEAD_PAD:(h + 1) * HEAD_PAD].astype(BF16)
        pieces += [vt[h * MLA_V:(h + 1) * MLA_V], ones_rows]
    v_ref[0, 0] = jnp.concatenate(pieces, axis=0).astype(BF16)


def _mla_prep(mla, qg, kg, wq2, wk, wv, e2, cos, sin):
    B, n, _ = mla.shape
    tm = min(256, n)
    HW = MLA_HEADS * HEAD_PAD
    full = lambda *s: pl.BlockSpec(s, lambda b, i: (0,) * len(s))
    return pl.pallas_call(
        _mla_prep_kernel,
        out_shape=(jax.ShapeDtypeStruct((B, MLA_HEADS, n, HEAD_PAD), BF16),
                   jax.ShapeDtypeStruct((B, MLA_HEADS, n, HEAD_PAD), BF16),
                   jax.ShapeDtypeStruct((B, n // tm, MLA_HEADS * V_ROWS, tm), BF16)),
        grid=(B, n // tm),
        in_specs=[pl.BlockSpec((1, tm, 512), lambda b, i: (b, i, 0)),
                  full(1, MLA_Q_LORA), full(1, MLA_KV_LORA),
                  full(MLA_Q_LORA, 2 * HW), full(MLA_KV_LORA, HW), full(MLA_HEADS * MLA_V, MLA_KV_LORA),
                  full(2 * MLA_ROPE, 2 * HW),
                  pl.BlockSpec((tm, HW), lambda b, i: (i, 0)),
                  pl.BlockSpec((tm, HW), lambda b, i: (i, 0))],
        out_specs=(pl.BlockSpec((1, MLA_HEADS, tm, HEAD_PAD), lambda b, i: (b, 0, i, 0)),
                   pl.BlockSpec((1, MLA_HEADS, tm, HEAD_PAD), lambda b, i: (b, 0, i, 0)),
                   pl.BlockSpec((1, 1, MLA_HEADS * V_ROWS, tm), lambda b, i: (b, i, 0, 0))),
        compiler_params=_cparams(2),
        name="mla_prep",
    )(mla, qg.reshape(1, -1), kg.reshape(1, -1), wq2, wk, wv, e2, cos, sin)


def _attn_kernel(q_ref, k_ref, vt_ref, o_ref, st_a, st_b, p_a, p_b, *, tk, n_kt):
    tq = q_ref.shape[2]
    H = MLA_HEADS
    qs = [q_ref[0, h] for h in range(H)]

    def scores(j, h):
        kt = k_ref[0, h, pl.ds(pl.multiple_of(j * tk, tk), tk), :]
        return lax.dot_general(kt, qs[h], (((1,), (1,)), ((), ())), preferred_element_type=F32)

    def weighted_values(j, h, p):
        vt = vt_ref[0, j, h * V_ROWS:(h + 1) * V_ROWS, :]
        return jnp.dot(vt, p, preferred_element_type=F32)

    def step(j, st_cur, st_next, p_prev, p_cur, carry):
        a_prev, m, acc = carry
        jn = jnp.minimum(j + 1, n_kt - 1)
        for h in range(H):
            st_next[h] = scores(jn, h)
        pv = [weighted_values(jnp.maximum(j - 1, 0), h, p_prev[h]) for h in range(H)]
        m_new, alpha = [], []
        for h in range(H):
            st = st_cur[h]
            mh = jnp.maximum(m[h], jnp.max(st, axis=0, keepdims=True))
            alpha.append(jnp.exp2(m[h] - mh))
            m_new.append(mh)
            p_cur[h] = jnp.exp2((st - mh).astype(BF16))
        acc_new = [a_prev[h] * acc[h] + pv[h] for h in range(H)]
        return (tuple(alpha), tuple(m_new), tuple(acc_new))

    even = lambda j, c: step(j, st_a, st_b, p_b, p_a, c)
    odd = lambda j, c: step(j, st_b, st_a, p_a, p_b, c)

    for h in range(H):
        st_a[h] = scores(0, h)
    p_b[...] = jnp.zeros(p_b.shape, BF16)
    per_head = lambda f: tuple(f() for _ in range(H))
    carry = (per_head(lambda: jnp.ones((1, tq), F32)),
             per_head(lambda: jnp.full((1, tq), -jnp.inf, F32)),
             per_head(lambda: jnp.zeros((V_ROWS, tq), F32)))
    carry = lax.fori_loop(0, n_kt // 2, lambda i, c: odd(2 * i + 1, even(2 * i, c)), carry)
    if n_kt % 2:
        carry = even(n_kt - 1, carry)
    p_last = p_a if n_kt % 2 else p_b
    a_last, _, acc = carry
    outs = []
    for h in range(H):
        tot = a_last[h] * acc[h] + weighted_values(n_kt - 1, h, p_last[h])
        outs.append(tot[0:MLA_V] / tot[MLA_V:MLA_V + 1])
    o_ref[0] = jnp.concatenate(outs, axis=0).T.astype(o_ref.dtype)


def _attention(q, k, vt):
    B, H, n, _ = q.shape
    M = k.shape[2]
    n_kt, tk = vt.shape[1], vt.shape[3]
    tq = min(256, n)
    return pl.pallas_call(
        functools.partial(_attn_kernel, tk=tk, n_kt=n_kt),
        out_shape=jax.ShapeDtypeStruct((B, n, H * MLA_V), BF16),
        grid=(B, n // tq),
        in_specs=[pl.BlockSpec((1, H, tq, HEAD_PAD), lambda b, i: (b, 0, i, 0)),
                  pl.BlockSpec((1, H, M, HEAD_PAD), lambda b, i: (b, 0, 0, 0)),
                  pl.BlockSpec((1, n_kt, H * V_ROWS, tk), lambda b, i: (b, 0, 0, 0))],
        out_specs=pl.BlockSpec((1, tq, H * MLA_V), lambda b, i: (b, i, 0)),
        scratch_shapes=[pltpu.VMEM((H, tk, tq), F32), pltpu.VMEM((H, tk, tq), F32),
                        pltpu.VMEM((H, tk, tq), BF16), pltpu.VMEM((H, tk, tq), BF16)],
        compiler_params=_cparams(2),
        name="mla_attention",
    )(q, k, vt)


ROW_TILE = 8


def _store_rows_contiguous(ref, lead, x):
    rows = x.shape[0]
    for j in range(ROW_TILE):
        ref[lead + (pl.ds(j, rows, stride=ROW_TILE), slice(None))] = x[:, j * 128:(j + 1) * 128]


def _load_rows_contiguous(ref, lead, rows, dtype):
    return jnp.concatenate([ref[lead + (pl.ds(j, rows, stride=ROW_TILE), slice(None))].astype(dtype)
                            for j in range(ROW_TILE)], axis=-1)


def _outproj_kernel(a_ref, b_ref, c_ref, d_ref, h_ref, w_ref, g1_ref, ng_ref, sh_ref, sc_ref, rw_ref, rb_ref,
                    hn_ref, f_ref, ti_ref, tg_ref):
    mixed = jnp.concatenate([a_ref[0], b_ref[0], c_ref[0], d_ref[0]], axis=-1)
    hn = h_ref[0] + g1_ref[0] * jnp.dot(mixed, w_ref[...], preferred_element_type=F32)
    hn_ref[0] = hn
    y = hn * lax.rsqrt(jnp.mean(hn * hn, axis=-1, keepdims=True) + EPS) * ng_ref[...]
    f = y * (1.0 + sc_ref[0]) + sh_ref[0]
    _store_rows_contiguous(f_ref, (0,), f)
    nt = (((1,), (1,)), ((), ()))
    rw = rw_ref[...]
    rw_hi = rw.astype(BF16)
    rw_lo = (rw - rw_hi.astype(F32)).astype(BF16)
    f_hi = f.astype(BF16)
    f_lo = (f - f_hi.astype(F32)).astype(BF16)
    logits = (lax.dot_general(rw_hi, f_hi, nt, preferred_element_type=F32)
              + (lax.dot_general(rw_hi, f_lo, nt, preferred_element_type=F32)
                 + lax.dot_general(rw_lo, f_hi, nt, preferred_element_type=F32))) + rb_ref[...]
    tm = logits.shape[1]
    sub = lax.broadcasted_iota(jnp.int32, (N_EXPERTS, tm), 0)
    row = lax.broadcasted_iota(jnp.int32, (8, tm), 0)
    top_v = jnp.full((8, tm), -jnp.inf, F32)
    top_i = jnp.zeros((8, tm), jnp.int32)
    cur = logits
    m0 = None
    for kk in range(TOP_K):
        m = jnp.max(cur, axis=0, keepdims=True)
        sel = jnp.min(jnp.where(cur == m, sub, N_EXPERTS), axis=0, keepdims=True)
        if kk == 0:
            m0 = m
        top_v = jnp.where(row == kk, m, top_v)
        top_i = jnp.where(row == kk, sel, top_i)
        cur = jnp.where(sub == sel, -jnp.inf, cur)
    e = jnp.where(row < TOP_K, jnp.exp(top_v - m0), 0.0)
    ti_ref[0] = top_i
    tg_ref[0] = e / jnp.sum(e, axis=0, keepdims=True)


def _out_projection(a, b, c, d, h, w_out, g1, ng, sh, sc, rw, rb):
    B, n, D = h.shape
    tm = min(512, n)
    tile = lambda w: pl.BlockSpec((1, tm, w), lambda bb, i: (bb, i, 0))
    vec = pl.BlockSpec((1, 1, D), lambda bb, i: (bb, 0, 0))
    full = lambda *s: pl.BlockSpec(s, lambda bb, i: (0,) * len(s))
    return pl.pallas_call(
        _outproj_kernel,
        out_shape=(jax.ShapeDtypeStruct((B, n, D), F32),
                   jax.ShapeDtypeStruct((B, n * ROW_TILE, 128), F32),
                   jax.ShapeDtypeStruct((B, 8, n), jnp.int32),
                   jax.ShapeDtypeStruct((B, 8, n), F32)),
        grid=(B, n // tm),
        in_specs=[tile(256), tile(256), tile(256), tile(256), tile(D),
                  full(D, D), vec, full(1, D), vec, vec, full(N_EXPERTS, D), full(N_EXPERTS, 1)],
        out_specs=(tile(D), pl.BlockSpec((1, tm * ROW_TILE, 128), lambda bb, i: (bb, i, 0)),
                   pl.BlockSpec((1, 8, tm), lambda bb, i: (bb, 0, i)),
                   pl.BlockSpec((1, 8, tm), lambda bb, i: (bb, 0, i))),
        compiler_params=_cparams(2),
        name="outproj_router",
    )(a, b, c, d, h, w_out, g1, ng.reshape(1, D), sh, sc, rw.T, rb.reshape(N_EXPERTS, 1))


def _rank_kernel(ti_ref, rank_ref, cnt_ref, carry_ref, *, tm):
    @pl.when(pl.program_id(0) == 0)
    def _():
        carry_ref[...] = jnp.zeros_like(carry_ref)

    ti = ti_ref[...]
    sub = lax.broadcasted_iota(jnp.int32, (N_EXPERTS, tm), 0)
    hot = jnp.zeros((N_EXPERTS, tm), F32)
    for kk in range(TOP_K):
        hot = hot + jnp.where(sub == ti[kk:kk + 1, :], 1.0, 0.0)
    r = lax.broadcasted_iota(jnp.int32, (tm, tm), 0)
    c = lax.broadcasted_iota(jnp.int32, (tm, tm), 1)
    earlier = jnp.where(r < c, 1.0, 0.0).astype(BF16)
    carry = carry_ref[:, 0:1]
    before = jnp.dot(hot.astype(BF16), earlier, preferred_element_type=F32) + carry
    row = lax.broadcasted_iota(jnp.int32, (8, tm), 0)
    out = jnp.zeros((8, tm), jnp.int32)
    for kk in range(TOP_K):
        rk = jnp.sum(jnp.where(sub == ti[kk:kk + 1, :], before, 0.0), axis=0, keepdims=True)
        out = jnp.where(row == kk, rk.astype(jnp.int32), out)
    rank_ref[...] = out
    total = carry + jnp.sum(hot, axis=1, keepdims=True)
    carry_ref[...] = jnp.broadcast_to(total, carry_ref.shape)
    cnt_ref[...] = jnp.broadcast_to(total, cnt_ref.shape).astype(jnp.int32)


def _expert_ranks(top_i_t):
    T = top_i_t.shape[1]
    tm = 512 if T % 512 == 0 else 256
    return pl.pallas_call(
        functools.partial(_rank_kernel, tm=tm),
        out_shape=(jax.ShapeDtypeStruct((8, T), jnp.int32), jax.ShapeDtypeStruct((N_EXPERTS, 128), jnp.int32)),
        grid=(T // tm,),
        in_specs=[pl.BlockSpec((8, tm), lambda i: (0, i))],
        out_specs=(pl.BlockSpec((8, tm), lambda i: (0, i)), pl.BlockSpec((N_EXPERTS, 128), lambda i: (0, 0))),
        scratch_shapes=[pltpu.VMEM((N_EXPERTS, 128), F32)],
        compiler_params=_cparams(1),
        name="moe_rank",
    )(top_i_t)


def _stream_maps(nc, nl):
    per_batch = nc + nl
    ctx_map = lambda i, *_: ((i // per_batch) * nc + jnp.minimum(i % per_batch, nc - 1), 0)
    lat_map = lambda i, *_: ((i // per_batch) * nl + jnp.maximum(i % per_batch - nc, 0), 0)
    return [ctx_map, lat_map] if nc > 0 else [lat_map]


def _dispatch_kernel(cnt_ref, ps_ref, pe_ref, pos_ref, *refs, tm, n_slots, nc, nl):
    n_in = 2 if nc > 0 else 1
    f_refs = refs[:n_in]
    xs_ref, buf_ref, zrow_ref, sem_ref, zsem_ref = refs[n_in:]
    i = pl.program_id(0)
    nt = pl.num_programs(0)
    copies = tm * TOP_K
    zslots = zrow_ref.shape[0] // ROW_TILE

    def tile_rows(first_row, n_rows):
        return pl.ds(pl.multiple_of(first_row * ROW_TILE, ROW_TILE), n_rows * ROW_TILE)

    def zero_fill(lo, hi, start):
        def chunk(off, size):
            cp = pltpu.make_async_copy(zrow_ref.at[tile_rows(0, size)], xs_ref.at[tile_rows(off, size)], zsem_ref)
            cp.start() if start else cp.wait()

        n = hi - lo

        def whole(j, _):
            chunk(lo + j * zslots, zslots)
            return 0

        lax.fori_loop(0, n // zslots, whole, 0)
        rem_lo = lo + (n // zslots) * zslots
        rem = n % zslots
        size = zslots // 2
        while size >= 1:
            @pl.when((rem & size) != 0)
            def _(size=size):
                chunk(rem_lo + (rem & ~(2 * size - 1)), size)
            size //= 2

    @pl.when(i == 0)
    def _():
        zrow_ref[...] = jnp.zeros_like(zrow_ref)
        for start in (True, False):
            for e in range(N_EXPERTS):
                zero_fill(ps_ref[e] + cnt_ref[e], pe_ref[e], start)
            zero_fill(pe_ref[N_EXPERTS - 1], n_slots, start)

    slot = i % 2
    if nc > 0:
        j = i % (nc + nl)

        @pl.when(j < nc)
        def _():
            buf_ref[slot] = f_refs[0][...]

        @pl.when(j >= nc)
        def _():
            buf_ref[slot] = f_refs[1][...]
    else:
        buf_ref[slot] = f_refs[0][...]

    def issue(t, _):
        for kk in range(TOP_K):
            pltpu.make_async_copy(buf_ref.at[slot, tile_rows(t, 1)],
                                  xs_ref.at[tile_rows(pos_ref[0, 0, t * TOP_K + kk], 1)],
                                  sem_ref.at[slot]).start(priority=kk % 2)
        return 0

    lax.fori_loop(0, tm, issue, 0, unroll=2)

    def wait_rows(s):
        pltpu.make_async_copy(xs_ref.at[tile_rows(0, copies)], xs_ref.at[tile_rows(0, copies)], sem_ref.at[s]).wait()

    @pl.when(i > 0)
    def _():
        wait_rows(1 - slot)

    @pl.when(i == nt - 1)
    def _():
        wait_rows(slot)


def _dispatch(fs, pos, counts, pad_start, pad_end, n_slots, tm, nc, nl):
    nt = pos.shape[0] // (tm * TOP_K)
    grid_spec = pltpu.PrefetchScalarGridSpec(
        num_scalar_prefetch=3,
        grid=(nt,),
        in_specs=[pl.BlockSpec((1, 1, tm * TOP_K), lambda i, *_: (i, 0, 0), memory_space=pltpu.SMEM)]
        + [pl.BlockSpec((tm * ROW_TILE, 128), m) for m in _stream_maps(nc, nl)],
        out_specs=pl.BlockSpec(memory_space=pl.ANY),
        scratch_shapes=[pltpu.VMEM((2, tm * ROW_TILE, 128), F32), pltpu.VMEM((256 * ROW_TILE, 128), F32),
                        pltpu.SemaphoreType.DMA((2,)), pltpu.SemaphoreType.DMA],
    )
    return pl.pallas_call(
        functools.partial(_dispatch_kernel, tm=tm, n_slots=n_slots, nc=nc, nl=nl),
        out_shape=jax.ShapeDtypeStruct((n_slots * ROW_TILE, 128), F32),
        grid_spec=grid_spec,
        compiler_params=_cparams(1),
        name="moe_dispatch",
    )(counts, pad_start, pad_end, pos.reshape(nt, 1, tm * TOP_K), *fs)


def _ffn_kernel(be_ref, bv_ref, nx_ref, par_ref, x_ref, wgu_hbm, bgu_ref, wd_hbm, bd_ref, y_ref,
                wgu_f, wd_f, wgu_s, wd_s, wsem, *, bm, layer):
    i = pl.program_id(0)
    valid = bv_ref[i]

    def weights(e, slot, start):
        for src, dst in ((wgu_hbm, wgu_f), (wd_hbm, wd_f)):
            cp = pltpu.make_async_copy(src.at[layer, e], dst.at[slot], wsem.at[slot])
            cp.start() if start else cp.wait()

    def experts_rows(rows):
        De = wd_s.shape[0]
        x = _load_rows_contiguous(x_ref, (), rows, BF16)
        gu = jnp.dot(x, wgu_s[...], preferred_element_type=F32) + bgu_ref[0, 0]
        g = jnp.minimum(gu[:, 0:De], SWIGLU_LIMIT)
        u = jnp.clip(gu[:, De:2 * De], -SWIGLU_LIMIT, SWIGLU_LIMIT)
        act = (u + 1.0) * (g * jax.nn.sigmoid(SWIGLU_ALPHA * g))
        y = jnp.dot(act.astype(BF16), wd_s[...], preferred_element_type=F32) + bd_ref[0, 0]
        _store_rows_contiguous(y_ref, (), y)
        if rows < bm:
            y_ref[rows * ROW_TILE:bm * ROW_TILE, :] = jnp.zeros(((bm - rows) * ROW_TILE, 128), y_ref.dtype)

    @pl.when(valid > 0)
    def _():
        @pl.when((i == 0) | (be_ref[i] != be_ref[jnp.maximum(i - 1, 0)]))
        def _():
            slot = par_ref[i]

            @pl.when(i == 0)
            def _():
                weights(be_ref[i], slot, True)

            weights(be_ref[i], slot, False)

            @pl.when(nx_ref[i] >= 0)
            def _():
                weights(nx_ref[i], 1 - slot, True)

            wgu_s[...] = wgu_f[slot].astype(BF16)
            wd_s[...] = wd_f[slot].astype(BF16)

        for rows in range(FFN_ROWS, bm + 1, FFN_ROWS):
            pl.when((valid > rows - FFN_ROWS) & (valid <= rows))(functools.partial(experts_rows, rows))

    @pl.when(valid <= 0)
    def _():
        y_ref[...] = jnp.zeros_like(y_ref)


def _expert_ffn(xs, block_exp, block_valid, next_exp, parity, wgu, bgu, wd, bd, layer, bm):
    n_slots = xs.shape[0] // ROW_TILE
    L, E, D, G = wgu.shape
    nb = n_slots // bm
    bmap = lambda i, be, *_: (layer, be[i], 0, 0)
    rows_spec = pl.BlockSpec((bm * ROW_TILE, 128), lambda i, *_: (i, 0))
    grid_spec = pltpu.PrefetchScalarGridSpec(
        num_scalar_prefetch=4,
        grid=(nb,),
        in_specs=[rows_spec,
                  pl.BlockSpec(memory_space=pl.ANY),
                  pl.BlockSpec((1, 1, 1, G), bmap),
                  pl.BlockSpec(memory_space=pl.ANY),
                  pl.BlockSpec((1, 1, 1, D), bmap)],
        out_specs=rows_spec,
        scratch_shapes=[pltpu.VMEM((2, D, G), F32), pltpu.VMEM((2, G // 2, D), F32),
                        pltpu.VMEM((D, G), BF16), pltpu.VMEM((G // 2, D), BF16),
                        pltpu.SemaphoreType.DMA((2,))],
    )
    return pl.pallas_call(
        functools.partial(_ffn_kernel, bm=bm, layer=layer),
        out_shape=jax.ShapeDtypeStruct((n_slots * ROW_TILE, 128), F32),
        grid_spec=grid_spec,
        compiler_params=_cparams(1),
        name="moe_expert_ffn",
    )(block_exp, block_valid, next_exp, parity, xs, wgu, bgu.reshape(L, E, 1, G), wd, bd.reshape(L, E, 1, D))


def _combine_kernel(pos_ref, pos_next_ref, y_ref, *refs, tm, nc, nl, final_norm):
    n_s = 2 if nc > 0 else 1
    h_refs = refs[:n_s]
    g2_ref, gate_ref, fg_ref = refs[n_s:n_s + 3]
    o_refs = refs[n_s + 3:2 * n_s + 3]
    buf_ref, sem_ref = refs[2 * n_s + 3:]
    i = pl.program_id(0)
    nt = pl.num_programs(0)
    slot = i % 2

    def tile_rows(first_row):
        return pl.ds(pl.multiple_of(first_row * ROW_TILE, ROW_TILE), ROW_TILE)

    def gather_tile(p_ref, s):
        def issue(r, _):
            for kk in range(TOP_K):
                pltpu.make_async_copy(y_ref.at[tile_rows(p_ref[0, 0, r * TOP_K + kk])],
                                      buf_ref.at[s, kk, tile_rows(r)], sem_ref.at[s]).start(priority=kk % 2)
            return 0

        lax.fori_loop(0, tm, issue, 0, unroll=2)

    @pl.when(i == 0)
    def _():
        gather_tile(pos_ref, 0)

    @pl.when(i + 1 < nt)
    def _():
        gather_tile(pos_next_ref, 1 - slot)

    pltpu.make_async_copy(buf_ref.at[slot], buf_ref.at[slot], sem_ref.at[slot]).wait()
    gate = gate_ref[...]
    acc = gate[:, 0:1] * _load_rows_contiguous(buf_ref, (slot, 0), tm, F32)
    for kk in range(1, TOP_K):
        acc = acc + gate[:, kk:kk + 1] * _load_rows_contiguous(buf_ref, (slot, kk), tm, F32)
    moe = g2_ref[0] * acc

    def finish(h_ref, o_ref):
        out = h_ref[...] + moe
        if final_norm:
            out = out * lax.rsqrt(jnp.mean(out * out, axis=-1, keepdims=True) + EPS) * fg_ref[...]
        o_ref[...] = out

    if nc > 0:
        j = i % (nc + nl)
        pl.when(j < nc)(lambda: finish(h_refs[0], o_refs[0]))
        pl.when(j >= nc)(lambda: finish(h_refs[1], o_refs[1]))
    else:
        finish(h_refs[0], o_refs[0])


def _combine(y, pos, hs, g2_tiles, gates, final_g, tm, nc, nl, final_norm):
    D = hs[0].shape[1]
    nt = pos.shape[0] // (tm * TOP_K)
    maps = _stream_maps(nc, nl)
    streams = [pl.BlockSpec((tm, D), m) for m in maps]
    pos3 = pos.reshape(nt, 1, tm * TOP_K)
    outs = pl.pallas_call(
        functools.partial(_combine_kernel, tm=tm, nc=nc, nl=nl, final_norm=final_norm),
        out_shape=[jax.ShapeDtypeStruct(h.shape, F32) for h in hs],
        grid=(nt,),
        in_specs=[pl.BlockSpec((1, 1, tm * TOP_K), lambda i: (i, 0, 0), memory_space=pltpu.SMEM),
                  pl.BlockSpec((1, 1, tm * TOP_K), lambda i: (jnp.minimum(i + 1, nt - 1), 0, 0),
                               memory_space=pltpu.SMEM),
                  pl.BlockSpec(memory_space=pl.ANY)]
        + streams
        + [pl.BlockSpec((1, 1, D), lambda i: (i, 0, 0)),
           pl.BlockSpec((tm, 8), lambda i: (i, 0)),
           pl.BlockSpec((1, D), lambda i: (0, 0))],
        out_specs=streams,
        scratch_shapes=[pltpu.VMEM((2, TOP_K, tm * ROW_TILE, 128), F32), pltpu.SemaphoreType.DMA((2,))],
        compiler_params=_cparams(1),
        name="moe_combine",
    )(pos3, pos3, y, *hs, g2_tiles, gates, final_g.reshape(1, D))
    return outs


def _slots_kernel(ps_ref, ti_ref, rank_ref, pos_ref):
    ti = ti_ref[...]
    start = jnp.zeros(ti.shape, jnp.int32)
    for e in range(N_EXPERTS):
        start = jnp.where(ti == e, ps_ref[e], start)
    pos_ref[...] = start + rank_ref[...]


def _slots(pad_start, top_i_t, rank_t):
    shape = top_i_t.shape
    whole = pl.BlockSpec(shape, lambda i, ps: (0, 0))
    return pl.pallas_call(
        _slots_kernel,
        out_shape=jax.ShapeDtypeStruct(shape, jnp.int32),
        grid_spec=pltpu.PrefetchScalarGridSpec(num_scalar_prefetch=1, grid=(1,), in_specs=[whole, whole],
                                               out_specs=whole),
        compiler_params=_cparams(1),
        name="moe_slots",
    )(pad_start, top_i_t, rank_t)


def _moe(fs, hs, top_i_t, top_g_t, g2_tiles, wgu, bgu, wd, bd, layer, final_g, tm, nc, nl, final_norm):
    T = top_i_t.shape[1]
    bm = MOE_BM
    n_asg = T * TOP_K
    rank_t, cnt = _expert_ranks(top_i_t)
    counts = cnt[:, 0]
    padded = (counts + bm - 1) // bm * bm
    pad_end = jnp.cumsum(padded).astype(jnp.int32)
    pad_start = pad_end - padded
    pos = _slots(pad_start, top_i_t, rank_t)[0:TOP_K].T.reshape(-1)
    nb = (n_asg + N_EXPERTS * (bm - 1) + bm - 1) // bm
    n_slots = nb * bm
    starts = jnp.arange(nb, dtype=jnp.int32) * bm
    block_exp = jnp.minimum(jnp.sum((pad_end[None, :] <= starts[:, None]).astype(jnp.int32), axis=1), N_EXPERTS - 1)
    block_valid = jnp.clip(pad_start[block_exp] + counts[block_exp] - starts, 0, bm)
    block_valid = jnp.where(starts < pad_end[-1], block_valid, 0).astype(jnp.int32)
    ids = jnp.arange(N_EXPERTS, dtype=jnp.int32)
    used = counts > 0
    later_used = used[None, :] & (ids[None, :] > ids[:, None])
    next_of = jnp.min(jnp.where(later_used, ids[None, :], N_EXPERTS), axis=1)
    next_of = jnp.where(next_of < N_EXPERTS, next_of, -1).astype(jnp.int32)
    order_of = (jnp.cumsum(used.astype(jnp.int32)) - 1).astype(jnp.int32)
    hot = (block_exp[:, None] == ids[None, :]).astype(jnp.int32)
    next_exp = jnp.sum(hot * next_of[None, :], axis=1).astype(jnp.int32)
    parity = (jnp.sum(hot * order_of[None, :], axis=1) % 2).astype(jnp.int32)

    xs = _dispatch(fs, pos, counts, pad_start, pad_end, n_slots, tm, nc, nl)
    y = _expert_ffn(xs, block_exp, block_valid, next_exp, parity, wgu, bgu, wd, bd, layer, bm)
    return _combine(y, pos, hs, g2_tiles, top_g_t.T, final_g, tm, nc, nl, final_norm)


_ROPE_SWAP = np.concatenate([np.arange(8, 16), np.arange(0, 8), np.arange(24, 32), np.arange(16, 24)])


def _dft_tables(n):
    hi = n // 64
    j = jnp.arange(n, dtype=jnp.int32)[:, None]
    k = jnp.arange(64, dtype=jnp.int32)[None, :]
    k2 = jnp.arange(hi, dtype=jnp.int32)[None, :]
    alpha = ((j * k2) % hi).astype(F32) * (2.0 * math.pi / hi)
    beta = ((j * k) % n).astype(F32) * (2.0 * math.pi / n)
    ca, sa, cb, sb = lax.optimization_barrier((jnp.cos(alpha), jnp.sin(alpha), jnp.cos(beta), jnp.sin(beta)))
    ca, sa = ca[:, :, None], sa[:, :, None]
    cb, sb = cb[:, None, :], sb[:, None, :]
    scale = 1.0 / math.sqrt(n)
    cn = ((ca * cb - sa * sb) * scale).reshape(n, n).astype(BF16)
    sn = ((sa * cb + ca * sb) * scale).reshape(n, n).astype(BF16)
    return cn, sn


def _channel_dft():
    j = jnp.arange(GROUP_W, dtype=jnp.int32)
    same_group = (j[:, None] // FNET_CH) == (j[None, :] // FNET_CH)
    ang = ((j[:, None] * j[None, :]) % FNET_CH).astype(F32) * (2.0 * math.pi / FNET_CH)
    scale = 1.0 / math.sqrt(FNET_CH)
    c = jnp.where(same_group, jnp.cos(ang) * scale, 0.0)
    s = jnp.where(same_group, jnp.sin(ang) * scale, 0.0)
    return jnp.concatenate([c, s], axis=1).astype(BF16)


def _rope_tables(n, rotary):
    HW = MLA_HEADS * HEAD_PAD
    cos_blk = np.zeros((HEAD_PAD,), np.float32)
    cos_blk[0:MLA_NOPE + MLA_ROPE] = 1.0
    if not rotary:
        return (jnp.broadcast_to(jnp.asarray(np.tile(cos_blk, MLA_HEADS)), (n, HW)),
                jnp.zeros((n, HW), F32))
    half = MLA_ROPE // 2
    inv = ROPE_BASE ** (-jnp.arange(0, half, 2, dtype=F32) / half)
    t = jnp.arange(n, dtype=jnp.int32)
    row = (t // GRID_W).astype(F32)
    col = (t % GRID_W).astype(F32)
    ar = row[:, None] * inv[None, :]
    ac = col[:, None] * inv[None, :]
    cos32 = jnp.concatenate([jnp.cos(ar), jnp.cos(ar), jnp.cos(ac), jnp.cos(ac)], axis=1)
    sin32 = jnp.concatenate([-jnp.sin(ar), jnp.sin(ar), -jnp.sin(ac), jnp.sin(ac)], axis=1)
    ones = jnp.ones((n, MLA_NOPE), F32)
    zeros_n = jnp.zeros((n, MLA_NOPE), F32)
    zeros_p = jnp.zeros((n, HEAD_PAD - MLA_NOPE - MLA_ROPE), F32)
    cos_h = jnp.concatenate([ones, cos32, zeros_p], axis=1)
    sin_h = jnp.concatenate([zeros_n, sin32, zeros_p], axis=1)
    return jnp.tile(cos_h, (1, MLA_HEADS)), jnp.tile(sin_h, (1, MLA_HEADS))


def _relayout_w_in(w_in):
    D = w_in.shape[0]
    krope = w_in[:, 1152:1184]
    return jnp.concatenate([w_in[:, 0:1184], krope[:, _ROPE_SWAP], jnp.zeros((D, 64), w_in.dtype),
                            w_in[:, 1184:1952]], axis=1).astype(BF16)


def _relayout_mla(w_uq, w_ukv):
    dq = MLA_NOPE + MLA_ROPE
    q_main, q_swap, k_cols, v_cols = [], [], [], []
    zq = jnp.zeros((MLA_Q_LORA, HEAD_PAD - dq), w_uq.dtype)
    zn = jnp.zeros((MLA_Q_LORA, MLA_NOPE), w_uq.dtype)
    zk = jnp.zeros((MLA_KV_LORA, HEAD_PAD - MLA_NOPE), w_ukv.dtype)
    for h in range(MLA_HEADS):
        wq_h = w_uq[:, h * dq:(h + 1) * dq]
        rope = wq_h[:, MLA_NOPE:]
        q_main += [wq_h, zq]
        q_swap += [zn, rope[:, _ROPE_SWAP], zq]
        kv_h = w_ukv[:, h * (MLA_NOPE + MLA_V):(h + 1) * (MLA_NOPE + MLA_V)]
        k_cols += [kv_h[:, 0:MLA_NOPE], zk]
        v_cols += [kv_h[:, MLA_NOPE:]]
    wq2 = jnp.concatenate(q_main + q_swap, axis=1).astype(BF16)
    wk = jnp.concatenate(k_cols, axis=1).astype(BF16)
    wv = jnp.concatenate(v_cols, axis=1).T.astype(BF16)
    HW = MLA_HEADS * HEAD_PAD
    e2 = np.zeros((2 * MLA_ROPE, 2 * HW), np.float32)
    for h in range(MLA_HEADS):
        for jj in range(MLA_ROPE):
            e2[jj, h * HEAD_PAD + MLA_NOPE + jj] = 1.0
            e2[MLA_ROPE + jj, HW + h * HEAD_PAD + MLA_NOPE + jj] = 1.0
    return wq2, wk, wv, jnp.asarray(e2, BF16)


def _block_diag_heads(w):
    H, d, _ = w.shape
    eye = jnp.eye(H, dtype=w.dtype)
    return (eye[:, None, :, None] * w[:, :, None, :]).reshape(H * d, H * d)


def kernel(x, c, ctx, c_ctx, mod_w, mod_b, norm1_g, norm2_g, w_in, lru_conv_w, lru_conv_b, lru_w_a, lru_b_a, lru_w_i, lru_b_i, lru_lambda, mla_q_norm_g, mla_w_uq, mla_kv_norm_g, mla_w_ukv, sc_conv_w, sc_conv_b, w_out, router_w, router_b, exp_w_gu, exp_b_gu, exp_w_down, exp_b_down, final_g):
    B, n, D = x.shape
    n_ctx = ctx.shape[1]
    depth = mod_w.shape[0]
    tm = min(256, n, n_ctx)

    rows = 8 * ((B + 1 + 7) // 8)
    c_all = jnp.concatenate([c, c_ctx[None, :], jnp.zeros((rows - B - 1, D), F32)], axis=0)
    mod = _modulation(c_all, mod_w, mod_b)

    cs = _channel_dft()
    cn_lat, sn_lat = _dft_tables(n)
    cn_ctx, sn_ctx = _dft_tables(n_ctx)
    cos_lat, sin_lat = _rope_tables(n, True)
    cos_ctx, sin_ctx = _rope_tables(n_ctx, False)

    h_lat, h_ctx = x, ctx
    for l in range(depth):
        with_ctx_out = l < depth - 1
        ml = mod[l, 0:B].reshape(B, 1, 6, D)
        mc = jnp.broadcast_to(mod[l, B:B + 1].reshape(1, 1, 6, D), (B, 1, 6, D))
        part = lambda m, i: m[:, :, i, :]
        w_in_p = _relayout_w_in(w_in[l])
        wq2, wk, wv, e2 = _relayout_mla(mla_w_uq[l], mla_w_ukv[l])
        wai = jnp.stack([jnp.concatenate([_block_diag_heads(lru_w_a[l, d]), _block_diag_heads(lru_w_i[l, d])], axis=1)
                         for d in range(2)]).astype(BF16)
        bai = jnp.concatenate([lru_b_a[l], lru_b_i[l]], axis=1)[:, None, :]
        lru_args = (lru_conv_w[l], lru_conv_b[l][:, None, :], wai, bai, lru_lambda[l][:, None, :])
        w_out_b = w_out[l].astype(BF16)

        zf_c, lru_c, mla_c, u_c, sb_c = _in_projection(h_ctx, norm1_g[l], part(mc, 0), part(mc, 1), w_in_p, cs)
        zf_l, lru_l, mla_l, u_l, sb_l = _in_projection(h_lat, norm1_g[l], part(ml, 0), part(ml, 1), w_in_p, cs)

        h0 = jnp.zeros((B, 2, 1, GROUP_W), F32)
        b_ctx, s_ctx = _rglru(lru_c, h0, *lru_args)
        b_lat, _ = _rglru(lru_l, s_ctx, *lru_args)

        q_c, k_c, v_c = _mla_prep(mla_c, mla_q_norm_g[l], mla_kv_norm_g[l], wq2, wk, wv, e2, cos_ctx, sin_ctx)
        q_l, k_l, v_l = _mla_prep(mla_l, mla_q_norm_g[l], mla_kv_norm_g[l], wq2, wk, wv, e2, cos_lat, sin_lat)
        k_all = jnp.concatenate([k_c, k_l], axis=2)
        v_all = jnp.concatenate([v_c, v_l], axis=1)
        c_lat = _attention(q_l, k_all, v_all)

        a_lat = _fourier_positions(zf_l, cn_lat, sn_lat)
        d_lat = _short_conv(u_l, sb_l, sc_conv_w[l], sc_conv_b[l])

        hn_lat, f_lat, ti_lat, tg_lat = _out_projection(
            a_lat, b_lat, c_lat, d_lat, h_lat, w_out_b, part(ml, 2), norm2_g[l], part(ml, 3), part(ml, 4),
            router_w[l], router_b[l])

        moe_w = (exp_w_gu, exp_b_gu, exp_w_down, exp_b_down, l)
        tokens_on_lanes = lambda u: jnp.transpose(u, (1, 0, 2)).reshape(8, -1)
        rows = lambda u, w: u.reshape(-1, w)
        g2_lat = part(ml, 5)
        if with_ctx_out:
            a_ctx = _fourier_positions(zf_c, cn_ctx, sn_ctx)
            c_ctx_out = _attention(q_c, k_c, v_c)
            d_ctx = _short_conv(u_c, sb_c, sc_conv_w[l], sc_conv_b[l])
            hn_ctx, f_ctx, ti_ctx, tg_ctx = _out_projection(
                a_ctx, b_ctx, c_ctx_out, d_ctx, h_ctx, w_out_b, part(mc, 2), norm2_g[l], part(mc, 3), part(mc, 4),
                router_w[l], router_b[l])
            S = n_ctx + n
            g2_rows = jnp.concatenate([jnp.broadcast_to(part(mc, 5), (B, n_ctx // tm, D)),
                                       jnp.broadcast_to(g2_lat, (B, n // tm, D))], axis=1)
            cat_t = lambda u, v: tokens_on_lanes(jnp.concatenate([u, v], axis=2))
            h_ctx, h_lat = _moe([rows(f_ctx, 128), rows(f_lat, 128)], [rows(hn_ctx, D), rows(hn_lat, D)],
                                cat_t(ti_ctx, ti_lat), cat_t(tg_ctx, tg_lat),
                                g2_rows.reshape(B * S // tm, 1, D), *moe_w, final_g, tm, n_ctx // tm, n // tm, False)
            h_ctx, h_lat = h_ctx.reshape(B, n_ctx, D), h_lat.reshape(B, n, D)
        else:
            g2_rows = jnp.broadcast_to(g2_lat, (B, n // tm, D)).reshape(B * n // tm, 1, D)
            (h_lat,) = _moe([rows(f_lat, 128)], [rows(hn_lat, D)],
                            tokens_on_lanes(ti_lat), tokens_on_lanes(tg_lat), g2_rows,
                            *moe_w, final_g, tm, 0, n // tm, True)
            h_lat = h_lat.reshape(B, n, D)

    return h_lat
```

```python
import functools
import math

import numpy as np
import jax
import jax.numpy as jnp
from jax import lax
from jax.experimental import pallas as pl
from jax.experimental.pallas import tpu as pltpu

F32 = jnp.float32
BF16 = jnp.bfloat16

D_MODEL = 1024
GRID_W = 64
GROUP_W = 256
EPS = 1e-6
FNET_CH = 64
LRU_HEADS = 4
LRU_C = 8.0
MLA_HEADS = 4
MLA_NOPE = 64
MLA_ROPE = 32
MLA_V = 64
MLA_Q_LORA = 256
MLA_KV_LORA = 128
ROPE_BASE = 10000.0
ATTN_SCALE = (MLA_NOPE + MLA_ROPE) ** -0.5
LOG2_E = math.log2(math.e)
N_EXPERTS = 32
TOP_K = 4
SWIGLU_ALPHA = 1.702
SWIGLU_LIMIT = 7.0

HEAD_PAD = 128
V_ROWS = 80
P_PAD = 2048
MOE_BM = 768
FFN_ROWS = 256
SCAN_GROUP = 8
VMEM_LIMIT = 56 * 1024 * 1024


def _cparams(n_axes, vmem=None):
    return pltpu.CompilerParams(dimension_semantics=("arbitrary",) * n_axes,
                                vmem_limit_bytes=vmem or VMEM_LIMIT)


def _bdot(a, b):
    return jnp.dot(a.astype(BF16), b.astype(BF16), preferred_element_type=F32)


def _mod_kernel(c_ref, w_ref, b_ref, o_ref):
    c = c_ref[...]
    s = c * jax.nn.sigmoid(c)
    o_ref[0] = _bdot(s, w_ref[0]) + b_ref[0]


def _modulation(c_all, mod_w, mod_b):
    L, D, N = mod_w.shape
    R = c_all.shape[0]
    tn = 1024
    return pl.pallas_call(
        _mod_kernel,
        out_shape=jax.ShapeDtypeStruct((L, R, N), F32),
        grid=(L, N // tn),
        in_specs=[pl.BlockSpec((R, D), lambda l, j: (0, 0)),
                  pl.BlockSpec((1, D, tn), lambda l, j: (l, 0, j)),
                  pl.BlockSpec((1, 1, tn), lambda l, j: (l, 0, j))],
        out_specs=pl.BlockSpec((1, R, tn), lambda l, j: (l, 0, j)),
        compiler_params=_cparams(2),
        name="adaln_mod",
    )(c_all, mod_w, mod_b.reshape(L, 1, N))


def _inproj_kernel(h_ref, g_ref, sh_ref, sc_ref, w_ref, cs_ref, zf_ref, lru_ref, mla_ref, u_ref, sb_ref):
    x = h_ref[0]
    y = x * lax.rsqrt(jnp.mean(x * x, axis=-1, keepdims=True) + EPS) * g_ref[...]
    a = y * (1.0 + sc_ref[0]) + sh_ref[0]
    p = _bdot(a, w_ref[...])
    zf_ref[0] = _bdot(p[:, 0:256], cs_ref[...]).astype(BF16)
    lru_ref[0] = p[:, 256:768]
    mla_ref[0] = p[:, 768:1280]
    u_ref[0] = p[:, 1280:1536] * p[:, 1792:2048]
    sb_ref[0] = p[:, 1536:1792]


def _in_projection(h, g, sh, sc, w_in_p, cs):
    B, n, D = h.shape
    tm = min(1024, n)
    tile = lambda w: pl.BlockSpec((1, tm, w), lambda b, i: (b, i, 0))
    vec = pl.BlockSpec((1, 1, D), lambda b, i: (b, 0, 0))
    return pl.pallas_call(
        _inproj_kernel,
        out_shape=(jax.ShapeDtypeStruct((B, n, 512), BF16),
                   jax.ShapeDtypeStruct((B, n, 512), F32),
                   jax.ShapeDtypeStruct((B, n, 512), F32),
                   jax.ShapeDtypeStruct((B, n, 256), F32),
                   jax.ShapeDtypeStruct((B, n, 256), F32)),
        grid=(B, n // tm),
        in_specs=[tile(D),
                  pl.BlockSpec((1, D), lambda b, i: (0, 0)),
                  vec, vec,
                  pl.BlockSpec((D, P_PAD), lambda b, i: (0, 0)),
                  pl.BlockSpec((256, 512), lambda b, i: (0, 0))],
        out_specs=(tile(512), tile(512), tile(512), tile(256), tile(256)),
        compiler_params=_cparams(2),
        name="norm_inproj",
    )(h, g.reshape(1, D), sh, sc, w_in_p, cs)


def _fnet_kernel(c_ref, s_ref, z_ref, o_ref):
    z = z_ref[0]
    acc = jnp.dot(c_ref[...], z[:, 0:256], preferred_element_type=F32)
    acc = acc - jnp.dot(s_ref[...], z[:, 256:512], preferred_element_type=F32)
    o_ref[0] = acc.astype(BF16)


def _fourier_positions(zf, cn, sn):
    B, n, _ = zf.shape
    tm = min(1024, n)
    return pl.pallas_call(
        _fnet_kernel,
        out_shape=jax.ShapeDtypeStruct((B, n, 256), BF16),
        grid=(n // tm, B),
        in_specs=[pl.BlockSpec((tm, n), lambda i, b: (i, 0)),
                  pl.BlockSpec((tm, n), lambda i, b: (i, 0)),
                  pl.BlockSpec((1, n, 512), lambda i, b: (b, 0, 0))],
        out_specs=pl.BlockSpec((1, tm, 256), lambda i, b: (b, i, 0)),
        compiler_params=_cparams(2),
        name="fourier_positions",
    )(cn, sn, zf)


def _shift_rows(xcat, off, tc):
    if off == 0:
        return xcat[8:8 + tc]
    rolled = pltpu.roll(xcat, (tc + 16) - (8 + off), axis=0)
    return rolled[0:tc]


def _load_halo(ref, col0, width, start, tc, n):
    main = ref[0, pl.ds(start, tc), col0:col0 + width]
    ps = jnp.maximum(start - 8, 0)
    prev = ref[0, pl.ds(pl.multiple_of(ps, 8), 8), col0:col0 + width]
    prev = jnp.where(start > 0, prev, 0.0)
    ns = jnp.minimum(start + tc, n - 8)
    nxt = ref[0, pl.ds(pl.multiple_of(ns, 8), 8), col0:col0 + width]
    nxt = jnp.where(start + tc < n, nxt, 0.0)
    return jnp.concatenate([prev, main, nxt], axis=0)


def _lru_kernel(x_ref, h0_ref, cw_ref, cb_ref, wai_ref, bai_ref, lam_ref, o_ref, hf_ref, hs_ref, *, tc, n):
    nc = n // tc
    in_group = lax.broadcasted_iota(jnp.int32, (tc, GROUP_W), 0) % SCAN_GROUP

    def direction(d, reverse):
        left = 1 if reverse else 2
        cw = cw_ref[d]
        cb = cb_ref[d]
        wai = wai_ref[d]
        bai = bai_ref[d]
        sp = jax.nn.softplus(-lam_ref[d])

        def chunk(ci, carry):
            c = (nc - 1 - ci) if reverse else ci
            start = pl.multiple_of(c * tc, tc)
            xcat = _load_halo(x_ref, 0, GROUP_W, start, tc, n)
            xc = cb
            for k in range(4):
                xc = xc + cw[k:k + 1, :] * _shift_rows(xcat, k - left, tc)
            gates = jax.nn.sigmoid(_bdot(xc, wai) + bai)
            r = gates[:, 0:GROUP_W]
            gi = gates[:, GROUP_W:2 * GROUP_W]
            a = jnp.exp(-LRU_C * r * sp)
            b = jnp.sqrt(jnp.maximum(1.0 - a * a, 0.0)) * (gi * xc)
            s = 1
            while s < SCAN_GROUP:
                if reverse:
                    a_sh = pltpu.roll(a, tc - s, axis=0)
                    b_sh = pltpu.roll(b, tc - s, axis=0)
                    keep = in_group < (SCAN_GROUP - s)
                else:
                    a_sh = pltpu.roll(a, s, axis=0)
                    b_sh = pltpu.roll(b, s, axis=0)
                    keep = in_group >= s
                b = jnp.where(keep, a * b_sh, 0.0) + b
                a = jnp.where(keep, a * a_sh, a)
                s *= 2
            groups = range(tc // SCAN_GROUP)
            pieces = [None] * len(groups)
            state = carry
            for gi in (reversed(groups) if reverse else groups):
                rows_g = slice(gi * SCAN_GROUP, (gi + 1) * SCAN_GROUP)
                hg = a[rows_g] * state + b[rows_g]
                pieces[gi] = hg
                state = hg[0:1, :] if reverse else hg[SCAN_GROUP - 1:SCAN_GROUP, :]
            h = jnp.concatenate(pieces, axis=0)
            if reverse:
                gate = x_ref[0, pl.ds(start, tc), GROUP_W:2 * GROUP_W]
                tot = hs_ref[pl.ds(start, tc), :] + h
                o_ref[0, pl.ds(start, tc), :] = (jax.nn.gelu(gate) * tot).astype(o_ref.dtype)
                return state
            hs_ref[pl.ds(start, tc), :] = h
            return state

        return lax.fori_loop(0, nc, chunk, h0_ref[0, d])

    hf_ref[0, 0] = direction(0, False)
    hf_ref[0, 1] = direction(1, True)


def _rglru(lru, h0, cw, cb, wai, bai, lam):
    B, n, _ = lru.shape
    tc = min(256, n)
    full = lambda *s: pl.BlockSpec(s, lambda b: (0,) * len(s))
    kern = functools.partial(_lru_kernel, tc=tc, n=n)
    return pl.pallas_call(
        kern,
        out_shape=(jax.ShapeDtypeStruct((B, n, GROUP_W), BF16),
                   jax.ShapeDtypeStruct((B, 2, 1, GROUP_W), F32)),
        grid=(B,),
        in_specs=[pl.BlockSpec((1, n, 512), lambda b: (b, 0, 0)),
                  pl.BlockSpec((1, 2, 1, GROUP_W), lambda b: (b, 0, 0, 0)),
                  full(2, 4, GROUP_W), full(2, 1, GROUP_W), full(2, GROUP_W, 2 * GROUP_W),
                  full(2, 1, 2 * GROUP_W), full(2, 1, GROUP_W)],
        out_specs=(pl.BlockSpec((1, n, GROUP_W), lambda b: (b, 0, 0)),
                   pl.BlockSpec((1, 2, 1, GROUP_W), lambda b: (b, 0, 0, 0))),
        scratch_shapes=[pltpu.VMEM((n, GROUP_W), F32)],
        compiler_params=_cparams(1),
        name="rglru",
    )(lru, h0, cw, cb, wai, bai, lam)


def _sconv_kernel(u_ref, sb_ref, w_ref, b_ref, o_ref, *, tc, n):
    w = w_ref[...]
    bias = b_ref[...]

    def chunk(c, _):
        start = pl.multiple_of(c * tc, tc)
        ucat = _load_halo(u_ref, 0, GROUP_W, start, tc, n)
        y = bias
        for k in range(3):
            y = y + w[k:k + 1, :] * _shift_rows(ucat, k - 1, tc)
        o_ref[0, pl.ds(start, tc), :] = (sb_ref[0, pl.ds(start, tc), :] * y).astype(o_ref.dtype)
        return 0

    lax.fori_loop(0, n // tc, chunk, 0)


def _short_conv(u, sb, w, b):
    B, n, W = u.shape
    tc = min(256, n)
    seq = pl.BlockSpec((1, n, W), lambda i: (i, 0, 0))
    return pl.pallas_call(
        functools.partial(_sconv_kernel, tc=tc, n=n),
        out_shape=jax.ShapeDtypeStruct((B, n, W), BF16),
        grid=(B,),
        in_specs=[seq, seq, pl.BlockSpec((3, W), lambda i: (0, 0)), pl.BlockSpec((1, W), lambda i: (0, 0))],
        out_specs=seq,
        compiler_params=_cparams(1),
        name="short_conv",
    )(u, sb, w, b.reshape(1, W))


def _mla_prep_kernel(x_ref, qg_ref, kg_ref, wq_ref, wk_ref, wv_ref, e2_ref, cos_ref, sin_ref,
                     q_ref, k_ref, v_ref):
    x = x_ref[0]
    cq = x[:, 0:256]
    ckv = x[:, 256:384]
    kr = x[:, 384:448]
    cqn = cq * lax.rsqrt(jnp.mean(cq * cq, axis=-1, keepdims=True) + EPS) * qg_ref[...]
    ckvn = ckv * lax.rsqrt(jnp.mean(ckv * ckv, axis=-1, keepdims=True) + EPS) * kg_ref[...]
    cos = cos_ref[...]
    sin = sin_ref[...]
    q2 = _bdot(cqn, wq_ref[...])
    q = (q2[:, 0:512] * cos + q2[:, 512:1024] * sin) * (ATTN_SCALE * LOG2_E)
    kr2 = _bdot(kr, e2_ref[...])
    ckvb = ckvn.astype(BF16)
    k = _bdot(ckvb, wk_ref[...]) + kr2[:, 0:512] * cos + kr2[:, 512:1024] * sin
    vt = lax.dot_general(wv_ref[...], ckvb, (((1,), (1,)), ((), ())), preferred_element_type=F32)
    tm = x.shape[0]
    ones_rows = jnp.where(lax.broadcasted_iota(jnp.int32, (V_ROWS - MLA_V, tm), 0) == 0, 1.0, 0.0)
    pieces = []
    for h in range(MLA_HEADS):
        q_ref[0, h] = q[:, h * HEAD_PAD:(h + 1) * HEAD_PAD].astype(BF16)
        k_ref[0, h] = k[:, h * HEAD_PAD:(h + 1) * HEAD_PAD].astype(BF16)
        pieces += [vt[h * MLA_V:(h + 1) * MLA_V], ones_rows]
    v_ref[0, 0] = jnp.concatenate(pieces, axis=0).astype(BF16)


def _mla_prep(mla, qg, kg, wq2, wk, wv, e2, cos, sin):
    B, n, _ = mla.shape
    tm = min(256, n)
    HW = MLA_HEADS * HEAD_PAD
    full = lambda *s: pl.BlockSpec(s, lambda b, i: (0,) * len(s))
    return pl.pallas_call(
        _mla_prep_kernel,
        out_shape=(jax.ShapeDtypeStruct((B, MLA_HEADS, n, HEAD_PAD), BF16),
                   jax.ShapeDtypeStruct((B, MLA_HEADS, n, HEAD_PAD), BF16),
                   jax.ShapeDtypeStruct((B, n // tm, MLA_HEADS * V_ROWS, tm), BF16)),
        grid=(B, n // tm),
        in_specs=[pl.BlockSpec((1, tm, 512), lambda b, i: (b, i, 0)),
                  full(1, MLA_Q_LORA), full(1, MLA_KV_LORA),
                  full(MLA_Q_LORA, 2 * HW), full(MLA_KV_LORA, HW), full(MLA_HEADS * MLA_V, MLA_KV_LORA),
                  full(2 * MLA_ROPE, 2 * HW),
                  pl.BlockSpec((tm, HW), lambda b, i: (i, 0)),
                  pl.BlockSpec((tm, HW), lambda b, i: (i, 0))],
        out_specs=(pl.BlockSpec((1, MLA_HEADS, tm, HEAD_PAD), lambda b, i: (b, 0, i, 0)),
                   pl.BlockSpec((1, MLA_HEADS, tm, HEAD_PAD), lambda b, i: (b, 0, i, 0)),
                   pl.BlockSpec((1, 1, MLA_HEADS * V_ROWS, tm), lambda b, i: (b, i, 0, 0))),
        compiler_params=_cparams(2),
        name="mla_prep",
    )(mla, qg.reshape(1, -1), kg.reshape(1, -1), wq2, wk, wv, e2, cos, sin)


def _attn_kernel(q_ref, k_ref, vt_ref, o_ref, st_a, st_b, p_a, p_b, *, tk, n_kt):
    tq = q_ref.shape[2]
    H = MLA_HEADS
    qs = [q_ref[0, h] for h in range(H)]

    def scores(j, h):
        kt = k_ref[0, h, pl.ds(pl.multiple_of(j * tk, tk), tk), :]
        return lax.dot_general(kt, qs[h], (((1,), (1,)), ((), ())), preferred_element_type=F32)

    def weighted_values(j, h, p):
        vt = vt_ref[0, j, h * V_ROWS:(h + 1) * V_ROWS, :]
        return jnp.dot(vt, p, preferred_element_type=F32)

    def step(j, st_cur, st_next, p_prev, p_cur, carry):
        a_prev, m, acc = carry
        jn = jnp.minimum(j + 1, n_kt - 1)
        for h in range(H):
            st_next[h] = scores(jn, h)
        pv = [weighted_values(jnp.maximum(j - 1, 0), h, p_prev[h]) for h in range(H)]
        m_new, alpha = [], []
        for h in range(H):
            st = st_cur[h]
            mh = jnp.maximum(m[h], jnp.max(st, axis=0, keepdims=True))
            alpha.append(jnp.exp2(m[h] - mh))
            m_new.append(mh)
            p_cur[h] = jnp.exp2((st - mh).astype(BF16))
        acc_new = [a_prev[h] * acc[h] + pv[h] for h in range(H)]
        return (tuple(alpha), tuple(m_new), tuple(acc_new))

    even = lambda j, c: step(j, st_a, st_b, p_b, p_a, c)
    odd = lambda j, c: step(j, st_b, st_a, p_a, p_b, c)

    for h in range(H):
        st_a[h] = scores(0, h)
    p_b[...] = jnp.zeros(p_b.shape, BF16)
    per_head = lambda f: tuple(f() for _ in range(H))
    carry = (per_head(lambda: jnp.ones((1, tq), F32)),
             per_head(lambda: jnp.full((1, tq), -jnp.inf, F32)),
             per_head(lambda: jnp.zeros((V_ROWS, tq), F32)))
    carry = lax.fori_loop(0, n_kt // 2, lambda i, c: odd(2 * i + 1, even(2 * i, c)), carry)
    if n_kt % 2:
        carry = even(n_kt - 1, carry)
    p_last = p_a if n_kt % 2 else p_b
    a_last, _, acc = carry
    outs = []
    for h in range(H):
        tot = a_last[h] * acc[h] + weighted_values(n_kt - 1, h, p_last[h])
        outs.append(tot[0:MLA_V] / tot[MLA_V:MLA_V + 1])
    o_ref[0] = jnp.concatenate(outs, axis=0).T.astype(o_ref.dtype)


def _attention(q, k, vt):
    B, H, n, _ = q.shape
    M = k.shape[2]
    n_kt, tk = vt.shape[1], vt.shape[3]
    tq = min(256, n)
    return pl.pallas_call(
        functools.partial(_attn_kernel, tk=tk, n_kt=n_kt),
        out_shape=jax.ShapeDtypeStruct((B, n, H * MLA_V), BF16),
        grid=(B, n // tq),
        in_specs=[pl.BlockSpec((1, H, tq, HEAD_PAD), lambda b, i: (b, 0, i, 0)),
                  pl.BlockSpec((1, H, M, HEAD_PAD), lambda b, i: (b, 0, 0, 0)),
                  pl.BlockSpec((1, n_kt, H * V_ROWS, tk), lambda b, i: (b, 0, 0, 0))],
        out_specs=pl.BlockSpec((1, tq, H * MLA_V), lambda b, i: (b, i, 0)),
        scratch_shapes=[pltpu.VMEM((H, tk, tq), F32), pltpu.VMEM((H, tk, tq), F32),
                        pltpu.VMEM((H, tk, tq), BF16), pltpu.VMEM((H, tk, tq), BF16)],
        compiler_params=_cparams(2),
        name="mla_attention",
    )(q, k, vt)


ROW_TILE = 8


def _store_rows_contiguous(ref, lead, x):
    rows = x.shape[0]
    for j in range(ROW_TILE):
        ref[lead + (pl.ds(j, rows, stride=ROW_TILE), slice(None))] = x[:, j * 128:(j + 1) * 128]


def _load_rows_contiguous(ref, lead, rows, dtype):
    return jnp.concatenate([ref[lead + (pl.ds(j, rows, stride=ROW_TILE), slice(None))].astype(dtype)
                            for j in range(ROW_TILE)], axis=-1)


def _outproj_kernel(a_ref, b_ref, c_ref, d_ref, h_ref, w_ref, g1_ref, ng_ref, sh_ref, sc_ref, rw_ref, rb_ref,
                    hn_ref, f_ref, ti_ref, tg_ref):
    mixed = jnp.concatenate([a_ref[0], b_ref[0], c_ref[0], d_ref[0]], axis=-1)
    hn = h_ref[0] + g1_ref[0] * jnp.dot(mixed, w_ref[...], preferred_element_type=F32)
    hn_ref[0] = hn
    y = hn * lax.rsqrt(jnp.mean(hn * hn, axis=-1, keepdims=True) + EPS) * ng_ref[...]
    f = y * (1.0 + sc_ref[0]) + sh_ref[0]
    _store_rows_contiguous(f_ref, (0,), f)
    nt = (((1,), (1,)), ((), ()))
    rw = rw_ref[...]
    rw_hi = rw.astype(BF16)
    rw_lo = (rw - rw_hi.astype(F32)).astype(BF16)
    f_hi = f.astype(BF16)
    f_lo = (f - f_hi.astype(F32)).astype(BF16)
    logits = (lax.dot_general(rw_hi, f_hi, nt, preferred_element_type=F32)
              + (lax.dot_general(rw_hi, f_lo, nt, preferred_element_type=F32)
                 + lax.dot_general(rw_lo, f_hi, nt, preferred_element_type=F32))) + rb_ref[...]
    tm = logits.shape[1]
    sub = lax.broadcasted_iota(jnp.int32, (N_EXPERTS, tm), 0)
    row = lax.broadcasted_iota(jnp.int32, (8, tm), 0)
    top_v = jnp.full((8, tm), -jnp.inf, F32)
    top_i = jnp.zeros((8, tm), jnp.int32)
    cur = logits
    m0 = None
    for kk in range(TOP_K):
        m = jnp.max(cur, axis=0, keepdims=True)
        sel = jnp.min(jnp.where(cur == m, sub, N_EXPERTS), axis=0, keepdims=True)
        if kk == 0:
            m0 = m
        top_v = jnp.where(row == kk, m, top_v)
        top_i = jnp.where(row == kk, sel, top_i)
        cur = jnp.where(sub == sel, -jnp.inf, cur)
    e = jnp.where(row < TOP_K, jnp.exp(top_v - m0), 0.0)
    ti_ref[0] = top_i
    tg_ref[0] = e / jnp.sum(e, axis=0, keepdims=True)


def _out_projection(a, b, c, d, h, w_out, g1, ng, sh, sc, rw, rb):
    B, n, D = h.shape
    tm = min(1024, n)
    tile = lambda w: pl.BlockSpec((1, tm, w), lambda bb, i: (bb, i, 0))
    vec = pl.BlockSpec((1, 1, D), lambda bb, i: (bb, 0, 0))
    full = lambda *s: pl.BlockSpec(s, lambda bb, i: (0,) * len(s))
    return pl.pallas_call(
        _outproj_kernel,
        out_shape=(jax.ShapeDtypeStruct((B, n, D), F32),
                   jax.ShapeDtypeStruct((B, n * ROW_TILE, 128), F32),
                   jax.ShapeDtypeStruct((B, 8, n), jnp.int32),
                   jax.ShapeDtypeStruct((B, 8, n), F32)),
        grid=(B, n // tm),
        in_specs=[tile(256), tile(256), tile(256), tile(256), tile(D),
                  full(D, D), vec, full(1, D), vec, vec, full(N_EXPERTS, D), full(N_EXPERTS, 1)],
        out_specs=(tile(D), pl.BlockSpec((1, tm * ROW_TILE, 128), lambda bb, i: (bb, i, 0)),
                   pl.BlockSpec((1, 8, tm), lambda bb, i: (bb, 0, i)),
                   pl.BlockSpec((1, 8, tm), lambda bb, i: (bb, 0, i))),
        compiler_params=_cparams(2),
        name="outproj_router",
    )(a, b, c, d, h, w_out, g1, ng.reshape(1, D), sh, sc, rw.T, rb.reshape(N_EXPERTS, 1))


def _rank_kernel(ti_ref, rank_ref, cnt_ref, carry_ref, *, tm):
    @pl.when(pl.program_id(0) == 0)
    def _():
        carry_ref[...] = jnp.zeros_like(carry_ref)

    ti = ti_ref[...]
    sub = lax.broadcasted_iota(jnp.int32, (N_EXPERTS, tm), 0)
    hot = jnp.zeros((N_EXPERTS, tm), F32)
    for kk in range(TOP_K):
        hot = hot + jnp.where(sub == ti[kk:kk + 1, :], 1.0, 0.0)
    r = lax.broadcasted_iota(jnp.int32, (tm, tm), 0)
    c = lax.broadcasted_iota(jnp.int32, (tm, tm), 1)
    earlier = jnp.where(r < c, 1.0, 0.0).astype(BF16)
    carry = carry_ref[:, 0:1]
    before = jnp.dot(hot.astype(BF16), earlier, preferred_element_type=F32) + carry
    row = lax.broadcasted_iota(jnp.int32, (8, tm), 0)
    out = jnp.zeros((8, tm), jnp.int32)
    for kk in range(TOP_K):
        rk = jnp.sum(jnp.where(sub == ti[kk:kk + 1, :], before, 0.0), axis=0, keepdims=True)
        out = jnp.where(row == kk, rk.astype(jnp.int32), out)
    rank_ref[...] = out
    total = carry + jnp.sum(hot, axis=1, keepdims=True)
    carry_ref[...] = jnp.broadcast_to(total, carry_ref.shape)
    cnt_ref[...] = jnp.broadcast_to(total, cnt_ref.shape).astype(jnp.int32)


def _expert_ranks(top_i_t):
    T = top_i_t.shape[1]
    tm = 512 if T % 512 == 0 else 256
    return pl.pallas_call(
        functools.partial(_rank_kernel, tm=tm),
        out_shape=(jax.ShapeDtypeStruct((8, T), jnp.int32), jax.ShapeDtypeStruct((N_EXPERTS, 128), jnp.int32)),
        grid=(T // tm,),
        in_specs=[pl.BlockSpec((8, tm), lambda i: (0, i))],
        out_specs=(pl.BlockSpec((8, tm), lambda i: (0, i)), pl.BlockSpec((N_EXPERTS, 128), lambda i: (0, 0))),
        scratch_shapes=[pltpu.VMEM((N_EXPERTS, 128), F32)],
        compiler_params=_cparams(1),
        name="moe_rank",
    )(top_i_t)


def _stream_maps(nc, nl):
    per_batch = nc + nl
    ctx_map = lambda i, *_: ((i // per_batch) * nc + jnp.minimum(i % per_batch, nc - 1), 0)
    lat_map = lambda i, *_: ((i // per_batch) * nl + jnp.maximum(i % per_batch - nc, 0), 0)
    return [ctx_map, lat_map] if nc > 0 else [lat_map]


def _dispatch_kernel(cnt_ref, ps_ref, pe_ref, pos_ref, *refs, tm, n_slots, nc, nl):
    n_in = 2 if nc > 0 else 1
    f_refs = refs[:n_in]
    xs_ref, buf_ref, zrow_ref, sem_ref, zsem_ref = refs[n_in:]
    i = pl.program_id(0)
    nt = pl.num_programs(0)
    copies = tm * TOP_K
    zslots = zrow_ref.shape[0] // ROW_TILE

    def tile_rows(first_row, n_rows):
        return pl.ds(pl.multiple_of(first_row * ROW_TILE, ROW_TILE), n_rows * ROW_TILE)

    def zero_fill(lo, hi, start):
        def chunk(off, size):
            cp = pltpu.make_async_copy(zrow_ref.at[tile_rows(0, size)], xs_ref.at[tile_rows(off, size)], zsem_ref)
            cp.start() if start else cp.wait()

        n = hi - lo

        def whole(j, _):
            chunk(lo + j * zslots, zslots)
            return 0

        lax.fori_loop(0, n // zslots, whole, 0)
        rem_lo = lo + (n // zslots) * zslots
        rem = n % zslots
        size = zslots // 2
        while size >= 1:
            @pl.when((rem & size) != 0)
            def _(size=size):
                chunk(rem_lo + (rem & ~(2 * size - 1)), size)
            size //= 2

    @pl.when(i == 0)
    def _():
        zrow_ref[...] = jnp.zeros_like(zrow_ref)
        for start in (True, False):
            for e in range(N_EXPERTS):
                zero_fill(ps_ref[e] + cnt_ref[e], pe_ref[e], start)
            zero_fill(pe_ref[N_EXPERTS - 1], n_slots, start)

    slot = i % 2
    if nc > 0:
        j = i % (nc + nl)

        @pl.when(j < nc)
        def _():
            buf_ref[slot] = f_refs[0][...]

        @pl.when(j >= nc)
        def _():
            buf_ref[slot] = f_refs[1][...]
    else:
        buf_ref[slot] = f_refs[0][...]

    def issue(t, _):
        for kk in range(TOP_K):
            pltpu.make_async_copy(buf_ref.at[slot, tile_rows(t, 1)],
                                  xs_ref.at[tile_rows(pos_ref[0, 0, t * TOP_K + kk], 1)],
                                  sem_ref.at[slot]).start(priority=kk % 2)
        return 0

    lax.fori_loop(0, tm, issue, 0, unroll=2)

    def wait_rows(s):
        pltpu.make_async_copy(xs_ref.at[tile_rows(0, copies)], xs_ref.at[tile_rows(0, copies)], sem_ref.at[s]).wait()

    @pl.when(i > 0)
    def _():
        wait_rows(1 - slot)

    @pl.when(i == nt - 1)
    def _():
        wait_rows(slot)


def _dispatch(fs, pos, counts, pad_start, pad_end, n_slots, tm, nc, nl):
    nt = pos.shape[0] // (tm * TOP_K)
    grid_spec = pltpu.PrefetchScalarGridSpec(
        num_scalar_prefetch=3,
        grid=(nt,),
        in_specs=[pl.BlockSpec((1, 1, tm * TOP_K), lambda i, *_: (i, 0, 0), memory_space=pltpu.SMEM)]
        + [pl.BlockSpec((tm * ROW_TILE, 128), m) for m in _stream_maps(nc, nl)],
        out_specs=pl.BlockSpec(memory_space=pl.ANY),
        scratch_shapes=[pltpu.VMEM((2, tm * ROW_TILE, 128), F32), pltpu.VMEM((256 * ROW_TILE, 128), F32),
                        pltpu.SemaphoreType.DMA((2,)), pltpu.SemaphoreType.DMA],
    )
    return pl.pallas_call(
        functools.partial(_dispatch_kernel, tm=tm, n_slots=n_slots, nc=nc, nl=nl),
        out_shape=jax.ShapeDtypeStruct((n_slots * ROW_TILE, 128), F32),
        grid_spec=grid_spec,
        compiler_params=_cparams(1),
        name="moe_dispatch",
    )(counts, pad_start, pad_end, pos.reshape(nt, 1, tm * TOP_K), *fs)


def _ffn_kernel(be_ref, bv_ref, nx_ref, par_ref, x_ref, wgu_hbm, bgu_ref, wd_hbm, bd_ref, y_ref,
                wgu_f, wd_f, wgu_s, wd_s, wsem, *, bm, layer):
    i = pl.program_id(0)
    valid = bv_ref[i]

    def weights(e, slot, start):
        for src, dst in ((wgu_hbm, wgu_f), (wd_hbm, wd_f)):
            cp = pltpu.make_async_copy(src.at[layer, e], dst.at[slot], wsem.at[slot])
            cp.start() if start else cp.wait()

    def experts_rows(rows):
        De = wd_s.shape[0]
        x = _load_rows_contiguous(x_ref, (), rows, BF16)
        gu = jnp.dot(x, wgu_s[...], preferred_element_type=F32) + bgu_ref[0, 0]
        g = jnp.minimum(gu[:, 0:De], SWIGLU_LIMIT)
        u = jnp.clip(gu[:, De:2 * De], -SWIGLU_LIMIT, SWIGLU_LIMIT)
        act = (u + 1.0) * (g * jax.nn.sigmoid(SWIGLU_ALPHA * g))
        y = jnp.dot(act.astype(BF16), wd_s[...], preferred_element_type=F32) + bd_ref[0, 0]
        _store_rows_contiguous(y_ref, (), y)
        if rows < bm:
            y_ref[rows * ROW_TILE:bm * ROW_TILE, :] = jnp.zeros(((bm - rows) * ROW_TILE, 128), y_ref.dtype)

    @pl.when(valid > 0)
    def _():
        @pl.when((i == 0) | (be_ref[i] != be_ref[jnp.maximum(i - 1, 0)]))
        def _():
            slot = par_ref[i]

            @pl.when(i == 0)
            def _():
                weights(be_ref[i], slot, True)

            weights(be_ref[i], slot, False)

            @pl.when(nx_ref[i] >= 0)
            def _():
                weights(nx_ref[i], 1 - slot, True)

            wgu_s[...] = wgu_f[slot].astype(BF16)
            wd_s[...] = wd_f[slot].astype(BF16)

        for rows in range(FFN_ROWS, bm + 1, FFN_ROWS):
            pl.when((valid > rows - FFN_ROWS) & (valid <= rows))(functools.partial(experts_rows, rows))

    @pl.when(valid <= 0)
    def _():
        y_ref[...] = jnp.zeros_like(y_ref)


def _expert_ffn(xs, block_exp, block_valid, next_exp, parity, wgu, bgu, wd, bd, layer, bm):
    n_slots = xs.shape[0] // ROW_TILE
    L, E, D, G = wgu.shape
    nb = n_slots // bm
    bmap = lambda i, be, *_: (layer, be[i], 0, 0)
    rows_spec = pl.BlockSpec((bm * ROW_TILE, 128), lambda i, *_: (i, 0))
    grid_spec = pltpu.PrefetchScalarGridSpec(
        num_scalar_prefetch=4,
        grid=(nb,),
        in_specs=[rows_spec,
                  pl.BlockSpec(memory_space=pl.ANY),
                  pl.BlockSpec((1, 1, 1, G), bmap),
                  pl.BlockSpec(memory_space=pl.ANY),
                  pl.BlockSpec((1, 1, 1, D), bmap)],
        out_specs=rows_spec,
        scratch_shapes=[pltpu.VMEM((2, D, G), F32), pltpu.VMEM((2, G // 2, D), F32),
                        pltpu.VMEM((D, G), BF16), pltpu.VMEM((G // 2, D), BF16),
                        pltpu.SemaphoreType.DMA((2,))],
    )
    return pl.pallas_call(
        functools.partial(_ffn_kernel, bm=bm, layer=layer),
        out_shape=jax.ShapeDtypeStruct((n_slots * ROW_TILE, 128), F32),
        grid_spec=grid_spec,
        compiler_params=_cparams(1),
        name="moe_expert_ffn",
    )(block_exp, block_valid, next_exp, parity, xs, wgu, bgu.reshape(L, E, 1, G), wd, bd.reshape(L, E, 1, D))


def _combine_kernel(pos_ref, pos_next_ref, y_ref, *refs, tm, nc, nl, final_norm):
    n_s = 2 if nc > 0 else 1
    h_refs = refs[:n_s]
    g2_ref, gate_ref, fg_ref = refs[n_s:n_s + 3]
    o_refs = refs[n_s + 3:2 * n_s + 3]
    buf_ref, sem_ref = refs[2 * n_s + 3:]
    i = pl.program_id(0)
    nt = pl.num_programs(0)
    slot = i % 2

    def tile_rows(first_row):
        return pl.ds(pl.multiple_of(first_row * ROW_TILE, ROW_TILE), ROW_TILE)

    def gather_tile(p_ref, s):
        def issue(r, _):
            for kk in range(TOP_K):
                pltpu.make_async_copy(y_ref.at[tile_rows(p_ref[0, 0, r * TOP_K + kk])],
                                      buf_ref.at[s, kk, tile_rows(r)], sem_ref.at[s]).start(priority=kk % 2)
            return 0

        lax.fori_loop(0, tm, issue, 0, unroll=2)

    @pl.when(i == 0)
    def _():
        gather_tile(pos_ref, 0)

    @pl.when(i + 1 < nt)
    def _():
        gather_tile(pos_next_ref, 1 - slot)

    pltpu.make_async_copy(buf_ref.at[slot], buf_ref.at[slot], sem_ref.at[slot]).wait()
    gate = gate_ref[...]
    acc = gate[:, 0:1] * _load_rows_contiguous(buf_ref, (slot, 0), tm, F32)
    for kk in range(1, TOP_K):
        acc = acc + gate[:, kk:kk + 1] * _load_rows_contiguous(buf_ref, (slot, kk), tm, F32)
    moe = g2_ref[0] * acc

    def finish(h_ref, o_ref):
        out = h_ref[...] + moe
        if final_norm:
            out = out * lax.rsqrt(jnp.mean(out * out, axis=-1, keepdims=True) + EPS) * fg_ref[...]
        o_ref[...] = out

    if nc > 0:
        j = i % (nc + nl)
        pl.when(j < nc)(lambda: finish(h_refs[0], o_refs[0]))
        pl.when(j >= nc)(lambda: finish(h_refs[1], o_refs[1]))
    else:
        finish(h_refs[0], o_refs[0])


def _combine(y, pos, hs, g2_tiles, gates, final_g, tm, nc, nl, final_norm):
    D = hs[0].shape[1]
    nt = pos.shape[0] // (tm * TOP_K)
    maps = _stream_maps(nc, nl)
    streams = [pl.BlockSpec((tm, D), m) for m in maps]
    pos3 = pos.reshape(nt, 1, tm * TOP_K)
    outs = pl.pallas_call(
        functools.partial(_combine_kernel, tm=tm, nc=nc, nl=nl, final_norm=final_norm),
        out_shape=[jax.ShapeDtypeStruct(h.shape, F32) for h in hs],
        grid=(nt,),
        in_specs=[pl.BlockSpec((1, 1, tm * TOP_K), lambda i: (i, 0, 0), memory_space=pltpu.SMEM),
                  pl.BlockSpec((1, 1, tm * TOP_K), lambda i: (jnp.minimum(i + 1, nt - 1), 0, 0),
                               memory_space=pltpu.SMEM),
                  pl.BlockSpec(memory_space=pl.ANY)]
        + streams
        + [pl.BlockSpec((1, 1, D), lambda i: (i, 0, 0)),
           pl.BlockSpec((tm, 8), lambda i: (i, 0)),
           pl.BlockSpec((1, D), lambda i: (0, 0))],
        out_specs=streams,
        scratch_shapes=[pltpu.VMEM((2, TOP_K, tm * ROW_TILE, 128), F32), pltpu.SemaphoreType.DMA((2,))],
        compiler_params=_cparams(1),
        name="moe_combine",
    )(pos3, pos3, y, *hs, g2_tiles, gates, final_g.reshape(1, D))
    return outs


def _slots_kernel(ps_ref, ti_ref, rank_ref, pos_ref):
    ti = ti_ref[...]
    start = jnp.zeros(ti.shape, jnp.int32)
    for e in range(N_EXPERTS):
        start = jnp.where(ti == e, ps_ref[e], start)
    pos_ref[...] = start + rank_ref[...]


def _slots(pad_start, top_i_t, rank_t):
    shape = top_i_t.shape
    whole = pl.BlockSpec(shape, lambda i, ps: (0, 0))
    return pl.pallas_call(
        _slots_kernel,
        out_shape=jax.ShapeDtypeStruct(shape, jnp.int32),
        grid_spec=pltpu.PrefetchScalarGridSpec(num_scalar_prefetch=1, grid=(1,), in_specs=[whole, whole],
                                               out_specs=whole),
        compiler_params=_cparams(1),
        name="moe_slots",
    )(pad_start, top_i_t, rank_t)


def _moe(fs, hs, top_i_t, top_g_t, g2_tiles, wgu, bgu, wd, bd, layer, final_g, tm, nc, nl, final_norm):
    T = top_i_t.shape[1]
    bm = MOE_BM
    n_asg = T * TOP_K
    rank_t, cnt = _expert_ranks(top_i_t)
    counts = cnt[:, 0]
    padded = (counts + bm - 1) // bm * bm
    pad_end = jnp.cumsum(padded).astype(jnp.int32)
    pad_start = pad_end - padded
    pos = _slots(pad_start, top_i_t, rank_t)[0:TOP_K].T.reshape(-1)
    nb = (n_asg + N_EXPERTS * (bm - 1) + bm - 1) // bm
    n_slots = nb * bm
    starts = jnp.arange(nb, dtype=jnp.int32) * bm
    block_exp = jnp.minimum(jnp.sum((pad_end[None, :] <= starts[:, None]).astype(jnp.int32), axis=1), N_EXPERTS - 1)
    block_valid = jnp.clip(pad_start[block_exp] + counts[block_exp] - starts, 0, bm)
    block_valid = jnp.where(starts < pad_end[-1], block_valid, 0).astype(jnp.int32)
    ids = jnp.arange(N_EXPERTS, dtype=jnp.int32)
    used = counts > 0
    later_used = used[None, :] & (ids[None, :] > ids[:, None])
    next_of = jnp.min(jnp.where(later_used, ids[None, :], N_EXPERTS), axis=1)
    next_of = jnp.where(next_of < N_EXPERTS, next_of, -1).astype(jnp.int32)
    order_of = (jnp.cumsum(used.astype(jnp.int32)) - 1).astype(jnp.int32)
    hot = (block_exp[:, None] == ids[None, :]).astype(jnp.int32)
    next_exp = jnp.sum(hot * next_of[None, :], axis=1).astype(jnp.int32)
    parity = (jnp.sum(hot * order_of[None, :], axis=1) % 2).astype(jnp.int32)

    xs = _dispatch(fs, pos, counts, pad_start, pad_end, n_slots, tm, nc, nl)
    y = _expert_ffn(xs, block_exp, block_valid, next_exp, parity, wgu, bgu, wd, bd, layer, bm)
    return _combine(y, pos, hs, g2_tiles, top_g_t.T, final_g, tm, nc, nl, final_norm)


_ROPE_SWAP = np.concatenate([np.arange(8, 16), np.arange(0, 8), np.arange(24, 32), np.arange(16, 24)])


def _dft_tables(n):
    hi = n // 64
    j = jnp.arange(n, dtype=jnp.int32)[:, None]
    k = jnp.arange(64, dtype=jnp.int32)[None, :]
    k2 = jnp.arange(hi, dtype=jnp.int32)[None, :]
    alpha = ((j * k2) % hi).astype(F32) * (2.0 * math.pi / hi)
    beta = ((j * k) % n).astype(F32) * (2.0 * math.pi / n)
    ca, sa, cb, sb = lax.optimization_barrier((jnp.cos(alpha), jnp.sin(alpha), jnp.cos(beta), jnp.sin(beta)))
    ca, sa = ca[:, :, None], sa[:, :, None]
    cb, sb = cb[:, None, :], sb[:, None, :]
    scale = 1.0 / math.sqrt(n)
    cn = ((ca * cb - sa * sb) * scale).reshape(n, n).astype(BF16)
    sn = ((sa * cb + ca * sb) * scale).reshape(n, n).astype(BF16)
    return cn, sn


def _channel_dft():
    j = jnp.arange(GROUP_W, dtype=jnp.int32)
    same_group = (j[:, None] // FNET_CH) == (j[None, :] // FNET_CH)
    ang = ((j[:, None] * j[None, :]) % FNET_CH).astype(F32) * (2.0 * math.pi / FNET_CH)
    scale = 1.0 / math.sqrt(FNET_CH)
    c = jnp.where(same_group, jnp.cos(ang) * scale, 0.0)
    s = jnp.where(same_group, jnp.sin(ang) * scale, 0.0)
    return jnp.concatenate([c, s], axis=1).astype(BF16)


def _rope_tables(n, rotary):
    HW = MLA_HEADS * HEAD_PAD
    cos_blk = np.zeros((HEAD_PAD,), np.float32)
    cos_blk[0:MLA_NOPE + MLA_ROPE] = 1.0
    if not rotary:
        return (jnp.broadcast_to(jnp.asarray(np.tile(cos_blk, MLA_HEADS)), (n, HW)),
                jnp.zeros((n, HW), F32))
    half = MLA_ROPE // 2
    inv = ROPE_BASE ** (-jnp.arange(0, half, 2, dtype=F32) / half)
    t = jnp.arange(n, dtype=jnp.int32)
    row = (t // GRID_W).astype(F32)
    col = (t % GRID_W).astype(F32)
    ar = row[:, None] * inv[None, :]
    ac = col[:, None] * inv[None, :]
    cos32 = jnp.concatenate([jnp.cos(ar), jnp.cos(ar), jnp.cos(ac), jnp.cos(ac)], axis=1)
    sin32 = jnp.concatenate([-jnp.sin(ar), jnp.sin(ar), -jnp.sin(ac), jnp.sin(ac)], axis=1)
    ones = jnp.ones((n, MLA_NOPE), F32)
    zeros_n = jnp.zeros((n, MLA_NOPE), F32)
    zeros_p = jnp.zeros((n, HEAD_PAD - MLA_NOPE - MLA_ROPE), F32)
    cos_h = jnp.concatenate([ones, cos32, zeros_p], axis=1)
    sin_h = jnp.concatenate([zeros_n, sin32, zeros_p], axis=1)
    return jnp.tile(cos_h, (1, MLA_HEADS)), jnp.tile(sin_h, (1, MLA_HEADS))


def _relayout_w_in(w_in):
    D = w_in.shape[0]
    krope = w_in[:, 1152:1184]
    return jnp.concatenate([w_in[:, 0:1184], krope[:, _ROPE_SWAP], jnp.zeros((D, 64), w_in.dtype),
                            w_in[:, 1184:1952]], axis=1).astype(BF16)


def _relayout_mla(w_uq, w_ukv):
    dq = MLA_NOPE + MLA_ROPE
    q_main, q_swap, k_cols, v_cols = [], [], [], []
    zq = jnp.zeros((MLA_Q_LORA, HEAD_PAD - dq), w_uq.dtype)
    zn = jnp.zeros((MLA_Q_LORA, MLA_NOPE), w_uq.dtype)
    zk = jnp.zeros((MLA_KV_LORA, HEAD_PAD - MLA_NOPE), w_ukv.dtype)
    for h in range(MLA_HEADS):
        wq_h = w_uq[:, h * dq:(h + 1) * dq]
        rope = wq_h[:, MLA_NOPE:]
        q_main += [wq_h, zq]
        q_swap += [zn, rope[:, _ROPE_SWAP], zq]
        kv_h = w_ukv[:, h * (MLA_NOPE + MLA_V):(h + 1) * (MLA_NOPE + MLA_V)]
        k_cols += [kv_h[:, 0:MLA_NOPE], zk]
        v_cols += [kv_h[:, MLA_NOPE:]]
    wq2 = jnp.concatenate(q_main + q_swap, axis=1).astype(BF16)
    wk = jnp.concatenate(k_cols, axis=1).astype(BF16)
    wv = jnp.concatenate(v_cols, axis=1).T.astype(BF16)
    HW = MLA_HEADS * HEAD_PAD
    e2 = np.zeros((2 * MLA_ROPE, 2 * HW), np.float32)
    for h in range(MLA_HEADS):
        for jj in range(MLA_ROPE):
            e2[jj, h * HEAD_PAD + MLA_NOPE + jj] = 1.0
            e2[MLA_ROPE + jj, HW + h * HEAD_PAD + MLA_NOPE + jj] = 1.0
    return wq2, wk, wv, jnp.asarray(e2, BF16)


def _block_diag_heads(w):
    H, d, _ = w.shape
    eye = jnp.eye(H, dtype=w.dtype)
    return (eye[:, None, :, None] * w[:, :, None, :]).reshape(H * d, H * d)


def kernel(x, c, ctx, c_ctx, mod_w, mod_b, norm1_g, norm2_g, w_in, lru_conv_w, lru_conv_b, lru_w_a, lru_b_a, lru_w_i, lru_b_i, lru_lambda, mla_q_norm_g, mla_w_uq, mla_kv_norm_g, mla_w_ukv, sc_conv_w, sc_conv_b, w_out, router_w, router_b, exp_w_gu, exp_b_gu, exp_w_down, exp_b_down, final_g):
    B, n, D = x.shape
    n_ctx = ctx.shape[1]
    depth = mod_w.shape[0]
    tm = min(256, n, n_ctx)

    rows = 8 * ((B + 1 + 7) // 8)
    c_all = jnp.concatenate([c, c_ctx[None, :], jnp.zeros((rows - B - 1, D), F32)], axis=0)
    mod = _modulation(c_all, mod_w, mod_b)

    cs = _channel_dft()
    cn_lat, sn_lat = _dft_tables(n)
    cn_ctx, sn_ctx = _dft_tables(n_ctx)
    cos_lat, sin_lat = _rope_tables(n, True)
    cos_ctx, sin_ctx = _rope_tables(n_ctx, False)

    h_lat, h_ctx = x, ctx
    for l in range(depth):
        with_ctx_out = l < depth - 1
        ml = mod[l, 0:B].reshape(B, 1, 6, D)
        mc = jnp.broadcast_to(mod[l, B:B + 1].reshape(1, 1, 6, D), (B, 1, 6, D))
        part = lambda m, i: m[:, :, i, :]
        w_in_p = _relayout_w_in(w_in[l])
        wq2, wk, wv, e2 = _relayout_mla(mla_w_uq[l], mla_w_ukv[l])
        wai = jnp.stack([jnp.concatenate([_block_diag_heads(lru_w_a[l, d]), _block_diag_heads(lru_w_i[l, d])], axis=1)
                         for d in range(2)]).astype(BF16)
        bai = jnp.concatenate([lru_b_a[l], lru_b_i[l]], axis=1)[:, None, :]
        lru_args = (lru_conv_w[l], lru_conv_b[l][:, None, :], wai, bai, lru_lambda[l][:, None, :])
        w_out_b = w_out[l].astype(BF16)

        zf_c, lru_c, mla_c, u_c, sb_c = _in_projection(h_ctx, norm1_g[l], part(mc, 0), part(mc, 1), w_in_p, cs)
        zf_l, lru_l, mla_l, u_l, sb_l = _in_projection(h_lat, norm1_g[l], part(ml, 0), part(ml, 1), w_in_p, cs)

        h0 = jnp.zeros((B, 2, 1, GROUP_W), F32)
        b_ctx, s_ctx = _rglru(lru_c, h0, *lru_args)
        b_lat, _ = _rglru(lru_l, s_ctx, *lru_args)

        q_c, k_c, v_c = _mla_prep(mla_c, mla_q_norm_g[l], mla_kv_norm_g[l], wq2, wk, wv, e2, cos_ctx, sin_ctx)
        q_l, k_l, v_l = _mla_prep(mla_l, mla_q_norm_g[l], mla_kv_norm_g[l], wq2, wk, wv, e2, cos_lat, sin_lat)
        k_all = jnp.concatenate([k_c, k_l], axis=2)
        v_all = jnp.concatenate([v_c, v_l], axis=1)
        c_lat = _attention(q_l, k_all, v_all)

        a_lat = _fourier_positions(zf_l, cn_lat, sn_lat)
        d_lat = _short_conv(u_l, sb_l, sc_conv_w[l], sc_conv_b[l])

        hn_lat, f_lat, ti_lat, tg_lat = _out_projection(
            a_lat, b_lat, c_lat, d_lat, h_lat, w_out_b, part(ml, 2), norm2_g[l], part(ml, 3), part(ml, 4),
            router_w[l], router_b[l])

        moe_w = (exp_w_gu, exp_b_gu, exp_w_down, exp_b_down, l)
        tokens_on_lanes = lambda u: jnp.transpose(u, (1, 0, 2)).reshape(8, -1)
        rows = lambda u, w: u.reshape(-1, w)
        g2_lat = part(ml, 5)
        if with_ctx_out:
            a_ctx = _fourier_positions(zf_c, cn_ctx, sn_ctx)
            c_ctx_out = _attention(q_c, k_c, v_c)
            d_ctx = _short_conv(u_c, sb_c, sc_conv_w[l], sc_conv_b[l])
            hn_ctx, f_ctx, ti_ctx, tg_ctx = _out_projection(
                a_ctx, b_ctx, c_ctx_out, d_ctx, h_ctx, w_out_b, part(mc, 2), norm2_g[l], part(mc, 3), part(mc, 4),
                router_w[l], router_b[l])
            S = n_ctx + n
            g2_rows = jnp.concatenate([jnp.broadcast_to(part(mc, 5), (B, n_ctx // tm, D)),
                                       jnp.broadcast_to(g2_lat, (B, n // tm, D))], axis=1)
            cat_t = lambda u, v: tokens_on_lanes(jnp.concatenate([u, v], axis=2))
            h_ctx, h_lat = _moe([rows(f_ctx, 128), rows(f_lat, 128)], [rows(hn_ctx, D), rows(hn_lat, D)],
                                cat_t(ti_ctx, ti_lat), cat_t(tg_ctx, tg_lat),
                                g2_rows.reshape(B * S // tm, 1, D), *moe_w, final_g, tm, n_ctx // tm, n // tm, False)
            h_ctx, h_lat = h_ctx.reshape(B, n_ctx, D), h_lat.reshape(B, n, D)
        else:
            g2_rows = jnp.broadcast_to(g2_lat, (B, n // tm, D)).reshape(B * n // tm, 1, D)
            (h_lat,) = _moe([rows(f_lat, 128)], [rows(hn_lat, D)],
                            tokens_on_lanes(ti_lat), tokens_on_lanes(tg_lat), g2_rows,
                            *moe_w, final_g, tm, 0, n // tm, True)
            h_lat = h_lat.reshape(B, n, D)

    return h_lat
```

```python
import functools
import math

import numpy as np
import jax
import jax.numpy as jnp
from jax import lax
from jax.experimental import pallas as pl
from jax.experimental.pallas import tpu as pltpu

F32 = jnp.float32
BF16 = jnp.bfloat16

D_MODEL = 1024
GRID_W = 64
GROUP_W = 256
EPS = 1e-6
FNET_CH = 64
LRU_HEADS = 4
LRU_C = 8.0
MLA_HEADS = 4
MLA_NOPE = 64
MLA_ROPE = 32
MLA_V = 64
MLA_Q_LORA = 256
MLA_KV_LORA = 128
ROPE_BASE = 10000.0
ATTN_SCALE = (MLA_NOPE + MLA_ROPE) ** -0.5
LOG2_E = math.log2(math.e)
N_EXPERTS = 32
TOP_K = 4
SWIGLU_ALPHA = 1.702
SWIGLU_LIMIT = 7.0

HEAD_PAD = 128
V_ROWS = 80
P_PAD = 2048
MOE_BM = 768
FFN_ROWS = 256
SCAN_GROUP = 8
VMEM_LIMIT = 56 * 1024 * 1024


def _cparams(n_axes, vmem=None):
    return pltpu.CompilerParams(dimension_semantics=("arbitrary",) * n_axes,
                                vmem_limit_bytes=vmem or VMEM_LIMIT)


def _bdot(a, b):
    return jnp.dot(a.astype(BF16), b.astype(BF16), preferred_element_type=F32)


def _mod_kernel(c_ref, w_ref, b_ref, o_ref):
    c = c_ref[...]
    s = c * jax.nn.sigmoid(c)
    o_ref[0] = _bdot(s, w_ref[0]) + b_ref[0]


def _modulation(c_all, mod_w, mod_b):
    L, D, N = mod_w.shape
    R = c_all.shape[0]
    tn = 1024
    return pl.pallas_call(
        _mod_kernel,
        out_shape=jax.ShapeDtypeStruct((L, R, N), F32),
        grid=(L, N // tn),
        in_specs=[pl.BlockSpec((R, D), lambda l, j: (0, 0)),
                  pl.BlockSpec((1, D, tn), lambda l, j: (l, 0, j)),
                  pl.BlockSpec((1, 1, tn), lambda l, j: (l, 0, j))],
        out_specs=pl.BlockSpec((1, R, tn), lambda l, j: (l, 0, j)),
        compiler_params=_cparams(2),
        name="adaln_mod",
    )(c_all, mod_w, mod_b.reshape(L, 1, N))


def _inproj_kernel(h_ref, g_ref, sh_ref, sc_ref, w_ref, cs_ref, zf_ref, lru_ref, mla_ref, u_ref, sb_ref):
    x = h_ref[0]
    y = x * lax.rsqrt(jnp.mean(x * x, axis=-1, keepdims=True) + EPS) * g_ref[...]
    a = y * (1.0 + sc_ref[0]) + sh_ref[0]
    p = _bdot(a, w_ref[...])
    zf_ref[0] = _bdot(p[:, 0:256], cs_ref[...]).astype(BF16)
    lru_ref[0] = p[:, 256:768]
    mla_ref[0] = p[:, 768:1280]
    u_ref[0] = p[:, 1280:1536] * p[:, 1792:2048]
    sb_ref[0] = p[:, 1536:1792]


def _in_projection(h, g, sh, sc, w_in_p, cs):
    B, n, D = h.shape
    tm = min(1024, n)
    tile = lambda w: pl.BlockSpec((1, tm, w), lambda b, i: (b, i, 0))
    vec = pl.BlockSpec((1, 1, D), lambda b, i: (b, 0, 0))
    return pl.pallas_call(
        _inproj_kernel,
        out_shape=(jax.ShapeDtypeStruct((B, n, 512), BF16),
                   jax.ShapeDtypeStruct((B, n, 512), F32),
                   jax.ShapeDtypeStruct((B, n, 512), F32),
                   jax.ShapeDtypeStruct((B, n, 256), F32),
                   jax.ShapeDtypeStruct((B, n, 256), F32)),
        grid=(B, n // tm),
        in_specs=[tile(D),
                  pl.BlockSpec((1, D), lambda b, i: (0, 0)),
                  vec, vec,
                  pl.BlockSpec((D, P_PAD), lambda b, i: (0, 0)),
                  pl.BlockSpec((256, 512), lambda b, i: (0, 0))],
        out_specs=(tile(512), tile(512), tile(512), tile(256), tile(256)),
        compiler_params=_cparams(2),
        name="norm_inproj",
    )(h, g.reshape(1, D), sh, sc, w_in_p, cs)


def _fnet_kernel(c_ref, s_ref, z_ref, o_ref):
    z = z_ref[0]
    acc = jnp.dot(c_ref[...], z[:, 0:256], preferred_element_type=F32)
    acc = acc - jnp.dot(s_ref[...], z[:, 256:512], preferred_element_type=F32)
    o_ref[0] = acc.astype(BF16)


def _fourier_positions(zf, cn, sn):
    B, n, _ = zf.shape
    tm = min(1024, n)
    return pl.pallas_call(
        _fnet_kernel,
        out_shape=jax.ShapeDtypeStruct((B, n, 256), BF16),
        grid=(n // tm, B),
        in_specs=[pl.BlockSpec((tm, n), lambda i, b: (i, 0)),
                  pl.BlockSpec((tm, n), lambda i, b: (i, 0)),
                  pl.BlockSpec((1, n, 512), lambda i, b: (b, 0, 0))],
        out_specs=pl.BlockSpec((1, tm, 256), lambda i, b: (b, i, 0)),
        compiler_params=_cparams(2),
        name="fourier_positions",
    )(cn, sn, zf)


def _shift_rows(xcat, off, tc):
    if off == 0:
        return xcat[8:8 + tc]
    rolled = pltpu.roll(xcat, (tc + 16) - (8 + off), axis=0)
    return rolled[0:tc]


def _load_halo(ref, col0, width, start, tc, n):
    main = ref[0, pl.ds(start, tc), col0:col0 + width]
    ps = jnp.maximum(start - 8, 0)
    prev = ref[0, pl.ds(pl.multiple_of(ps, 8), 8), col0:col0 + width]
    prev = jnp.where(start > 0, prev, 0.0)
    ns = jnp.minimum(start + tc, n - 8)
    nxt = ref[0, pl.ds(pl.multiple_of(ns, 8), 8), col0:col0 + width]
    nxt = jnp.where(start + tc < n, nxt, 0.0)
    return jnp.concatenate([prev, main, nxt], axis=0)


def _lru_kernel(x_ref, h0_ref, cw_ref, cb_ref, wai_ref, bai_ref, lam_ref, o_ref, hf_ref, hs_ref, *, tc, n):
    nc = n // tc
    in_group = lax.broadcasted_iota(jnp.int32, (tc, GROUP_W), 0) % SCAN_GROUP

    def direction(d, reverse):
        left = 1 if reverse else 2
        cw = cw_ref[d]
        cb = cb_ref[d]
        wai = wai_ref[d]
        bai = bai_ref[d]
        sp = jax.nn.softplus(-lam_ref[d])

        def chunk(ci, carry):
            c = (nc - 1 - ci) if reverse else ci
            start = pl.multiple_of(c * tc, tc)
            xcat = _load_halo(x_ref, 0, GROUP_W, start, tc, n)
            xc = cb
            for k in range(4):
                xc = xc + cw[k:k + 1, :] * _shift_rows(xcat, k - left, tc)
            gates = jax.nn.sigmoid(_bdot(xc, wai) + bai)
            r = gates[:, 0:GROUP_W]
            gi = gates[:, GROUP_W:2 * GROUP_W]
            a = jnp.exp(-LRU_C * r * sp)
            b = jnp.sqrt(jnp.maximum(1.0 - a * a, 0.0)) * (gi * xc)
            s = 1
            while s < SCAN_GROUP:
                if reverse:
                    a_sh = pltpu.roll(a, tc - s, axis=0)
                    b_sh = pltpu.roll(b, tc - s, axis=0)
                    keep = in_group < (SCAN_GROUP - s)
                else:
                    a_sh = pltpu.roll(a, s, axis=0)
                    b_sh = pltpu.roll(b, s, axis=0)
                    keep = in_group >= s
                b = jnp.where(keep, a * b_sh, 0.0) + b
                a = jnp.where(keep, a * a_sh, a)
                s *= 2
            groups = range(tc // SCAN_GROUP)
            pieces = [None] * len(groups)
            state = carry
            for gi in (reversed(groups) if reverse else groups):
                rows_g = slice(gi * SCAN_GROUP, (gi + 1) * SCAN_GROUP)
                hg = a[rows_g] * state + b[rows_g]
                pieces[gi] = hg
                state = hg[0:1, :] if reverse else hg[SCAN_GROUP - 1:SCAN_GROUP, :]
            h = jnp.concatenate(pieces, axis=0)
            if reverse:
                gate = x_ref[0, pl.ds(start, tc), GROUP_W:2 * GROUP_W]
                tot = hs_ref[pl.ds(start, tc), :] + h
                o_ref[0, pl.ds(start, tc), :] = (jax.nn.gelu(gate) * tot).astype(o_ref.dtype)
                return state
            hs_ref[pl.ds(start, tc), :] = h
            return state

        return lax.fori_loop(0, nc, chunk, h0_ref[0, d])

    hf_ref[0, 0] = direction(0, False)
    hf_ref[0, 1] = direction(1, True)


def _rglru(lru, h0, cw, cb, wai, bai, lam):
    B, n, _ = lru.shape
    tc = min(256, n)
    full = lambda *s: pl.BlockSpec(s, lambda b: (0,) * len(s))
    kern = functools.partial(_lru_kernel, tc=tc, n=n)
    return pl.pallas_call(
        kern,
        out_shape=(jax.ShapeDtypeStruct((B, n, GROUP_W), BF16),
                   jax.ShapeDtypeStruct((B, 2, 1, GROUP_W), F32)),
        grid=(B,),
        in_specs=[pl.BlockSpec((1, n, 512), lambda b: (b, 0, 0)),
                  pl.BlockSpec((1, 2, 1, GROUP_W), lambda b: (b, 0, 0, 0)),
                  full(2, 4, GROUP_W), full(2, 1, GROUP_W), full(2, GROUP_W, 2 * GROUP_W),
                  full(2, 1, 2 * GROUP_W), full(2, 1, GROUP_W)],
        out_specs=(pl.BlockSpec((1, n, GROUP_W), lambda b: (b, 0, 0)),
                   pl.BlockSpec((1, 2, 1, GROUP_W), lambda b: (b, 0, 0, 0))),
        scratch_shapes=[pltpu.VMEM((n, GROUP_W), F32)],
        compiler_params=_cparams(1),
        name="rglru",
    )(lru, h0, cw, cb, wai, bai, lam)


def _sconv_kernel(u_ref, sb_ref, w_ref, b_ref, o_ref, *, tc, n):
    w = w_ref[...]
    bias = b_ref[...]

    def chunk(c, _):
        start = pl.multiple_of(c * tc, tc)
        ucat = _load_halo(u_ref, 0, GROUP_W, start, tc, n)
        y = bias
        for k in range(3):
            y = y + w[k:k + 1, :] * _shift_rows(ucat, k - 1, tc)
        o_ref[0, pl.ds(start, tc), :] = (sb_ref[0, pl.ds(start, tc), :] * y).astype(o_ref.dtype)
        return 0

    lax.fori_loop(0, n // tc, chunk, 0)


def _short_conv(u, sb, w, b):
    B, n, W = u.shape
    tc = min(256, n)
    seq = pl.BlockSpec((1, n, W), lambda i: (i, 0, 0))
    return pl.pallas_call(
        functools.partial(_sconv_kernel, tc=tc, n=n),
        out_shape=jax.ShapeDtypeStruct((B, n, W), BF16),
        grid=(B,),
        in_specs=[seq, seq, pl.BlockSpec((3, W), lambda i: (0, 0)), pl.BlockSpec((1, W), lambda i: (0, 0))],
        out_specs=seq,
        compiler_params=_cparams(1),
        name="short_conv",
    )(u, sb, w, b.reshape(1, W))


def _mla_prep_kernel(x_ref, qg_ref, kg_ref, wq_ref, wk_ref, wv_ref, e2_ref, cos_ref, sin_ref,
                     q_ref, k_ref, v_ref):
    x = x_ref[0]
    cq = x[:, 0:256]
    ckv = x[:, 256:384]
    kr = x[:, 384:448]
    cqn = cq * lax.rsqrt(jnp.mean(cq * cq, axis=-1, keepdims=True) + EPS) * qg_ref[...]
    ckvn = ckv * lax.rsqrt(jnp.mean(ckv * ckv, axis=-1, keepdims=True) + EPS) * kg_ref[...]
    cos = cos_ref[...]
    sin = sin_ref[...]
    q2 = _bdot(cqn, wq_ref[...])
    q = (q2[:, 0:512] * cos + q2[:, 512:1024] * sin) * (ATTN_SCALE * LOG2_E)
    kr2 = _bdot(kr, e2_ref[...])
    ckvb = ckvn.astype(BF16)
    k = _bdot(ckvb, wk_ref[...]) + kr2[:, 0:512] * cos + kr2[:, 512:1024] * sin
    vt = lax.dot_general(wv_ref[...], ckvb, (((1,), (1,)), ((), ())), preferred_element_type=F32)
    tm = x.shape[0]
    ones_rows = jnp.where(lax.broadcasted_iota(jnp.int32, (V_ROWS - MLA_V, tm), 0) == 0, 1.0, 0.0)
    pieces = []
    for h in range(MLA_HEADS):
        q_ref[0, h] = q[:, h * HEAD_PAD:(h + 1) * HEAD_PAD].astype(BF16)
        k_ref[0, h] = k[:, h * HEAD_PAD:(h + 1) * HEAD_PAD].astype(BF16)
        pieces += [vt[h * MLA_V:(h + 1) * MLA_V], ones_rows]
    v_all = jnp.concatenate(pieces, axis=0).astype(BF16)
    tk = v_ref.shape[3]
    for j in range(tm // tk):
        v_ref[0, j] = v_all[:, j * tk:(j + 1) * tk]


def _mla_prep(mla, qg, kg, wq2, wk, wv, e2, cos, sin):
    B, n, _ = mla.shape
    tm = min(512, n)
    tk = min(256, n)
    HW = MLA_HEADS * HEAD_PAD
    full = lambda *s: pl.BlockSpec(s, lambda b, i: (0,) * len(s))
    return pl.pallas_call(
        _mla_prep_kernel,
        out_shape=(jax.ShapeDtypeStruct((B, MLA_HEADS, n, HEAD_PAD), BF16),
                   jax.ShapeDtypeStruct((B, MLA_HEADS, n, HEAD_PAD), BF16),
                   jax.ShapeDtypeStruct((B, n // tk, MLA_HEADS * V_ROWS, tk), BF16)),
        grid=(B, n // tm),
        in_specs=[pl.BlockSpec((1, tm, 512), lambda b, i: (b, i, 0)),
                  full(1, MLA_Q_LORA), full(1, MLA_KV_LORA),
                  full(MLA_Q_LORA, 2 * HW), full(MLA_KV_LORA, HW), full(MLA_HEADS * MLA_V, MLA_KV_LORA),
                  full(2 * MLA_ROPE, 2 * HW),
                  pl.BlockSpec((tm, HW), lambda b, i: (i, 0)),
                  pl.BlockSpec((tm, HW), lambda b, i: (i, 0))],
        out_specs=(pl.BlockSpec((1, MLA_HEADS, tm, HEAD_PAD), lambda b, i: (b, 0, i, 0)),
                   pl.BlockSpec((1, MLA_HEADS, tm, HEAD_PAD), lambda b, i: (b, 0, i, 0)),
                   pl.BlockSpec((1, tm // tk, MLA_HEADS * V_ROWS, tk), lambda b, i: (b, i, 0, 0))),
        compiler_params=_cparams(2),
        name="mla_prep",
    )(mla, qg.reshape(1, -1), kg.reshape(1, -1), wq2, wk, wv, e2, cos, sin)


def _attn_kernel(q_ref, k_ref, vt_ref, o_ref, st_a, st_b, p_a, p_b, *, tk, n_kt):
    tq = q_ref.shape[2]
    H = MLA_HEADS
    qs = [q_ref[0, h] for h in range(H)]

    def scores(j, h):
        kt = k_ref[0, h, pl.ds(pl.multiple_of(j * tk, tk), tk), :]
        return lax.dot_general(kt, qs[h], (((1,), (1,)), ((), ())), preferred_element_type=F32)

    def weighted_values(j, h, p):
        vt = vt_ref[0, j, h * V_ROWS:(h + 1) * V_ROWS, :]
        return jnp.dot(vt, p, preferred_element_type=F32)

    def step(j, st_cur, st_next, p_prev, p_cur, carry):
        a_prev, m, acc = carry
        jn = jnp.minimum(j + 1, n_kt - 1)
        for h in range(H):
            st_next[h] = scores(jn, h)
        pv = [weighted_values(jnp.maximum(j - 1, 0), h, p_prev[h]) for h in range(H)]
        m_new, alpha = [], []
        for h in range(H):
            st = st_cur[h]
            mh = jnp.maximum(m[h], jnp.max(st, axis=0, keepdims=True))
            alpha.append(jnp.exp2(m[h] - mh))
            m_new.append(mh)
            p_cur[h] = jnp.exp2((st - mh).astype(BF16))
        acc_new = [a_prev[h] * acc[h] + pv[h] for h in range(H)]
        return (tuple(alpha), tuple(m_new), tuple(acc_new))

    even = lambda j, c: step(j, st_a, st_b, p_b, p_a, c)
    odd = lambda j, c: step(j, st_b, st_a, p_a, p_b, c)

    for h in range(H):
        st_a[h] = scores(0, h)
    p_b[...] = jnp.zeros(p_b.shape, BF16)
    per_head = lambda f: tuple(f() for _ in range(H))
    carry = (per_head(lambda: jnp.ones((1, tq), F32)),
             per_head(lambda: jnp.full((1, tq), -jnp.inf, F32)),
             per_head(lambda: jnp.zeros((V_ROWS, tq), F32)))
    carry = lax.fori_loop(0, n_kt // 2, lambda i, c: odd(2 * i + 1, even(2 * i, c)), carry)
    if n_kt % 2:
        carry = even(n_kt - 1, carry)
    p_last = p_a if n_kt % 2 else p_b
    a_last, _, acc = carry
    outs = []
    for h in range(H):
        tot = a_last[h] * acc[h] + weighted_values(n_kt - 1, h, p_last[h])
        outs.append(tot[0:MLA_V] / tot[MLA_V:MLA_V + 1])
    o_ref[0] = jnp.concatenate(outs, axis=0).T.astype(o_ref.dtype)


def _attention(q, k, vt):
    B, H, n, _ = q.shape
    M = k.shape[2]
    n_kt, tk = vt.shape[1], vt.shape[3]
    tq = min(256, n)
    return pl.pallas_call(
        functools.partial(_attn_kernel, tk=tk, n_kt=n_kt),
        out_shape=jax.ShapeDtypeStruct((B, n, H * MLA_V), BF16),
        grid=(B, n // tq),
        in_specs=[pl.BlockSpec((1, H, tq, HEAD_PAD), lambda b, i: (b, 0, i, 0)),
                  pl.BlockSpec((1, H, M, HEAD_PAD), lambda b, i: (b, 0, 0, 0)),
                  pl.BlockSpec((1, n_kt, H * V_ROWS, tk), lambda b, i: (b, 0, 0, 0))],
        out_specs=pl.BlockSpec((1, tq, H * MLA_V), lambda b, i: (b, i, 0)),
        scratch_shapes=[pltpu.VMEM((H, tk, tq), F32), pltpu.VMEM((H, tk, tq), F32),
                        pltpu.VMEM((H, tk, tq), BF16), pltpu.VMEM((H, tk, tq), BF16)],
        compiler_params=_cparams(2),
        name="mla_attention",
    )(q, k, vt)


ROW_TILE = 8


def _store_rows_contiguous(ref, lead, x):
    rows = x.shape[0]
    for j in range(ROW_TILE):
        ref[lead + (pl.ds(j, rows, stride=ROW_TILE), slice(None))] = x[:, j * 128:(j + 1) * 128]


def _load_rows_contiguous(ref, lead, rows, dtype):
    return jnp.concatenate([ref[lead + (pl.ds(j, rows, stride=ROW_TILE), slice(None))].astype(dtype)
                            for j in range(ROW_TILE)], axis=-1)


def _outproj_kernel(a_ref, b_ref, c_ref, d_ref, h_ref, w_ref, g1_ref, ng_ref, sh_ref, sc_ref, rw_ref, rb_ref,
                    hn_ref, f_ref, ti_ref, tg_ref):
    mixed = jnp.concatenate([a_ref[0], b_ref[0], c_ref[0], d_ref[0]], axis=-1)
    hn = h_ref[0] + g1_ref[0] * jnp.dot(mixed, w_ref[...], preferred_element_type=F32)
    hn_ref[0] = hn
    y = hn * lax.rsqrt(jnp.mean(hn * hn, axis=-1, keepdims=True) + EPS) * ng_ref[...]
    f = y * (1.0 + sc_ref[0]) + sh_ref[0]
    _store_rows_contiguous(f_ref, (0,), f)
    nt = (((1,), (1,)), ((), ()))
    rw = rw_ref[...]
    rw_hi = rw.astype(BF16)
    rw_lo = (rw - rw_hi.astype(F32)).astype(BF16)
    f_hi = f.astype(BF16)
    f_lo = (f - f_hi.astype(F32)).astype(BF16)
    logits = (lax.dot_general(rw_hi, f_hi, nt, preferred_element_type=F32)
              + (lax.dot_general(rw_hi, f_lo, nt, preferred_element_type=F32)
                 + lax.dot_general(rw_lo, f_hi, nt, preferred_element_type=F32))) + rb_ref[...]
    tm = logits.shape[1]
    sub = lax.broadcasted_iota(jnp.int32, (N_EXPERTS, tm), 0)
    row = lax.broadcasted_iota(jnp.int32, (8, tm), 0)
    top_v = jnp.full((8, tm), -jnp.inf, F32)
    top_i = jnp.zeros((8, tm), jnp.int32)
    cur = logits
    m0 = None
    for kk in range(TOP_K):
        m = jnp.max(cur, axis=0, keepdims=True)
        sel = jnp.min(jnp.where(cur == m, sub, N_EXPERTS), axis=0, keepdims=True)
        if kk == 0:
            m0 = m
        top_v = jnp.where(row == kk, m, top_v)
        top_i = jnp.where(row == kk, sel, top_i)
        cur = jnp.where(sub == sel, -jnp.inf, cur)
    e = jnp.where(row < TOP_K, jnp.exp(top_v - m0), 0.0)
    ti_ref[0] = top_i
    tg_ref[0] = e / jnp.sum(e, axis=0, keepdims=True)


def _out_projection(a, b, c, d, h, w_out, g1, ng, sh, sc, rw, rb):
    B, n, D = h.shape
    tm = min(1024, n)
    tile = lambda w: pl.BlockSpec((1, tm, w), lambda bb, i: (bb, i, 0))
    vec = pl.BlockSpec((1, 1, D), lambda bb, i: (bb, 0, 0))
    full = lambda *s: pl.BlockSpec(s, lambda bb, i: (0,) * len(s))
    return pl.pallas_call(
        _outproj_kernel,
        out_shape=(jax.ShapeDtypeStruct((B, n, D), F32),
                   jax.ShapeDtypeStruct((B, n * ROW_TILE, 128), F32),
                   jax.ShapeDtypeStruct((B, 8, n), jnp.int32),
                   jax.ShapeDtypeStruct((B, 8, n), F32)),
        grid=(B, n // tm),
        in_specs=[tile(256), tile(256), tile(256), tile(256), tile(D),
                  full(D, D), vec, full(1, D), vec, vec, full(N_EXPERTS, D), full(N_EXPERTS, 1)],
        out_specs=(tile(D), pl.BlockSpec((1, tm * ROW_TILE, 128), lambda bb, i: (bb, i, 0)),
                   pl.BlockSpec((1, 8, tm), lambda bb, i: (bb, 0, i)),
                   pl.BlockSpec((1, 8, tm), lambda bb, i: (bb, 0, i))),
        compiler_params=_cparams(2),
        name="outproj_router",
    )(a, b, c, d, h, w_out, g1, ng.reshape(1, D), sh, sc, rw.T, rb.reshape(N_EXPERTS, 1))


def _rank_kernel(ti_ref, rank_ref, cnt_ref, carry_ref, *, tm):
    @pl.when(pl.program_id(0) == 0)
    def _():
        carry_ref[...] = jnp.zeros_like(carry_ref)

    ti = ti_ref[...]
    sub = lax.broadcasted_iota(jnp.int32, (N_EXPERTS, tm), 0)
    hot = jnp.zeros((N_EXPERTS, tm), F32)
    for kk in range(TOP_K):
        hot = hot + jnp.where(sub == ti[kk:kk + 1, :], 1.0, 0.0)
    r = lax.broadcasted_iota(jnp.int32, (tm, tm), 0)
    c = lax.broadcasted_iota(jnp.int32, (tm, tm), 1)
    earlier = jnp.where(r < c, 1.0, 0.0).astype(BF16)
    carry = carry_ref[:, 0:1]
    before = jnp.dot(hot.astype(BF16), earlier, preferred_element_type=F32) + carry
    row = lax.broadcasted_iota(jnp.int32, (8, tm), 0)
    out = jnp.zeros((8, tm), jnp.int32)
    for kk in range(TOP_K):
        rk = jnp.sum(jnp.where(sub == ti[kk:kk + 1, :], before, 0.0), axis=0, keepdims=True)
        out = jnp.where(row == kk, rk.astype(jnp.int32), out)
    rank_ref[...] = out
    total = carry + jnp.sum(hot, axis=1, keepdims=True)
    carry_ref[...] = jnp.broadcast_to(total, carry_ref.shape)
    cnt_ref[...] = jnp.broadcast_to(total, cnt_ref.shape).astype(jnp.int32)


def _expert_ranks(top_i_t):
    T = top_i_t.shape[1]
    tm = 512 if T % 512 == 0 else 256
    return pl.pallas_call(
        functools.partial(_rank_kernel, tm=tm),
        out_shape=(jax.ShapeDtypeStruct((8, T), jnp.int32), jax.ShapeDtypeStruct((N_EXPERTS, 128), jnp.int32)),
        grid=(T // tm,),
        in_specs=[pl.BlockSpec((8, tm), lambda i: (0, i))],
        out_specs=(pl.BlockSpec((8, tm), lambda i: (0, i)), pl.BlockSpec((N_EXPERTS, 128), lambda i: (0, 0))),
        scratch_shapes=[pltpu.VMEM((N_EXPERTS, 128), F32)],
        compiler_params=_cparams(1),
        name="moe_rank",
    )(top_i_t)


def _stream_maps(nc, nl):
    per_batch = nc + nl
    ctx_map = lambda i, *_: ((i // per_batch) * nc + jnp.minimum(i % per_batch, nc - 1), 0)
    lat_map = lambda i, *_: ((i // per_batch) * nl + jnp.maximum(i % per_batch - nc, 0), 0)
    return [ctx_map, lat_map] if nc > 0 else [lat_map]


def _dispatch_kernel(cnt_ref, ps_ref, pe_ref, pos_ref, *refs, tm, n_slots, nc, nl):
    n_in = 2 if nc > 0 else 1
    f_refs = refs[:n_in]
    xs_ref, buf_ref, zrow_ref, sem_ref, zsem_ref = refs[n_in:]
    i = pl.program_id(0)
    nt = pl.num_programs(0)
    copies = tm * TOP_K
    zslots = zrow_ref.shape[0] // ROW_TILE

    def tile_rows(first_row, n_rows):
        return pl.ds(pl.multiple_of(first_row * ROW_TILE, ROW_TILE), n_rows * ROW_TILE)

    def zero_fill(lo, hi, start):
        def chunk(off, size):
            cp = pltpu.make_async_copy(zrow_ref.at[tile_rows(0, size)], xs_ref.at[tile_rows(off, size)], zsem_ref)
            cp.start() if start else cp.wait()

        n = hi - lo

        def whole(j, _):
            chunk(lo + j * zslots, zslots)
            return 0

        lax.fori_loop(0, n // zslots, whole, 0)
        rem_lo = lo + (n // zslots) * zslots
        rem = n % zslots
        size = zslots // 2
        while size >= 1:
            @pl.when((rem & size) != 0)
            def _(size=size):
                chunk(rem_lo + (rem & ~(2 * size - 1)), size)
            size //= 2

    @pl.when(i == 0)
    def _():
        zrow_ref[...] = jnp.zeros_like(zrow_ref)
        for start in (True, False):
            for e in range(N_EXPERTS):
                zero_fill(ps_ref[e] + cnt_ref[e], pe_ref[e], start)
            zero_fill(pe_ref[N_EXPERTS - 1], n_slots, start)

    slot = i % 2
    if nc > 0:
        j = i % (nc + nl)

        @pl.when(j < nc)
        def _():
            buf_ref[slot] = f_refs[0][...]

        @pl.when(j >= nc)
        def _():
            buf_ref[slot] = f_refs[1][...]
    else:
        buf_ref[slot] = f_refs[0][...]

    def issue(t, _):
        for kk in range(TOP_K):
            pltpu.make_async_copy(buf_ref.at[slot, tile_rows(t, 1)],
                                  xs_ref.at[tile_rows(pos_ref[0, 0, t * TOP_K + kk], 1)],
                                  sem_ref.at[slot]).start(priority=kk % 2)
        return 0

    lax.fori_loop(0, tm, issue, 0, unroll=2)

    def wait_rows(s):
        pltpu.make_async_copy(xs_ref.at[tile_rows(0, copies)], xs_ref.at[tile_rows(0, copies)], sem_ref.at[s]).wait()

    @pl.when(i > 0)
    def _():
        wait_rows(1 - slot)

    @pl.when(i == nt - 1)
    def _():
        wait_rows(slot)


def _dispatch(fs, pos, counts, pad_start, pad_end, n_slots, tm, nc, nl):
    nt = pos.shape[0] // (tm * TOP_K)
    grid_spec = pltpu.PrefetchScalarGridSpec(
        num_scalar_prefetch=3,
        grid=(nt,),
        in_specs=[pl.BlockSpec((1, 1, tm * TOP_K), lambda i, *_: (i, 0, 0), memory_space=pltpu.SMEM)]
        + [pl.BlockSpec((tm * ROW_TILE, 128), m) for m in _stream_maps(nc, nl)],
        out_specs=pl.BlockSpec(memory_space=pl.ANY),
        scratch_shapes=[pltpu.VMEM((2, tm * ROW_TILE, 128), F32), pltpu.VMEM((256 * ROW_TILE, 128), F32),
                        pltpu.SemaphoreType.DMA((2,)), pltpu.SemaphoreType.DMA],
    )
    return pl.pallas_call(
        functools.partial(_dispatch_kernel, tm=tm, n_slots=n_slots, nc=nc, nl=nl),
        out_shape=jax.ShapeDtypeStruct((n_slots * ROW_TILE, 128), F32),
        grid_spec=grid_spec,
        compiler_params=_cparams(1),
        name="moe_dispatch",
    )(counts, pad_start, pad_end, pos.reshape(nt, 1, tm * TOP_K), *fs)


def _ffn_kernel(be_ref, bv_ref, nx_ref, par_ref, x_ref, wgu_hbm, bgu_ref, wd_hbm, bd_ref, y_ref,
                wgu_f, wd_f, wgu_s, wd_s, wsem, *, bm, layer):
    i = pl.program_id(0)
    valid = bv_ref[i]

    def weights(e, slot, start):
        for src, dst in ((wgu_hbm, wgu_f), (wd_hbm, wd_f)):
            cp = pltpu.make_async_copy(src.at[layer, e], dst.at[slot], wsem.at[slot])
            cp.start() if start else cp.wait()

    def experts_rows(rows):
        De = wd_s.shape[0]
        x = _load_rows_contiguous(x_ref, (), rows, BF16)
        gu = jnp.dot(x, wgu_s[...], preferred_element_type=F32) + bgu_ref[0, 0]
        g = jnp.minimum(gu[:, 0:De], SWIGLU_LIMIT)
        u = jnp.clip(gu[:, De:2 * De], -SWIGLU_LIMIT, SWIGLU_LIMIT)
        act = (u + 1.0) * (g * jax.nn.sigmoid(SWIGLU_ALPHA * g))
        y = jnp.dot(act.astype(BF16), wd_s[...], preferred_element_type=F32) + bd_ref[0, 0]
        _store_rows_contiguous(y_ref, (), y)
        if rows < bm:
            y_ref[rows * ROW_TILE:bm * ROW_TILE, :] = jnp.zeros(((bm - rows) * ROW_TILE, 128), y_ref.dtype)

    @pl.when(valid > 0)
    def _():
        @pl.when((i == 0) | (be_ref[i] != be_ref[jnp.maximum(i - 1, 0)]))
        def _():
            slot = par_ref[i]

            @pl.when(i == 0)
            def _():
                weights(be_ref[i], slot, True)

            weights(be_ref[i], slot, False)

            @pl.when(nx_ref[i] >= 0)
            def _():
                weights(nx_ref[i], 1 - slot, True)

            wgu_s[...] = wgu_f[slot].astype(BF16)
            wd_s[...] = wd_f[slot].astype(BF16)

        for rows in range(FFN_ROWS, bm + 1, FFN_ROWS):
            pl.when((valid > rows - FFN_ROWS) & (valid <= rows))(functools.partial(experts_rows, rows))

    @pl.when(valid <= 0)
    def _():
        y_ref[...] = jnp.zeros_like(y_ref)


def _expert_ffn(xs, block_exp, block_valid, next_exp, parity, wgu, bgu, wd, bd, layer, bm):
    n_slots = xs.shape[0] // ROW_TILE
    L, E, D, G = wgu.shape
    nb = n_slots // bm
    bmap = lambda i, be, *_: (layer, be[i], 0, 0)
    rows_spec = pl.BlockSpec((bm * ROW_TILE, 128), lambda i, *_: (i, 0))
    grid_spec = pltpu.PrefetchScalarGridSpec(
        num_scalar_prefetch=4,
        grid=(nb,),
        in_specs=[rows_spec,
                  pl.BlockSpec(memory_space=pl.ANY),
                  pl.BlockSpec((1, 1, 1, G), bmap),
                  pl.BlockSpec(memory_space=pl.ANY),
                  pl.BlockSpec((1, 1, 1, D), bmap)],
        out_specs=rows_spec,
        scratch_shapes=[pltpu.VMEM((2, D, G), F32), pltpu.VMEM((2, G // 2, D), F32),
                        pltpu.VMEM((D, G), BF16), pltpu.VMEM((G // 2, D), BF16),
                        pltpu.SemaphoreType.DMA((2,))],
    )
    return pl.pallas_call(
        functools.partial(_ffn_kernel, bm=bm, layer=layer),
        out_shape=jax.ShapeDtypeStruct((n_slots * ROW_TILE, 128), F32),
        grid_spec=grid_spec,
        compiler_params=_cparams(1),
        name="moe_expert_ffn",
    )(block_exp, block_valid, next_exp, parity, xs, wgu, bgu.reshape(L, E, 1, G), wd, bd.reshape(L, E, 1, D))


def _combine_kernel(pos_ref, pos_next_ref, y_ref, *refs, tm, nc, nl, final_norm):
    n_s = 2 if nc > 0 else 1
    h_refs = refs[:n_s]
    g2_ref, gate_ref, fg_ref = refs[n_s:n_s + 3]
    o_refs = refs[n_s + 3:2 * n_s + 3]
    buf_ref, sem_ref = refs[2 * n_s + 3:]
    i = pl.program_id(0)
    nt = pl.num_programs(0)
    slot = i % 2

    def tile_rows(first_row):
        return pl.ds(pl.multiple_of(first_row * ROW_TILE, ROW_TILE), ROW_TILE)

    def gather_tile(p_ref, s):
        def issue(r, _):
            for kk in range(TOP_K):
                pltpu.make_async_copy(y_ref.at[tile_rows(p_ref[0, 0, r * TOP_K + kk])],
                                      buf_ref.at[s, kk, tile_rows(r)], sem_ref.at[s]).start(priority=kk % 2)
            return 0

        lax.fori_loop(0, tm, issue, 0, unroll=2)

    @pl.when(i == 0)
    def _():
        gather_tile(pos_ref, 0)

    @pl.when(i + 1 < nt)
    def _():
        gather_tile(pos_next_ref, 1 - slot)

    pltpu.make_async_copy(buf_ref.at[slot], buf_ref.at[slot], sem_ref.at[slot]).wait()
    gate = gate_ref[...]
    acc = gate[:, 0:1] * _load_rows_contiguous(buf_ref, (slot, 0), tm, F32)
    for kk in range(1, TOP_K):
        acc = acc + gate[:, kk:kk + 1] * _load_rows_contiguous(buf_ref, (slot, kk), tm, F32)
    moe = g2_ref[0] * acc

    def finish(h_ref, o_ref):
        out = h_ref[...] + moe
        if final_norm:
            out = out * lax.rsqrt(jnp.mean(out * out, axis=-1, keepdims=True) + EPS) * fg_ref[...]
        o_ref[...] = out

    if nc > 0:
        j = i % (nc + nl)
        pl.when(j < nc)(lambda: finish(h_refs[0], o_refs[0]))
        pl.when(j >= nc)(lambda: finish(h_refs[1], o_refs[1]))
    else:
        finish(h_refs[0], o_refs[0])


def _combine(y, pos, hs, g2_tiles, gates, final_g, tm, nc, nl, final_norm):
    D = hs[0].shape[1]
    nt = pos.shape[0] // (tm * TOP_K)
    maps = _stream_maps(nc, nl)
    streams = [pl.BlockSpec((tm, D), m) for m in maps]
    pos3 = pos.reshape(nt, 1, tm * TOP_K)
    outs = pl.pallas_call(
        functools.partial(_combine_kernel, tm=tm, nc=nc, nl=nl, final_norm=final_norm),
        out_shape=[jax.ShapeDtypeStruct(h.shape, F32) for h in hs],
        grid=(nt,),
        in_specs=[pl.BlockSpec((1, 1, tm * TOP_K), lambda i: (i, 0, 0), memory_space=pltpu.SMEM),
                  pl.BlockSpec((1, 1, tm * TOP_K), lambda i: (jnp.minimum(i + 1, nt - 1), 0, 0),
                               memory_space=pltpu.SMEM),
                  pl.BlockSpec(memory_space=pl.ANY)]
        + streams
        + [pl.BlockSpec((1, 1, D), lambda i: (i, 0, 0)),
           pl.BlockSpec((tm, 8), lambda i: (i, 0)),
           pl.BlockSpec((1, D), lambda i: (0, 0))],
        out_specs=streams,
        scratch_shapes=[pltpu.VMEM((2, TOP_K, tm * ROW_TILE, 128), F32), pltpu.SemaphoreType.DMA((2,))],
        compiler_params=_cparams(1),
        name="moe_combine",
    )(pos3, pos3, y, *hs, g2_tiles, gates, final_g.reshape(1, D))
    return outs


def _slots_kernel(ps_ref, ti_ref, rank_ref, pos_ref):
    ti = ti_ref[...]
    start = jnp.zeros(ti.shape, jnp.int32)
    for e in range(N_EXPERTS):
        start = jnp.where(ti == e, ps_ref[e], start)
    pos_ref[...] = start + rank_ref[...]


def _slots(pad_start, top_i_t, rank_t):
    shape = top_i_t.shape
    whole = pl.BlockSpec(shape, lambda i, ps: (0, 0))
    return pl.pallas_call(
        _slots_kernel,
        out_shape=jax.ShapeDtypeStruct(shape, jnp.int32),
        grid_spec=pltpu.PrefetchScalarGridSpec(num_scalar_prefetch=1, grid=(1,), in_specs=[whole, whole],
                                               out_specs=whole),
        compiler_params=_cparams(1),
        name="moe_slots",
    )(pad_start, top_i_t, rank_t)


def _moe(fs, hs, top_i_t, top_g_t, g2_tiles, wgu, bgu, wd, bd, layer, final_g, tm, nc, nl, final_norm):
    T = top_i_t.shape[1]
    bm = MOE_BM
    n_asg = T * TOP_K
    rank_t, cnt = _expert_ranks(top_i_t)
    counts = cnt[:, 0]
    padded = (counts + bm - 1) // bm * bm
    pad_end = jnp.cumsum(padded).astype(jnp.int32)
    pad_start = pad_end - padded
    pos = _slots(pad_start, top_i_t, rank_t)[0:TOP_K].T.reshape(-1)
    nb = (n_asg + N_EXPERTS * (bm - 1) + bm - 1) // bm
    n_slots = nb * bm
    starts = jnp.arange(nb, dtype=jnp.int32) * bm
    block_exp = jnp.minimum(jnp.sum((pad_end[None, :] <= starts[:, None]).astype(jnp.int32), axis=1), N_EXPERTS - 1)
    block_valid = jnp.clip(pad_start[block_exp] + counts[block_exp] - starts, 0, bm)
    block_valid = jnp.where(starts < pad_end[-1], block_valid, 0).astype(jnp.int32)
    ids = jnp.arange(N_EXPERTS, dtype=jnp.int32)
    used = counts > 0
    later_used = used[None, :] & (ids[None, :] > ids[:, None])
    next_of = jnp.min(jnp.where(later_used, ids[None, :], N_EXPERTS), axis=1)
    next_of = jnp.where(next_of < N_EXPERTS, next_of, -1).astype(jnp.int32)
    order_of = (jnp.cumsum(used.astype(jnp.int32)) - 1).astype(jnp.int32)
    hot = (block_exp[:, None] == ids[None, :]).astype(jnp.int32)
    next_exp = jnp.sum(hot * next_of[None, :], axis=1).astype(jnp.int32)
    parity = (jnp.sum(hot * order_of[None, :], axis=1) % 2).astype(jnp.int32)

    xs = _dispatch(fs, pos, counts, pad_start, pad_end, n_slots, tm, nc, nl)
    y = _expert_ffn(xs, block_exp, block_valid, next_exp, parity, wgu, bgu, wd, bd, layer, bm)
    return _combine(y, pos, hs, g2_tiles, top_g_t.T, final_g, tm, nc, nl, final_norm)


_ROPE_SWAP = np.concatenate([np.arange(8, 16), np.arange(0, 8), np.arange(24, 32), np.arange(16, 24)])


def _dft_tables(n):
    hi = n // 64
    j = jnp.arange(n, dtype=jnp.int32)[:, None]
    k = jnp.arange(64, dtype=jnp.int32)[None, :]
    k2 = jnp.arange(hi, dtype=jnp.int32)[None, :]
    alpha = ((j * k2) % hi).astype(F32) * (2.0 * math.pi / hi)
    beta = ((j * k) % n).astype(F32) * (2.0 * math.pi / n)
    ca, sa, cb, sb = lax.optimization_barrier((jnp.cos(alpha), jnp.sin(alpha), jnp.cos(beta), jnp.sin(beta)))
    ca, sa = ca[:, :, None], sa[:, :, None]
    cb, sb = cb[:, None, :], sb[:, None, :]
    scale = 1.0 / math.sqrt(n)
    cn = ((ca * cb - sa * sb) * scale).reshape(n, n).astype(BF16)
    sn = ((sa * cb + ca * sb) * scale).reshape(n, n).astype(BF16)
    return cn, sn


def _channel_dft():
    j = jnp.arange(GROUP_W, dtype=jnp.int32)
    same_group = (j[:, None] // FNET_CH) == (j[None, :] // FNET_CH)
    ang = ((j[:, None] * j[None, :]) % FNET_CH).astype(F32) * (2.0 * math.pi / FNET_CH)
    scale = 1.0 / math.sqrt(FNET_CH)
    c = jnp.where(same_group, jnp.cos(ang) * scale, 0.0)
    s = jnp.where(same_group, jnp.sin(ang) * scale, 0.0)
    return jnp.concatenate([c, s], axis=1).astype(BF16)


def _rope_tables(n, rotary):
    HW = MLA_HEADS * HEAD_PAD
    cos_blk = np.zeros((HEAD_PAD,), np.float32)
    cos_blk[0:MLA_NOPE + MLA_ROPE] = 1.0
    if not rotary:
        return (jnp.broadcast_to(jnp.asarray(np.tile(cos_blk, MLA_HEADS)), (n, HW)),
                jnp.zeros((n, HW), F32))
    half = MLA_ROPE // 2
    inv = ROPE_BASE ** (-jnp.arange(0, half, 2, dtype=F32) / half)
    t = jnp.arange(n, dtype=jnp.int32)
    row = (t // GRID_W).astype(F32)
    col = (t % GRID_W).astype(F32)
    ar = row[:, None] * inv[None, :]
    ac = col[:, None] * inv[None, :]
    cos32 = jnp.concatenate([jnp.cos(ar), jnp.cos(ar), jnp.cos(ac), jnp.cos(ac)], axis=1)
    sin32 = jnp.concatenate([-jnp.sin(ar), jnp.sin(ar), -jnp.sin(ac), jnp.sin(ac)], axis=1)
    ones = jnp.ones((n, MLA_NOPE), F32)
    zeros_n = jnp.zeros((n, MLA_NOPE), F32)
    zeros_p = jnp.zeros((n, HEAD_PAD - MLA_NOPE - MLA_ROPE), F32)
    cos_h = jnp.concatenate([ones, cos32, zeros_p], axis=1)
    sin_h = jnp.concatenate([zeros_n, sin32, zeros_p], axis=1)
    return jnp.tile(cos_h, (1, MLA_HEADS)), jnp.tile(sin_h, (1, MLA_HEADS))


def _relayout_w_in(w_in):
    D = w_in.shape[0]
    krope = w_in[:, 1152:1184]
    return jnp.concatenate([w_in[:, 0:1184], krope[:, _ROPE_SWAP], jnp.zeros((D, 64), w_in.dtype),
                            w_in[:, 1184:1952]], axis=1).astype(BF16)


def _relayout_mla(w_uq, w_ukv):
    dq = MLA_NOPE + MLA_ROPE
    q_main, q_swap, k_cols, v_cols = [], [], [], []
    zq = jnp.zeros((MLA_Q_LORA, HEAD_PAD - dq), w_uq.dtype)
    zn = jnp.zeros((MLA_Q_LORA, MLA_NOPE), w_uq.dtype)
    zk = jnp.zeros((MLA_KV_LORA, HEAD_PAD - MLA_NOPE), w_ukv.dtype)
    for h in range(MLA_HEADS):
        wq_h = w_uq[:, h * dq:(h + 1) * dq]
        rope = wq_h[:, MLA_NOPE:]
        q_main += [wq_h, zq]
        q_swap += [zn, rope[:, _ROPE_SWAP], zq]
        kv_h = w_ukv[:, h * (MLA_NOPE + MLA_V):(h + 1) * (MLA_NOPE + MLA_V)]
        k_cols += [kv_h[:, 0:MLA_NOPE], zk]
        v_cols += [kv_h[:, MLA_NOPE:]]
    wq2 = jnp.concatenate(q_main + q_swap, axis=1).astype(BF16)
    wk = jnp.concatenate(k_cols, axis=1).astype(BF16)
    wv = jnp.concatenate(v_cols, axis=1).T.astype(BF16)
    HW = MLA_HEADS * HEAD_PAD
    e2 = np.zeros((2 * MLA_ROPE, 2 * HW), np.float32)
    for h in range(MLA_HEADS):
        for jj in range(MLA_ROPE):
            e2[jj, h * HEAD_PAD + MLA_NOPE + jj] = 1.0
            e2[MLA_ROPE + jj, HW + h * HEAD_PAD + MLA_NOPE + jj] = 1.0
    return wq2, wk, wv, jnp.asarray(e2, BF16)


def _block_diag_heads(w):
    H, d, _ = w.shape
    eye = jnp.eye(H, dtype=w.dtype)
    return (eye[:, None, :, None] * w[:, :, None, :]).reshape(H * d, H * d)


def kernel(x, c, ctx, c_ctx, mod_w, mod_b, norm1_g, norm2_g, w_in, lru_conv_w, lru_conv_b, lru_w_a, lru_b_a, lru_w_i, lru_b_i, lru_lambda, mla_q_norm_g, mla_w_uq, mla_kv_norm_g, mla_w_ukv, sc_conv_w, sc_conv_b, w_out, router_w, router_b, exp_w_gu, exp_b_gu, exp_w_down, exp_b_down, final_g):
    B, n, D = x.shape
    n_ctx = ctx.shape[1]
    depth = mod_w.shape[0]
    tm = min(256, n, n_ctx)

    rows = 8 * ((B + 1 + 7) // 8)
    c_all = jnp.concatenate([c, c_ctx[None, :], jnp.zeros((rows - B - 1, D), F32)], axis=0)
    mod = _modulation(c_all, mod_w, mod_b)

    cs = _channel_dft()
    cn_lat, sn_lat = _dft_tables(n)
    cn_ctx, sn_ctx = _dft_tables(n_ctx)
    cos_lat, sin_lat = _rope_tables(n, True)
    cos_ctx, sin_ctx = _rope_tables(n_ctx, False)

    h_lat, h_ctx = x, ctx
    for l in range(depth):
        with_ctx_out = l < depth - 1
        ml = mod[l, 0:B].reshape(B, 1, 6, D)
        mc = jnp.broadcast_to(mod[l, B:B + 1].reshape(1, 1, 6, D), (B, 1, 6, D))
        part = lambda m, i: m[:, :, i, :]
        w_in_p = _relayout_w_in(w_in[l])
        wq2, wk, wv, e2 = _relayout_mla(mla_w_uq[l], mla_w_ukv[l])
        wai = jnp.stack([jnp.concatenate([_block_diag_heads(lru_w_a[l, d]), _block_diag_heads(lru_w_i[l, d])], axis=1)
                         for d in range(2)]).astype(BF16)
        bai = jnp.concatenate([lru_b_a[l], lru_b_i[l]], axis=1)[:, None, :]
        lru_args = (lru_conv_w[l], lru_conv_b[l][:, None, :], wai, bai, lru_lambda[l][:, None, :])
        w_out_b = w_out[l].astype(BF16)

        zf_c, lru_c, mla_c, u_c, sb_c = _in_projection(h_ctx, norm1_g[l], part(mc, 0), part(mc, 1), w_in_p, cs)
        zf_l, lru_l, mla_l, u_l, sb_l = _in_projection(h_lat, norm1_g[l], part(ml, 0), part(ml, 1), w_in_p, cs)

        h0 = jnp.zeros((B, 2, 1, GROUP_W), F32)
        b_ctx, s_ctx = _rglru(lru_c, h0, *lru_args)
        b_lat, _ = _rglru(lru_l, s_ctx, *lru_args)

        q_c, k_c, v_c = _mla_prep(mla_c, mla_q_norm_g[l], mla_kv_norm_g[l], wq2, wk, wv, e2, cos_ctx, sin_ctx)
        q_l, k_l, v_l = _mla_prep(mla_l, mla_q_norm_g[l], mla_kv_norm_g[l], wq2, wk, wv, e2, cos_lat, sin_lat)
        k_all = jnp.concatenate([k_c, k_l], axis=2)
        v_all = jnp.concatenate([v_c, v_l], axis=1)
        c_lat = _attention(q_l, k_all, v_all)

        a_lat = _fourier_positions(zf_l, cn_lat, sn_lat)
        d_lat = _short_conv(u_l, sb_l, sc_conv_w[l], sc_conv_b[l])

        hn_lat, f_lat, ti_lat, tg_lat = _out_projection(
            a_lat, b_lat, c_lat, d_lat, h_lat, w_out_b, part(ml, 2), norm2_g[l], part(ml, 3), part(ml, 4),
            router_w[l], router_b[l])

        moe_w = (exp_w_gu, exp_b_gu, exp_w_down, exp_b_down, l)
        tokens_on_lanes = lambda u: jnp.transpose(u, (1, 0, 2)).reshape(8, -1)
        rows = lambda u, w: u.reshape(-1, w)
        g2_lat = part(ml, 5)
        if with_ctx_out:
            a_ctx = _fourier_positions(zf_c, cn_ctx, sn_ctx)
            c_ctx_out = _attention(q_c, k_c, v_c)
            d_ctx = _short_conv(u_c, sb_c, sc_conv_w[l], sc_conv_b[l])
            hn_ctx, f_ctx, ti_ctx, tg_ctx = _out_projection(
                a_ctx, b_ctx, c_ctx_out, d_ctx, h_ctx, w_out_b, part(mc, 2), norm2_g[l], part(mc, 3), part(mc, 4),
                router_w[l], router_b[l])
            S = n_ctx + n
            g2_rows = jnp.concatenate([jnp.broadcast_to(part(mc, 5), (B, n_ctx // tm, D)),
                                       jnp.broadcast_to(g2_lat, (B, n // tm, D))], axis=1)
            cat_t = lambda u, v: tokens_on_lanes(jnp.concatenate([u, v], axis=2))
            h_ctx, h_lat = _moe([rows(f_ctx, 128), rows(f_lat, 128)], [rows(hn_ctx, D), rows(hn_lat, D)],
                                cat_t(ti_ctx, ti_lat), cat_t(tg_ctx, tg_lat),
                                g2_rows.reshape(B * S // tm, 1, D), *moe_w, final_g, tm, n_ctx // tm, n // tm, False)
            h_ctx, h_lat = h_ctx.reshape(B, n_ctx, D), h_lat.reshape(B, n, D)
        else:
            g2_rows = jnp.broadcast_to(g2_lat, (B, n // tm, D)).reshape(B * n // tm, 1, D)
            (h_lat,) = _moe([rows(f_lat, 128)], [rows(hn_lat, D)],
                            tokens_on_lanes(ti_lat), tokens_on_lanes(tg_lat), g2_rows,
                            *moe_w, final_g, tm, 0, n // tm, True)
            h_lat = h_lat.reshape(B, n, D)

    return h_lat
```
